```python
import jax, jax.numpy as jnp
from jax import lax
import numpy as np

D_MODEL = 2048
BATCH = 8
SEQ = 2048
DEPTH = 2

HEAD_DIM = 128
A_WIDTH = D_MODEL // 2
A_HEADS = A_WIDTH // HEAD_DIM
MOBA_BLOCK = 256
MOBA_TOPK = 3
Q_CHUNK = 128
ROPE_THETA = 500000.0
ROPE_DIM = HEAD_DIM // 4
ATTN_SCALE = HEAD_DIM ** -0.5
G_VAL_WIDTH = D_MODEL - A_WIDTH
G_HEADS = 4
G_VAL_DIM = G_VAL_WIDTH // G_HEADS
G_KEY_WIDTH = G_VAL_WIDTH // 2
G_KEY_DIM = G_KEY_WIDTH // G_HEADS
GATE_RANK = 16
GATE_TAU = 16.0
GLA_CHUNK = 64
MIX_WIDTH = A_WIDTH + G_VAL_WIDTH
IN_SIZES = [A_WIDTH, A_WIDTH, A_WIDTH, G_KEY_WIDTH, G_KEY_WIDTH, G_VAL_WIDTH, G_VAL_WIDTH, GATE_RANK]
IN_WIDTH = sum(IN_SIZES)
IN_SPLITS = np.cumsum(IN_SIZES)[:-1].tolist()
N_GROUPS = 4
EXPERTS_PER_GROUP = 8
N_EXPERTS = N_GROUPS * EXPERTS_PER_GROUP
TOP_K = 2
D_EXPERT = D_MODEL // 2
MOE_BLOCK = 128
EPS = 1e-6

kernel_name = 'hymba_moba_gla_hmoe_adaln'


def rms_norm(x, g):
    xf = x.astype(jnp.float32)
    y = xf * lax.rsqrt(jnp.mean(xf * xf, axis=-1, keepdims=True) + EPS)
    return (y * g.astype(jnp.float32)).astype(x.dtype)


def partial_rotary(x):
    S = x.shape[1]
    half = ROPE_DIM // 2
    inv = jnp.power(ROPE_THETA, -jnp.arange(half, dtype=jnp.float32) * (2.0 / ROPE_DIM))
    ang = jnp.arange(S, dtype=jnp.float32)[:, None] * inv[None, :]
    cos = jnp.cos(ang)[None, :, None, :]
    sin = jnp.sin(ang)[None, :, None, :]
    xr = x[..., :ROPE_DIM].astype(jnp.float32)
    x1, x2 = xr[..., :half], xr[..., half:]
    rot = jnp.concatenate([x1 * cos - x2 * sin, x2 * cos + x1 * sin], axis=-1).astype(x.dtype)
    return jnp.concatenate([rot, x[..., ROPE_DIM:]], axis=-1)


def moba_sequence(q, k, v):
    H, S, Dh = q.shape
    n_blk = max(-(-S // MOBA_BLOCK), MOBA_TOPK)
    pad = n_blk * MOBA_BLOCK - S
    k_p = jnp.pad(k, ((0, 0), (0, pad), (0, 0)))
    v_p = jnp.pad(v, ((0, 0), (0, pad), (0, 0)))
    kb = k_p.reshape(H, n_blk, MOBA_BLOCK, Dh)
    vb = v_p.reshape(H, n_blk, MOBA_BLOCK, Dh)
    kmean = jnp.mean(kb.astype(jnp.float32), axis=2).astype(k.dtype)
    n_chunks = S // Q_CHUNK

    def chunk(ci):
        q0 = ci * Q_CHUNK
        qc = lax.dynamic_slice_in_dim(q, q0, Q_CHUNK, axis=1)
        qpos = q0 + jnp.arange(Q_CHUNK)
        blk = q0 // MOBA_BLOCK
        gate = jnp.einsum('hqd,hnd->hqn', qc, kmean).astype(jnp.float32)
        past = jnp.arange(n_blk) < blk
        gate = jnp.where(past[None, None, :], gate, -jnp.inf)
        _, sel = lax.top_k(gate, MOBA_TOPK)
        valid = jnp.arange(MOBA_TOPK) < blk
        ksel = jax.vmap(lambda kh, ih: kh[ih])(kb, sel)
        vsel = jax.vmap(lambda vh, ih: vh[ih])(vb, sel)
        s_sel = jnp.einsum('hqd,hqkld->hqkl', qc, ksel).astype(jnp.float32)
        s_sel = jnp.where(valid[None, None, :, None], s_sel, -jnp.inf)
        s_sel = s_sel.reshape(H, Q_CHUNK, MOBA_TOPK * MOBA_BLOCK)
        kown = lax.dynamic_slice_in_dim(k_p, blk * MOBA_BLOCK, MOBA_BLOCK, axis=1)
        vown = lax.dynamic_slice_in_dim(v_p, blk * MOBA_BLOCK, MOBA_BLOCK, axis=1)
        kpos = blk * MOBA_BLOCK + jnp.arange(MOBA_BLOCK)
        s_own = jnp.einsum('hqd,hld->hql', qc, kown).astype(jnp.float32)
        s_own = jnp.where(kpos[None, None, :] <= qpos[None, :, None], s_own, -jnp.inf)
        p = jax.nn.softmax(jnp.concatenate([s_sel, s_own], axis=-1), axis=-1).astype(v.dtype)
        p_sel = p[..., :MOBA_TOPK * MOBA_BLOCK].reshape(H, Q_CHUNK, MOBA_TOPK, MOBA_BLOCK)
        p_own = p[..., MOBA_TOPK * MOBA_BLOCK:]
        return (jnp.einsum('hqkl,hqkld->hqd', p_sel, vsel)
                + jnp.einsum('hql,hld->hqd', p_own, vown))

    out = lax.map(chunk, jnp.arange(n_chunks))
    return out.transpose(1, 0, 2, 3).reshape(H, S, Dh)


def moba_attention(q, k, v):
    return lax.map(lambda a: moba_sequence(a[0], a[1], a[2]), (q, k, v))


def gla_chunked(q, k, v, log_a):
    B, S, H, dk = q.shape
    dv = v.shape[-1]
    nc = S // GLA_CHUNK

    def blocks(t):
        return t.reshape(B, nc, GLA_CHUNK, H, t.shape[-1]).transpose(0, 3, 1, 2, 4)

    q, k, v, la = blocks(q), blocks(k), blocks(v), blocks(log_a)
    b = jnp.cumsum(la, axis=3)
    b_last = b[:, :, :, -1:, :]
    q_dec = q * jnp.exp(b)
    k_inv = k * jnp.exp(-b)
    k_end = k * jnp.exp(b_last - b)
    causal = jnp.tril(jnp.ones((GLA_CHUNK, GLA_CHUNK), dtype=bool))
    att = jnp.einsum('bhnte,bhnse->bhnts', q_dec, k_inv)
    att = jnp.where(causal, att, 0.0)
    o_intra = jnp.einsum('bhnts,bhnsv->bhntv', att, v)
    kv = jnp.einsum('bhnse,bhnsv->bhnev', k_end, v)
    decay = jnp.exp(b_last[:, :, :, 0, :])

    def step(state, inp):
        d, kvc = inp
        return d[..., None] * state + kvc, state

    init = jnp.zeros((B, H, dk, dv), jnp.float32)
    _, s_prev = lax.scan(step, init, (decay.transpose(2, 0, 1, 3), kv.transpose(2, 0, 1, 3, 4)))
    s_prev = s_prev.transpose(1, 2, 0, 3, 4)
    o_inter = jnp.einsum('bhnte,bhnev->bhntv', q_dec, s_prev)
    return (o_intra + o_inter).transpose(0, 2, 3, 1, 4).reshape(B, S, H, dv)


def gla_mixer(q, k, v, og, ga, w_gk, b_gk, g_onorm):
    B, S, _ = q.shape
    f32 = jnp.float32
    q = q.reshape(B, S, G_HEADS, G_KEY_DIM).astype(f32) * (G_KEY_DIM ** -0.5)
    k = k.reshape(B, S, G_HEADS, G_KEY_DIM).astype(f32)
    v = v.reshape(B, S, G_HEADS, G_VAL_DIM).astype(f32)
    z = (ga @ w_gk + b_gk).astype(f32).reshape(B, S, G_HEADS, G_KEY_DIM)
    log_a = jax.nn.log_sigmoid(z) / GATE_TAU
    o = gla_chunked(q, k, v, log_a)
    o = o * lax.rsqrt(jnp.mean(o * o, axis=-1, keepdims=True) + EPS) * g_onorm.astype(f32)
    o = o * jax.nn.silu(og.reshape(B, S, G_HEADS, G_VAL_DIM).astype(f32))
    return o.reshape(B, S, G_VAL_WIDTH).astype(og.dtype)


def hier_moe(h, w_r1, b_r1, w_r2, b_r2, w_gate, w_up, w_down):
    B, S, D = h.shape
    T = B * S
    t = h.reshape(T, D)
    tf = t.astype(jnp.float32)
    l1 = tf @ w_r1.astype(jnp.float32) + b_r1.astype(jnp.float32)
    p1 = jax.nn.softmax(l1, axis=-1)
    grp = jnp.argmax(l1, axis=-1)
    p_grp = jnp.take_along_axis(p1, grp[:, None], axis=-1)
    l2 = (tf @ w_r2.astype(jnp.float32) + b_r2.astype(jnp.float32)).reshape(T, N_GROUPS, EXPERTS_PER_GROUP)
    l2g = jnp.take_along_axis(l2, grp[:, None, None], axis=1)[:, 0]
    v2, i2 = lax.top_k(l2g, TOP_K)
    p2 = jax.nn.softmax(v2, axis=-1)
    weights = (p_grp * p2).reshape(-1)
    experts = (grp[:, None] * EXPERTS_PER_GROUP + i2).reshape(-1).astype(jnp.int32)
    tokens = jnp.repeat(jnp.arange(T, dtype=jnp.int32), TOP_K)
    N = T * TOP_K
    counts = jnp.zeros((N_EXPERTS,), jnp.int32).at[experts].add(1)
    padded = (counts + MOE_BLOCK - 1) // MOE_BLOCK * MOE_BLOCK
    ends = jnp.cumsum(padded)
    starts = ends - padded
    cstart = jnp.cumsum(counts) - counts
    order = jnp.argsort(experts)
    e_sorted = experts[order]
    rank = jnp.arange(N, dtype=jnp.int32) - cstart[e_sorted]
    dest = starts[e_sorted] + rank
    P = -(-N // MOE_BLOCK) * MOE_BLOCK + N_EXPERTS * MOE_BLOCK
    n_blocks = P // MOE_BLOCK
    slot_tok = jnp.full((P,), T, jnp.int32).at[dest].set(tokens[order])
    slot_w = jnp.zeros((P,), h.dtype).at[dest].set(weights[order].astype(h.dtype))
    block_exp = jnp.minimum(jnp.searchsorted(ends, jnp.arange(n_blocks) * MOE_BLOCK, side='right'),
                            N_EXPERTS - 1).astype(jnp.int32)
    t_pad = jnp.concatenate([t, jnp.zeros((1, D), t.dtype)], axis=0)
    xb = t_pad[slot_tok].reshape(n_blocks, MOE_BLOCK, D)

    def run(args):
        xblk, e = args
        return (jax.nn.silu(xblk @ w_gate[e]) * (xblk @ w_up[e])) @ w_down[e]

    yb = lax.map(run, (xb, block_exp)).reshape(P, D)
    y = jax.ops.segment_sum(yb * slot_w[:, None], slot_tok, num_segments=T + 1)[:T]
    return y.reshape(B, S, D)


def setup_inputs(seed: int = 0) -> dict:
    key = jax.random.key(seed)
    ks = jax.random.split(key, 20)
    f32 = jnp.float32

    def nrm(k, shape, scale):
        return jax.random.normal(k, shape, f32) * scale

    return {
        'x': nrm(ks[0], (BATCH, SEQ, D_MODEL), 1.0),
        'c': nrm(ks[1], (BATCH, D_MODEL), 1.0),
        'ln1': 1.0 + nrm(ks[2], (DEPTH, D_MODEL), 0.02),
        'ln2': 1.0 + nrm(ks[3], (DEPTH, D_MODEL), 0.02),
        'w_ada': nrm(ks[4], (DEPTH, D_MODEL, 6 * D_MODEL), 0.5 * D_MODEL ** -0.5),
        'b_ada': nrm(ks[5], (DEPTH, 6 * D_MODEL), 0.02),
        'w_in': nrm(ks[6], (DEPTH, D_MODEL, IN_WIDTH), D_MODEL ** -0.5),
        'w_gk': nrm(ks[7], (DEPTH, GATE_RANK, G_KEY_WIDTH), GATE_RANK ** -0.5),
        'b_gk': nrm(ks[8], (DEPTH, G_KEY_WIDTH), 0.1),
        'g_onorm': 1.0 + nrm(ks[9], (DEPTH, G_VAL_DIM), 0.02),
        'w_out': nrm(ks[10], (DEPTH, MIX_WIDTH, D_MODEL), MIX_WIDTH ** -0.5),
        'w_r1': nrm(ks[11], (DEPTH, D_MODEL, N_GROUPS), D_MODEL ** -0.5),
        'b_r1': nrm(ks[12], (DEPTH, N_GROUPS), 0.01),
        'w_r2': nrm(ks[13], (DEPTH, D_MODEL, N_EXPERTS), D_MODEL ** -0.5),
        'b_r2': nrm(ks[14], (DEPTH, N_EXPERTS), 0.01),
        'w_e_gate': nrm(ks[15], (DEPTH, N_EXPERTS, D_MODEL, D_EXPERT), D_MODEL ** -0.5),
        'w_e_up': nrm(ks[16], (DEPTH, N_EXPERTS, D_MODEL, D_EXPERT), D_MODEL ** -0.5),
        'w_e_down': nrm(ks[17], (DEPTH, N_EXPERTS, D_EXPERT, D_MODEL), D_EXPERT ** -0.5),
        'ln_f': 1.0 + nrm(ks[18], (D_MODEL,), 0.02),
    }


def reference(x, c, ln1, ln2, w_ada, b_ada, w_in, w_gk, b_gk, g_onorm, w_out,
              w_r1, b_r1, w_r2, b_r2, w_e_gate, w_e_up, w_e_down, ln_f):
    B, S, D = x.shape
    c_act = jax.nn.silu(c)
    for l in range(DEPTH):
        mod = (c_act @ w_ada[l] + b_ada[l])[:, None, :]
        sh1, sc1, gt1, sh2, sc2, gt2 = jnp.split(mod, 6, axis=-1)
        h = rms_norm(x, ln1[l]) * (1.0 + sc1) + sh1
        proj = h @ w_in[l]
        qa, ka, va, qg, kg, vg, og, ga = jnp.split(proj, IN_SPLITS, axis=-1)
        qa = partial_rotary(qa.reshape(B, S, A_HEADS, HEAD_DIM)) * ATTN_SCALE
        ka = partial_rotary(ka.reshape(B, S, A_HEADS, HEAD_DIM))
        va = va.reshape(B, S, A_HEADS, HEAD_DIM)
        oa = moba_attention(qa.transpose(0, 2, 1, 3), ka.transpose(0, 2, 1, 3), va.transpose(0, 2, 1, 3))
        oa = oa.transpose(0, 2, 1, 3).reshape(B, S, A_WIDTH)
        ob = gla_mixer(qg, kg, vg, og, ga, w_gk[l], b_gk[l], g_onorm[l])
        mix = jnp.concatenate([oa, ob], axis=-1) @ w_out[l]
        x = x + gt1 * mix
        h2 = rms_norm(x, ln2[l]) * (1.0 + sc2) + sh2
        x = x + gt2 * hier_moe(h2, w_r1[l], b_r1[l], w_r2[l], b_r2[l], w_e_gate[l], w_e_up[l], w_e_down[l])
    return rms_norm(x, ln_f)
```

```python
import functools

import jax
import jax.numpy as jnp
import numpy as np
from jax import lax
from jax.experimental import pallas as pl
from jax.experimental.pallas import tpu as pltpu

F32 = jnp.float32
BF16 = jnp.bfloat16
HIGHEST = lax.Precision.HIGHEST

HEAD_DIM = 128
A_HEADS = 8
A_WIDTH = A_HEADS * HEAD_DIM
MOBA_BLOCK = 256
MOBA_TOPK = 3
ROPE_THETA = 500000.0
ROPE_DIM = HEAD_DIM // 4
ATTN_SCALE = HEAD_DIM ** -0.5
G_HEADS = 4
G_VAL_DIM = 256
G_KEY_DIM = 128
G_KEY_WIDTH = G_HEADS * G_KEY_DIM
G_VAL_WIDTH = G_HEADS * G_VAL_DIM
GATE_RANK = 16
GATE_TAU = 16.0
GLA_CHUNK = 64
N_GROUPS = 4
EXPERTS_PER_GROUP = 8
N_EXPERTS = N_GROUPS * EXPERTS_PER_GROUP
TOP_K = 2
EPS = 1e-6

LANES = 128
PROJ_MAIN = 3 * A_WIDTH + 2 * G_KEY_WIDTH + 2 * G_VAL_WIDTH
ROW_BLOCK = 256
VMEM_LIMIT = 52 * 1024 * 1024


def _cparams(sem):
    return pltpu.CompilerParams(dimension_semantics=sem, vmem_limit_bytes=VMEM_LIMIT)


def _mod_kernel(c_ref, w_ref, b_ref, o_ref):
    c = c_ref[...]
    ca = c * jax.nn.sigmoid(c)
    o_ref[...] = jnp.dot(ca, w_ref[...], precision=HIGHEST, preferred_element_type=F32) + b_ref[...]


def _adaln_mod(c, w_ada, b_ada):
    depth, d, n6 = w_ada.shape
    b = c.shape[0]
    tn = 1024
    return pl.pallas_call(
        _mod_kernel,
        grid=(depth, n6 // tn),
        in_specs=[
            pl.BlockSpec((b, d), lambda l, n: (0, 0)),
            pl.BlockSpec((None, d, tn), lambda l, n: (l, 0, n)),
            pl.BlockSpec((None, 1, tn), lambda l, n: (l, 0, n)),
        ],
        out_specs=pl.BlockSpec((None, b, tn), lambda l, n: (l, 0, n)),
        out_shape=jax.ShapeDtypeStruct((depth, b, n6), F32),
        compiler_params=_cparams(("arbitrary", "arbitrary")),
        name="adaln_mod",
    )(c, w_ada, b_ada.reshape(depth, 1, n6))


def _modulated_norm(x, g, shift, scale):
    y = x * lax.rsqrt(jnp.mean(x * x, axis=-1, keepdims=True) + EPS) * g
    return y * (1.0 + scale) + shift


def _inproj_kernel(x_ref, ln_ref, mod_ref, w_ref, wga_ref, wgk_ref, bgk_ref, proj_ref, z_ref, h_scr):
    @pl.when(pl.program_id(1) == 0)
    def _():
        h = _modulated_norm(x_ref[...], ln_ref[...], mod_ref[0:1, :], mod_ref[1:2, :])
        hb = h.astype(BF16)
        h_scr[...] = hb
        ga = jnp.dot(hb, wga_ref[...], preferred_element_type=F32)
        z_ref[...] = jnp.dot(ga, wgk_ref[...], precision=HIGHEST, preferred_element_type=F32) + bgk_ref[...]

    proj_ref[...] = jnp.dot(h_scr[...], w_ref[...], preferred_element_type=F32)


def _in_projection(x2, ln, mod, w_main, w_ga, w_gk, b_gk, seq):
    t, d = x2.shape
    tm, tn = 512, 1024
    return pl.pallas_call(
        _inproj_kernel,
        grid=(t // tm, PROJ_MAIN // tn),
        in_specs=[
            pl.BlockSpec((tm, d), lambda m, n: (m, 0)),
            pl.BlockSpec((1, d), lambda m, n: (0, 0)),
            pl.BlockSpec((None, 6, d), lambda m, n: ((m * tm) // seq, 0, 0)),
            pl.BlockSpec((d, tn), lambda m, n: (0, n)),
            pl.BlockSpec((d, LANES), lambda m, n: (0, 0)),
            pl.BlockSpec((LANES, G_KEY_WIDTH), lambda m, n: (0, 0)),
            pl.BlockSpec((1, G_KEY_WIDTH), lambda m, n: (0, 0)),
        ],
        out_specs=[
            pl.BlockSpec((tm, tn), lambda m, n: (m, n)),
            pl.BlockSpec((tm, G_KEY_WIDTH), lambda m, n: (m, 0)),
        ],
        out_shape=[
            jax.ShapeDtypeStruct((t, PROJ_MAIN), F32),
            jax.ShapeDtypeStruct((t, G_KEY_WIDTH), F32),
        ],
        scratch_shapes=[pltpu.VMEM((tm, d), BF16)],
        compiler_params=_cparams(("arbitrary", "arbitrary")),
        name="in_projection",
    )(x2, ln, mod, w_main, w_ga, w_gk, b_gk)


def _rope(x, cos, sin_lo, sin_hi):
    half = ROPE_DIM // 2
    return x * cos + pltpu.roll(x, half, 1) * sin_hi + pltpu.roll(x, LANES - half, 1) * sin_lo


def _moba_kernel(q_ref, k_ref, v_ref, cos_ref, slo_ref, shi_ref, o_ref, s_scr):
    seq = q_ref.shape[0]
    nblk = seq // MOBA_BLOCK
    cos, slo, shi = cos_ref[...], slo_ref[...], shi_ref[...]
    q = _rope(q_ref[...], cos, slo, shi) * ATTN_SCALE
    k = _rope(k_ref[...], cos, slo, shi)
    kmean = jnp.mean(k.reshape(nblk, MOBA_BLOCK, HEAD_DIM), axis=1)
    gate = lax.dot_general(kmean, q, (((1,), (1,)), ((), ())), precision=HIGHEST,
                           preferred_element_type=F32)
    blk_of_q = lax.broadcasted_iota(jnp.int32, (nblk, seq), 1) // MOBA_BLOCK
    row = lax.broadcasted_iota(jnp.int32, (nblk, seq), 0)
    past = row < blk_of_q
    better = jnp.zeros((nblk, seq), F32)
    for m in range(nblk):
        gm = gate[m:m + 1, :]
        past_m = blk_of_q[m:m + 1, :] > m
        beats = (gm > gate) | ((gm == gate) & (row > m))
        better = better + jnp.where(beats & past_m, 1.0, 0.0)
    sel = past & (better < float(MOBA_TOPK))

    qb = q.astype(BF16)
    kb = k.astype(BF16)
    vt = v_ref[...].T.astype(BF16)
    key_i = lax.broadcasted_iota(jnp.int32, (MOBA_BLOCK, MOBA_BLOCK), 0)
    qry_i = lax.broadcasted_iota(jnp.int32, (MOBA_BLOCK, MOBA_BLOCK), 1)
    causal = key_i <= qry_i
    for i in range(nblk):
        qs = slice(i * MOBA_BLOCK, (i + 1) * MOBA_BLOCK)
        qi = qb[qs, :]
        mx = None
        for n in range(i + 1):
            ks = slice(n * MOBA_BLOCK, (n + 1) * MOBA_BLOCK)
            s = lax.dot_general(kb[ks, :], qi, (((1,), (1,)), ((), ())),
                                preferred_element_type=F32)
            keep = causal if n == i else sel[n:n + 1, qs]
            s = jnp.where(keep, s, -jnp.inf)
            s_scr[n] = s
            bm = jnp.max(s, axis=0, keepdims=True)
            mx = bm if mx is None else jnp.maximum(mx, bm)
        den = jnp.zeros((1, MOBA_BLOCK), F32)
        acc = jnp.zeros((HEAD_DIM, MOBA_BLOCK), F32)
        for n in range(i + 1):
            ks = slice(n * MOBA_BLOCK, (n + 1) * MOBA_BLOCK)
            p = jnp.exp(s_scr[n] - mx)
            den = den + jnp.sum(p, axis=0, keepdims=True)
            acc = acc + jnp.dot(vt[:, ks], p.astype(BF16), preferred_element_type=F32)
        o_ref[qs, :] = (acc / den).T.astype(o_ref.dtype)


def _rope_tables(seq):
    half = ROPE_DIM // 2
    inv = jnp.power(ROPE_THETA, -jnp.arange(half, dtype=F32) * (2.0 / ROPE_DIM))
    ang = jnp.arange(seq, dtype=F32)[:, None] * inv[None, :]
    cos, sin = jnp.cos(ang), jnp.sin(ang)
    ones = jnp.ones((seq, HEAD_DIM - ROPE_DIM), F32)
    zeros = jnp.zeros((seq, HEAD_DIM - half), F32)
    cos_t = jnp.concatenate([cos, cos, ones], axis=1)
    sin_lo = jnp.concatenate([-sin, zeros], axis=1)
    sin_hi = jnp.concatenate([jnp.zeros((seq, half), F32), sin, zeros[:, half:]], axis=1)
    return cos_t, sin_lo, sin_hi


def _moba_attention(proj3, tables):
    b, seq, _ = proj3.shape
    head = lambda off: pl.BlockSpec((None, seq, HEAD_DIM), lambda bi, h: (bi, 0, off + h))
    tab = pl.BlockSpec((seq, HEAD_DIM), lambda bi, h: (0, 0))
    return pl.pallas_call(
        _moba_kernel,
        grid=(b, A_HEADS),
        in_specs=[head(0), head(A_HEADS), head(2 * A_HEADS), tab, tab, tab],
        out_specs=pl.BlockSpec((None, seq, HEAD_DIM), lambda bi, h: (bi, 0, h)),
        out_shape=jax.ShapeDtypeStruct((b, seq, A_WIDTH), BF16),
        scratch_shapes=[pltpu.VMEM((seq // MOBA_BLOCK, MOBA_BLOCK, MOBA_BLOCK), F32)],
        compiler_params=_cparams(("arbitrary", "arbitrary")),
        name="moba_attention",
    )(proj3, proj3, proj3, *tables)


def _log_sigmoid(z):
    return jnp.minimum(z, 0.0) - jnp.log1p(jnp.exp(-jnp.abs(z)))


def _chunk_cumsum(x):
    pos = lax.broadcasted_iota(jnp.int32, x.shape, 0) % GLA_CHUNK
    shift = 1
    while shift < GLA_CHUNK:
        x = x + jnp.where(pos >= shift, pltpu.roll(x, shift, 0), 0.0)
        shift *= 2
    return x


def _gla_kernel(q_ref, k_ref, v_ref, og_ref, z_ref, gn_ref, o_ref,
                qd_scr, ki_scr, ke_scr, dec_scr, st_scr):
    seq = q_ref.shape[0]
    nc = seq // GLA_CHUNK
    log_a = _log_sigmoid(z_ref[...]) / GATE_TAU
    b = _chunk_cumsum(log_a)
    b3 = b.reshape(nc, GLA_CHUNK, G_KEY_DIM)
    b_last = b3[:, GLA_CHUNK - 1:GLA_CHUNK, :]
    k = k_ref[...]
    qd_scr[...] = ((q_ref[...] * (G_KEY_DIM ** -0.5)) * jnp.exp(b)).astype(BF16)
    ki_scr[...] = (k * jnp.exp(-b)).astype(BF16)
    ke3 = k.reshape(nc, GLA_CHUNK, G_KEY_DIM) * jnp.exp(b_last - b3)
    ke_scr[...] = ke3.reshape(seq, G_KEY_DIM).astype(BF16)
    dec_scr[...] = jnp.exp(b_last)
    st_scr[...] = jnp.zeros_like(st_scr)
    t_i = lax.broadcasted_iota(jnp.int32, (GLA_CHUNK, GLA_CHUNK), 0)
    s_i = lax.broadcasted_iota(jnp.int32, (GLA_CHUNK, GLA_CHUNK), 1)
    causal = s_i <= t_i
    gn = gn_ref[...]

    def chunk(n, carry):
        rows = pl.ds(pl.multiple_of(n * GLA_CHUNK, GLA_CHUNK), GLA_CHUNK)
        qd = qd_scr[rows, :]
        vb = v_ref[rows, :].astype(BF16)
        att = lax.dot_general(qd, ki_scr[rows, :], (((1,), (1,)), ((), ())),
                              preferred_element_type=F32)
        att = jnp.where(causal, att, 0.0).astype(BF16)
        state_t = st_scr[...]
        o = jnp.dot(att, vb, preferred_element_type=F32)
        o = o + lax.dot_general(qd, state_t.astype(BF16), (((1,), (1,)), ((), ())),
                                preferred_element_type=F32)
        kv_t = lax.dot_general(vb, ke_scr[rows, :], (((0,), (0,)), ((), ())),
                               preferred_element_type=F32)
        st_scr[...] = state_t * dec_scr[n] + kv_t
        o = o * lax.rsqrt(jnp.mean(o * o, axis=-1, keepdims=True) + EPS) * gn
        og = og_ref[rows, :]
        o_ref[rows, :] = (o * (og * jax.nn.sigmoid(og))).astype(o_ref.dtype)
        return carry

    lax.fori_loop(0, nc, chunk, 0)


def _gla_mixer(proj3, z3, g_onorm):
    b, seq, _ = proj3.shape
    kq = 3 * A_WIDTH // G_KEY_DIM
    kv = (3 * A_WIDTH + 2 * G_KEY_WIDTH) // G_VAL_DIM
    key = lambda off: pl.BlockSpec((None, seq, G_KEY_DIM), lambda bi, h: (bi, 0, off + h))
    val = lambda off: pl.BlockSpec((None, seq, G_VAL_DIM), lambda bi, h: (bi, 0, off + h))
    return pl.pallas_call(
        _gla_kernel,
        grid=(b, G_HEADS),
        in_specs=[
            key(kq), key(kq + G_HEADS), val(kv), val(kv + G_HEADS),
            pl.BlockSpec((None, seq, G_KEY_DIM), lambda bi, h: (bi, 0, h)),
            pl.BlockSpec((1, G_VAL_DIM), lambda bi, h: (0, 0)),
        ],
        out_specs=pl.BlockSpec((None, seq, G_VAL_DIM), lambda bi, h: (bi, 0, h)),
        out_shape=jax.ShapeDtypeStruct((b, seq, G_VAL_WIDTH), BF16),
        scratch_shapes=[
            pltpu.VMEM((seq, G_KEY_DIM), BF16),
            pltpu.VMEM((seq, G_KEY_DIM), BF16),
            pltpu.VMEM((seq, G_KEY_DIM), BF16),
            pltpu.VMEM((seq // GLA_CHUNK, 1, G_KEY_DIM), F32),
            pltpu.VMEM((G_VAL_DIM, G_KEY_DIM), F32),
        ],
        compiler_params=_cparams(("arbitrary", "arbitrary")),
        name="gla_mixer",
    )(proj3, proj3, proj3, proj3, z3, g_onorm)


def _first_lane(mask, lane):
    return jnp.min(jnp.where(mask, lane, LANES), axis=-1, keepdims=True)


def _outproj_router_kernel(oa_ref, ob_ref, x_ref, wa_ref, wb_ref, mod_ref, ln_ref, wr_ref, br_ref,
                           xo_ref, h_ref, ri_ref, rw_ref, cnt_ref, run_scr):
    tm = x_ref.shape[0]

    @pl.when(pl.program_id(0) == 0)
    def _():
        run_scr[...] = jnp.zeros_like(run_scr)

    mix = jnp.dot(oa_ref[...], wa_ref[...], preferred_element_type=F32)
    mix = mix + jnp.dot(ob_ref[...], wb_ref[...], preferred_element_type=F32)
    x_new = x_ref[...] + mod_ref[2:3, :] * mix
    xo_ref[...] = x_new
    h = _modulated_norm(x_new, ln_ref[...], mod_ref[3:4, :], mod_ref[4:5, :])
    h_ref[...] = h

    logits = jnp.dot(h, wr_ref[...], precision=HIGHEST, preferred_element_type=F32) + br_ref[...]
    lane = lax.broadcasted_iota(jnp.int32, (tm, LANES), 1)
    is_grp = (lane >= N_EXPERTS) & (lane < N_EXPERTS + N_GROUPS)
    l1 = jnp.where(is_grp, logits, -jnp.inf)
    m1 = jnp.max(l1, axis=-1, keepdims=True)
    grp = _first_lane(l1 == m1, lane) - N_EXPERTS
    p_grp = 1.0 / jnp.sum(jnp.exp(l1 - m1), axis=-1, keepdims=True)
    in_grp = (lane < N_EXPERTS) & ((lane // EXPERTS_PER_GROUP) == grp)
    l2 = jnp.where(in_grp, logits, -jnp.inf)
    va = jnp.max(l2, axis=-1, keepdims=True)
    ia = _first_lane(l2 == va, lane)
    l2b = jnp.where(lane == ia, -jnp.inf, l2)
    vb = jnp.max(l2b, axis=-1, keepdims=True)
    ib = _first_lane(l2b == vb, lane)
    eb = jnp.exp(vb - va)
    wa = p_grp * (1.0 / (1.0 + eb))
    wb = p_grp * (eb / (1.0 + eb))

    oh_a = jnp.where(lane == ia, 1.0, 0.0)
    oh_b = jnp.where(lane == ib, 1.0, 0.0)
    cnt = oh_a + oh_b
    r_i = lax.broadcasted_iota(jnp.int32, (tm, tm), 0)
    c_i = lax.broadcasted_iota(jnp.int32, (tm, tm), 1)
    strict_lower = jnp.where(c_i < r_i, 1.0, 0.0).astype(BF16)
    before = jnp.dot(strict_lower, cnt.astype(BF16), preferred_element_type=F32) + run_scr[...]
    rank_a = jnp.sum(oh_a * before, axis=-1, keepdims=True)
    rank_b = jnp.sum(oh_b * before, axis=-1, keepdims=True)
    run_new = run_scr[...] + jnp.sum(cnt, axis=0, keepdims=True)
    run_scr[...] = run_new
    cnt_ref[...] = jnp.broadcast_to(run_new, cnt_ref.shape).astype(jnp.int32)

    ri = jnp.where(lane == 0, ia, 0) + jnp.where(lane == 1, ib, 0)
    ri = ri + jnp.where(lane == 2, rank_a.astype(jnp.int32), 0) + jnp.where(lane == 3, rank_b.astype(jnp.int32), 0)
    ri_ref[...] = ri
    rw_ref[...] = jnp.where(lane == 0, wa, 0.0) + jnp.where(lane == 1, wb, 0.0)


def _outproj_router(oa2, ob2, x2, w_oa, w_ob, mod, ln, w_r, b_r, seq):
    t, d = x2.shape
    tm = 256
    row = lambda w: pl.BlockSpec((tm, w), lambda m: (m, 0))
    full = lambda a, c: pl.BlockSpec((a, c), lambda m: (0, 0))
    return pl.pallas_call(
        _outproj_router_kernel,
        grid=(t // tm,),
        in_specs=[
            row(A_WIDTH), row(G_VAL_WIDTH), row(d),
            full(A_WIDTH, d), full(G_VAL_WIDTH, d),
            pl.BlockSpec((None, 6, d), lambda m: ((m * tm) // seq, 0, 0)),
            full(1, d), full(d, LANES), full(1, LANES),
        ],
        out_specs=[row(d), row(d), row(LANES), row(LANES), full(8, LANES)],
        out_shape=[
            jax.ShapeDtypeStruct((t, d), F32),
            jax.ShapeDtypeStruct((t, d), F32),
            jax.ShapeDtypeStruct((t, LANES), jnp.int32),
            jax.ShapeDtypeStruct((t, LANES), F32),
            jax.ShapeDtypeStruct((8, LANES), jnp.int32),
        ],
        scratch_shapes=[pltpu.VMEM((1, LANES), F32)],
        compiler_params=_cparams(("arbitrary",)),
        name="outproj_router",
    )(oa2, ob2, x2, w_oa, w_ob, mod, ln, w_r, b_r)


def _gather_rows(idx_ref, n_rows, src_hbm, dst_vmem, sem):
    def issue(r, carry):
        pltpu.make_async_copy(src_hbm.at[pl.ds(idx_ref[0, r], 1), :],
                              dst_vmem.at[pl.ds(r, 1), :], sem).start()
        return carry

    lax.fori_loop(0, n_rows, issue, 0, unroll=8)
    pltpu.make_async_copy(src_hbm.at[pl.ds(0, n_rows), :], dst_vmem, sem).wait()


def _expert_kernel(bexp_ref, nused_ref, tok_ref, h_hbm, wg_ref, wu_ref, wd_ref, y_ref, x_scr, sem):
    del bexp_ref
    blk = pl.program_id(0)

    @pl.when(blk < nused_ref[0])
    def _():
        _gather_rows(tok_ref, ROW_BLOCK, h_hbm, x_scr, sem)
        xb = x_scr[...].astype(BF16)
        g = jnp.dot(xb, wg_ref[...], preferred_element_type=F32)
        u = jnp.dot(xb, wu_ref[...], preferred_element_type=F32)
        a = ((g * jax.nn.sigmoid(g)) * u).astype(BF16)
        y_ref[...] = jnp.dot(a, wd_ref[...], preferred_element_type=F32)

    @pl.when(blk >= nused_ref[0])
    def _():
        y_ref[...] = jnp.zeros_like(y_ref)


def _expert_mlp(block_exp, n_used, slot_tok3, h2, wg, wu, wd):
    nb = slot_tok3.shape[0]
    d = h2.shape[1]
    f = wg.shape[2]
    grid_spec = pltpu.PrefetchScalarGridSpec(
        num_scalar_prefetch=2,
        grid=(nb,),
        in_specs=[
            pl.BlockSpec((None, 1, ROW_BLOCK), lambda i, be, nu: (i, 0, 0), memory_space=pltpu.SMEM),
            pl.BlockSpec(memory_space=pl.ANY),
            pl.BlockSpec((None, d, f), lambda i, be, nu: (be[i], 0, 0)),
            pl.BlockSpec((None, d, f), lambda i, be, nu: (be[i], 0, 0)),
            pl.BlockSpec((None, f, d), lambda i, be, nu: (be[i], 0, 0)),
        ],
        out_specs=pl.BlockSpec((ROW_BLOCK, d), lambda i, be, nu: (i, 0)),
        scratch_shapes=[pltpu.VMEM((ROW_BLOCK, d), F32), pltpu.SemaphoreType.DMA(())],
    )
    return pl.pallas_call(
        _expert_kernel,
        grid_spec=grid_spec,
        out_shape=jax.ShapeDtypeStruct((nb * ROW_BLOCK, d), F32),
        compiler_params=_cparams(("arbitrary",)),
        name="expert_mlp",
    )(block_exp, n_used, slot_tok3, h2, wg, wu, wd)


def _combine_kernel(dest_ref, y_hbm, rw_ref, x_ref, mod_ref, lnf_ref, o_ref, y_scr, sem, *, final):
    tm = x_ref.shape[0]
    _gather_rows(dest_ref, 2 * tm, y_hbm, y_scr, sem)
    rw = rw_ref[...]
    y = rw[:, 0:1] * y_scr[0:tm, :] + rw[:, 1:2] * y_scr[tm:2 * tm, :]
    x_new = x_ref[...] + mod_ref[5:6, :] * y
    if final:
        x_new = x_new * lax.rsqrt(jnp.mean(x_new * x_new, axis=-1, keepdims=True) + EPS) * lnf_ref[...]
    o_ref[...] = x_new


def _combine(dest3, yg, rw, x2, mod, ln_f, seq, final):
    t, d = x2.shape
    tm = dest3.shape[2] // 2
    return pl.pallas_call(
        functools.partial(_combine_kernel, final=final),
        grid=(t // tm,),
        in_specs=[
            pl.BlockSpec((None, 1, 2 * tm), lambda m: (m, 0, 0), memory_space=pltpu.SMEM),
            pl.BlockSpec(memory_space=pl.ANY),
            pl.BlockSpec((tm, LANES), lambda m: (m, 0)),
            pl.BlockSpec((tm, d), lambda m: (m, 0)),
            pl.BlockSpec((None, 6, d), lambda m: ((m * tm) // seq, 0, 0)),
            pl.BlockSpec((1, d), lambda m: (0, 0)),
        ],
        out_specs=pl.BlockSpec((tm, d), lambda m: (m, 0)),
        out_shape=jax.ShapeDtypeStruct((t, d), F32),
        scratch_shapes=[pltpu.VMEM((2 * tm, d), F32), pltpu.SemaphoreType.DMA(())],
        compiler_params=_cparams(("arbitrary",)),
        name="moe_combine",
    )(dest3, yg, rw, x2, mod, ln_f)


def _routing_tables(ri, counts, t):
    n_blocks = (t * TOP_K) // ROW_BLOCK + N_EXPERTS
    cnt = counts[0, :N_EXPERTS]
    padded = (cnt + ROW_BLOCK - 1) // ROW_BLOCK * ROW_BLOCK
    ends = jnp.cumsum(padded)
    starts = ends - padded
    experts, ranks = ri[:, 0:TOP_K], ri[:, TOP_K:2 * TOP_K]
    dest = starts[experts] + ranks
    tokens = jnp.broadcast_to(jnp.arange(t, dtype=jnp.int32)[:, None], (t, TOP_K))
    slot_tok = jnp.zeros((n_blocks * ROW_BLOCK,), jnp.int32).at[dest.reshape(-1)].set(tokens.reshape(-1))
    n_used = (ends[-1] // ROW_BLOCK).astype(jnp.int32)
    blocks = jnp.minimum(jnp.arange(n_blocks, dtype=jnp.int32), n_used - 1) * ROW_BLOCK
    block_exp = jnp.minimum(jnp.searchsorted(ends, blocks, side='right'), N_EXPERTS - 1).astype(jnp.int32)
    return dest.astype(jnp.int32), slot_tok.reshape(n_blocks, 1, ROW_BLOCK), block_exp, n_used.reshape(1)


def kernel(x, c, ln1, ln2, w_ada, b_ada, w_in, w_gk, b_gk, g_onorm, w_out,
           w_r1, b_r1, w_r2, b_r2, w_e_gate, w_e_up, w_e_down, ln_f):
    b, seq, d = x.shape
    depth = w_ada.shape[0]
    t = b * seq
    mod_all = _adaln_mod(c, w_ada, b_ada).reshape(depth, b, 6, d)
    tables = _rope_tables(seq)
    x2 = x.reshape(t, d)
    tm_c = 256
    for l in range(depth):
        mod = mod_all[l]
        w_main = w_in[l][:, :PROJ_MAIN].astype(BF16)
        w_ga = jnp.pad(w_in[l][:, PROJ_MAIN:], ((0, 0), (0, LANES - GATE_RANK))).astype(BF16)
        w_gk_p = jnp.pad(w_gk[l], ((0, LANES - GATE_RANK), (0, 0)))
        proj, z = _in_projection(x2, ln1[l][None, :], mod, w_main, w_ga, w_gk_p, b_gk[l][None, :], seq)
        proj3 = proj.reshape(b, seq, PROJ_MAIN)
        oa = _moba_attention(proj3, tables)
        ob = _gla_mixer(proj3, z.reshape(b, seq, G_KEY_WIDTH), g_onorm[l][None, :])
        w_o = w_out[l].astype(BF16)
        w_r = jnp.pad(jnp.concatenate([w_r2[l], w_r1[l]], axis=1), ((0, 0), (0, LANES - N_EXPERTS - N_GROUPS)))
        b_r = jnp.pad(jnp.concatenate([b_r2[l], b_r1[l]]), (0, LANES - N_EXPERTS - N_GROUPS))[None, :]
        x2, h2, ri, rw, counts = _outproj_router(
            oa.reshape(t, A_WIDTH), ob.reshape(t, G_VAL_WIDTH), x2, w_o[:A_WIDTH], w_o[A_WIDTH:],
            mod, ln2[l][None, :], w_r, b_r, seq)
        dest, slot_tok3, block_exp, n_used = _routing_tables(ri, counts, t)
        yg = _expert_mlp(block_exp, n_used, slot_tok3, h2,
                         w_e_gate[l].astype(BF16), w_e_up[l].astype(BF16), w_e_down[l].astype(BF16))
        dest3 = dest.reshape(t // tm_c, tm_c, TOP_K).transpose(0, 2, 1).reshape(t // tm_c, 1, TOP_K * tm_c)
        x2 = _combine(dest3, yg, rw, x2, mod, ln_f[None, :], seq, final=(l == depth - 1))
    return x2.reshape(b, seq, d)
```

```python
import functools

import jax
import jax.numpy as jnp
import numpy as np
from jax import lax
from jax.experimental import pallas as pl
from jax.experimental.pallas import tpu as pltpu

F32 = jnp.float32
BF16 = jnp.bfloat16
HIGHEST = lax.Precision.HIGHEST

HEAD_DIM = 128
A_HEADS = 8
A_WIDTH = A_HEADS * HEAD_DIM
MOBA_BLOCK = 256
MOBA_TOPK = 3
ROPE_THETA = 500000.0
ROPE_DIM = HEAD_DIM // 4
ATTN_SCALE = HEAD_DIM ** -0.5
G_HEADS = 4
G_VAL_DIM = 256
G_KEY_DIM = 128
G_KEY_WIDTH = G_HEADS * G_KEY_DIM
G_VAL_WIDTH = G_HEADS * G_VAL_DIM
GATE_RANK = 16
GATE_TAU = 16.0
GLA_CHUNK = 64
N_GROUPS = 4
EXPERTS_PER_GROUP = 8
N_EXPERTS = N_GROUPS * EXPERTS_PER_GROUP
TOP_K = 2
EPS = 1e-6

LANES = 128
PROJ_MAIN = 3 * A_WIDTH + 2 * G_KEY_WIDTH + 2 * G_VAL_WIDTH
ROW_BLOCK = 256
VMEM_LIMIT = 52 * 1024 * 1024


def _cparams(sem):
    return pltpu.CompilerParams(dimension_semantics=sem, vmem_limit_bytes=VMEM_LIMIT)


def _mod_kernel(c_ref, w_ref, b_ref, o_ref):
    c = c_ref[...]
    ca = c * jax.nn.sigmoid(c)
    o_ref[...] = jnp.dot(ca, w_ref[...], precision=HIGHEST, preferred_element_type=F32) + b_ref[...]


def _adaln_mod(c, w_ada, b_ada):
    depth, d, n6 = w_ada.shape
    b = c.shape[0]
    tn = 1024
    return pl.pallas_call(
        _mod_kernel,
        grid=(depth, n6 // tn),
        in_specs=[
            pl.BlockSpec((b, d), lambda l, n: (0, 0)),
            pl.BlockSpec((None, d, tn), lambda l, n: (l, 0, n)),
            pl.BlockSpec((None, 1, tn), lambda l, n: (l, 0, n)),
        ],
        out_specs=pl.BlockSpec((None, b, tn), lambda l, n: (l, 0, n)),
        out_shape=jax.ShapeDtypeStruct((depth, b, n6), F32),
        compiler_params=_cparams(("arbitrary", "arbitrary")),
        name="adaln_mod",
    )(c, w_ada, b_ada.reshape(depth, 1, n6))


def _modulated_norm(x, g, shift, scale):
    y = x * lax.rsqrt(jnp.mean(x * x, axis=-1, keepdims=True) + EPS) * g
    return y * (1.0 + scale) + shift


def _inproj_kernel(x_ref, ln_ref, mod_ref, w_ref, wga_ref, wgk_ref, bgk_ref, proj_ref, z_ref, h_scr):
    @pl.when(pl.program_id(1) == 0)
    def _():
        h = _modulated_norm(x_ref[...], ln_ref[...], mod_ref[0:1, :], mod_ref[1:2, :])
        hb = h.astype(BF16)
        h_scr[...] = hb
        ga = jnp.dot(hb, wga_ref[...], preferred_element_type=F32)
        z_ref[...] = jnp.dot(ga, wgk_ref[...], precision=HIGHEST, preferred_element_type=F32) + bgk_ref[...]

    proj_ref[...] = jnp.dot(h_scr[...], w_ref[...], preferred_element_type=F32)


def _in_projection(x2, ln, mod, w_main, w_ga, w_gk, b_gk, seq):
    t, d = x2.shape
    tm, tn = 512, 1024
    return pl.pallas_call(
        _inproj_kernel,
        grid=(t // tm, PROJ_MAIN // tn),
        in_specs=[
            pl.BlockSpec((tm, d), lambda m, n: (m, 0)),
            pl.BlockSpec((1, d), lambda m, n: (0, 0)),
            pl.BlockSpec((None, 6, d), lambda m, n: ((m * tm) // seq, 0, 0)),
            pl.BlockSpec((d, tn), lambda m, n: (0, n)),
            pl.BlockSpec((d, LANES), lambda m, n: (0, 0)),
            pl.BlockSpec((LANES, G_KEY_WIDTH), lambda m, n: (0, 0)),
            pl.BlockSpec((1, G_KEY_WIDTH), lambda m, n: (0, 0)),
        ],
        out_specs=[
            pl.BlockSpec((tm, tn), lambda m, n: (m, n)),
            pl.BlockSpec((tm, G_KEY_WIDTH), lambda m, n: (m, 0)),
        ],
        out_shape=[
            jax.ShapeDtypeStruct((t, PROJ_MAIN), F32),
            jax.ShapeDtypeStruct((t, G_KEY_WIDTH), F32),
        ],
        scratch_shapes=[pltpu.VMEM((tm, d), BF16)],
        compiler_params=_cparams(("arbitrary", "arbitrary")),
        name="in_projection",
    )(x2, ln, mod, w_main, w_ga, w_gk, b_gk)


def _rope(x, cos, sin_lo, sin_hi):
    half = ROPE_DIM // 2
    return x * cos + pltpu.roll(x, half, 1) * sin_hi + pltpu.roll(x, LANES - half, 1) * sin_lo


def _moba_kernel(q_ref, k_ref, v_ref, cos_ref, slo_ref, shi_ref, o_ref, s_scr):
    seq = q_ref.shape[0]
    nblk = seq // MOBA_BLOCK
    cos, slo, shi = cos_ref[...], slo_ref[...], shi_ref[...]
    q = _rope(q_ref[...], cos, slo, shi) * ATTN_SCALE
    k = _rope(k_ref[...], cos, slo, shi)
    kmean = jnp.mean(k.reshape(nblk, MOBA_BLOCK, HEAD_DIM), axis=1)
    gate = lax.dot_general(kmean, q, (((1,), (1,)), ((), ())), precision=HIGHEST,
                           preferred_element_type=F32)
    blk_of_q = lax.broadcasted_iota(jnp.int32, (nblk, seq), 1) // MOBA_BLOCK
    row = lax.broadcasted_iota(jnp.int32, (nblk, seq), 0)
    past = row < blk_of_q
    better = jnp.zeros((nblk, seq), F32)
    for m in range(nblk):
        gm = gate[m:m + 1, :]
        past_m = blk_of_q[m:m + 1, :] > m
        beats = (gm > gate) | ((gm == gate) & (row > m))
        better = better + jnp.where(beats & past_m, 1.0, 0.0)
    sel = past & (better < float(MOBA_TOPK))

    qb = q.astype(BF16)
    kb = k.astype(BF16)
    vt = v_ref[...].T.astype(BF16)
    key_i = lax.broadcasted_iota(jnp.int32, (MOBA_BLOCK, MOBA_BLOCK), 0)
    qry_i = lax.broadcasted_iota(jnp.int32, (MOBA_BLOCK, MOBA_BLOCK), 1)
    causal = key_i <= qry_i
    for i in range(nblk):
        qs = slice(i * MOBA_BLOCK, (i + 1) * MOBA_BLOCK)
        qi = qb[qs, :]
        mx = None
        for n in range(i + 1):
            ks = slice(n * MOBA_BLOCK, (n + 1) * MOBA_BLOCK)
            s = lax.dot_general(kb[ks, :], qi, (((1,), (1,)), ((), ())),
                                preferred_element_type=F32)
            keep = causal if n == i else sel[n:n + 1, qs]
            s = jnp.where(keep, s, -jnp.inf)
            s_scr[n] = s
            bm = jnp.max(s, axis=0, keepdims=True)
            mx = bm if mx is None else jnp.maximum(mx, bm)
        den = jnp.zeros((1, MOBA_BLOCK), F32)
        acc = jnp.zeros((HEAD_DIM, MOBA_BLOCK), F32)
        for n in range(i + 1):
            ks = slice(n * MOBA_BLOCK, (n + 1) * MOBA_BLOCK)
            p = jnp.exp(s_scr[n] - mx)
            den = den + jnp.sum(p, axis=0, keepdims=True)
            acc = acc + jnp.dot(vt[:, ks], p.astype(BF16), preferred_element_type=F32)
        o_ref[qs, :] = (acc / den).T.astype(o_ref.dtype)


def _rope_tables(seq):
    half = ROPE_DIM // 2
    inv = jnp.power(ROPE_THETA, -jnp.arange(half, dtype=F32) * (2.0 / ROPE_DIM))
    ang = jnp.arange(seq, dtype=F32)[:, None] * inv[None, :]
    cos, sin = jnp.cos(ang), jnp.sin(ang)
    ones = jnp.ones((seq, HEAD_DIM - ROPE_DIM), F32)
    zeros = jnp.zeros((seq, HEAD_DIM - half), F32)
    cos_t = jnp.concatenate([cos, cos, ones], axis=1)
    sin_lo = jnp.concatenate([-sin, zeros], axis=1)
    sin_hi = jnp.concatenate([jnp.zeros((seq, half), F32), sin, zeros[:, half:]], axis=1)
    return cos_t, sin_lo, sin_hi


def _moba_attention(proj3, tables):
    b, seq, _ = proj3.shape
    head = lambda off: pl.BlockSpec((None, seq, HEAD_DIM), lambda bi, h: (bi, 0, off + h))
    tab = pl.BlockSpec((seq, HEAD_DIM), lambda bi, h: (0, 0))
    return pl.pallas_call(
        _moba_kernel,
        grid=(b, A_HEADS),
        in_specs=[head(0), head(A_HEADS), head(2 * A_HEADS), tab, tab, tab],
        out_specs=pl.BlockSpec((None, seq, HEAD_DIM), lambda bi, h: (bi, 0, h)),
        out_shape=jax.ShapeDtypeStruct((b, seq, A_WIDTH), BF16),
        scratch_shapes=[pltpu.VMEM((seq // MOBA_BLOCK, MOBA_BLOCK, MOBA_BLOCK), F32)],
        compiler_params=_cparams(("arbitrary", "arbitrary")),
        name="moba_attention",
    )(proj3, proj3, proj3, *tables)


def _log_sigmoid(z):
    return jnp.minimum(z, 0.0) - jnp.log1p(jnp.exp(-jnp.abs(z)))


def _chunk_cumsum(x):
    pos = lax.broadcasted_iota(jnp.int32, x.shape, 0) % GLA_CHUNK
    shift = 1
    while shift < GLA_CHUNK:
        x = x + jnp.where(pos >= shift, pltpu.roll(x, shift, 0), 0.0)
        shift *= 2
    return x


def _gla_kernel(q_ref, k_ref, v_ref, og_ref, z_ref, gn_ref, o_ref,
                qd_scr, ki_scr, ke_scr, dec_scr, st_scr):
    seq = q_ref.shape[0]
    nc = seq // GLA_CHUNK
    log_a = _log_sigmoid(z_ref[...]) / GATE_TAU
    b = _chunk_cumsum(log_a)
    b3 = b.reshape(nc, GLA_CHUNK, G_KEY_DIM)
    b_last = b3[:, GLA_CHUNK - 1:GLA_CHUNK, :]
    k = k_ref[...]
    qd_scr[...] = ((q_ref[...] * (G_KEY_DIM ** -0.5)) * jnp.exp(b)).astype(BF16)
    ki_scr[...] = (k * jnp.exp(-b)).astype(BF16)
    ke3 = k.reshape(nc, GLA_CHUNK, G_KEY_DIM) * jnp.exp(b_last - b3)
    ke_scr[...] = ke3.reshape(seq, G_KEY_DIM).astype(BF16)
    dec_scr[...] = jnp.exp(b_last)
    st_scr[...] = jnp.zeros_like(st_scr)
    t_i = lax.broadcasted_iota(jnp.int32, (GLA_CHUNK, GLA_CHUNK), 0)
    s_i = lax.broadcasted_iota(jnp.int32, (GLA_CHUNK, GLA_CHUNK), 1)
    causal = s_i <= t_i
    gn = gn_ref[...]

    def chunk(n, carry):
        rows = pl.ds(pl.multiple_of(n * GLA_CHUNK, GLA_CHUNK), GLA_CHUNK)
        qd = qd_scr[rows, :]
        vb = v_ref[rows, :].astype(BF16)
        att = lax.dot_general(qd, ki_scr[rows, :], (((1,), (1,)), ((), ())),
                              preferred_element_type=F32)
        att = jnp.where(causal, att, 0.0).astype(BF16)
        state_t = st_scr[...]
        o = jnp.dot(att, vb, preferred_element_type=F32)
        o = o + lax.dot_general(qd, state_t.astype(BF16), (((1,), (1,)), ((), ())),
                                preferred_element_type=F32)
        kv_t = lax.dot_general(vb, ke_scr[rows, :], (((0,), (0,)), ((), ())),
                               preferred_element_type=F32)
        st_scr[...] = state_t * dec_scr[n] + kv_t
        o = o * lax.rsqrt(jnp.mean(o * o, axis=-1, keepdims=True) + EPS) * gn
        og = og_ref[rows, :]
        o_ref[rows, :] = (o * (og * jax.nn.sigmoid(og))).astype(o_ref.dtype)
        return carry

    lax.fori_loop(0, nc, chunk, 0)


def _gla_mixer(proj3, z3, g_onorm):
    b, seq, _ = proj3.shape
    kq = 3 * A_WIDTH // G_KEY_DIM
    kv = (3 * A_WIDTH + 2 * G_KEY_WIDTH) // G_VAL_DIM
    key = lambda off: pl.BlockSpec((None, seq, G_KEY_DIM), lambda bi, h: (bi, 0, off + h))
    val = lambda off: pl.BlockSpec((None, seq, G_VAL_DIM), lambda bi, h: (bi, 0, off + h))
    return pl.pallas_call(
        _gla_kernel,
        grid=(b, G_HEADS),
        in_specs=[
            key(kq), key(kq + G_HEADS), val(kv), val(kv + G_HEADS),
            pl.BlockSpec((None, seq, G_KEY_DIM), lambda bi, h: (bi, 0, h)),
            pl.BlockSpec((1, G_VAL_DIM), lambda bi, h: (0, 0)),
        ],
        out_specs=pl.BlockSpec((None, seq, G_VAL_DIM), lambda bi, h: (bi, 0, h)),
        out_shape=jax.ShapeDtypeStruct((b, seq, G_VAL_WIDTH), BF16),
        scratch_shapes=[
            pltpu.VMEM((seq, G_KEY_DIM), BF16),
            pltpu.VMEM((seq, G_KEY_DIM), BF16),
            pltpu.VMEM((seq, G_KEY_DIM), BF16),
            pltpu.VMEM((seq // GLA_CHUNK, 1, G_KEY_DIM), F32),
            pltpu.VMEM((G_VAL_DIM, G_KEY_DIM), F32),
        ],
        compiler_params=_cparams(("arbitrary", "arbitrary")),
        name="gla_mixer",
    )(proj3, proj3, proj3, proj3, z3, g_onorm)


def _first_lane(mask, lane):
    return jnp.min(jnp.where(mask, lane, LANES), axis=-1, keepdims=True)


def _outproj_router_kernel(oa_ref, ob_ref, x_ref, wa_ref, wb_ref, mod_ref, ln_ref, wr_ref, br_ref,
                           xo_ref, h_ref, ri_ref, rw_ref, cnt_ref, run_scr):
    tm = x_ref.shape[0]

    @pl.when(pl.program_id(0) == 0)
    def _():
        run_scr[...] = jnp.zeros_like(run_scr)

    mix = jnp.dot(oa_ref[...], wa_ref[...], preferred_element_type=F32)
    mix = mix + jnp.dot(ob_ref[...], wb_ref[...], preferred_element_type=F32)
    x_new = x_ref[...] + mod_ref[2:3, :] * mix
    xo_ref[...] = x_new
    h = _modulated_norm(x_new, ln_ref[...], mod_ref[3:4, :], mod_ref[4:5, :])
    h_ref[...] = h

    logits = jnp.dot(h, wr_ref[...], precision=HIGHEST, preferred_element_type=F32) + br_ref[...]
    lane = lax.broadcasted_iota(jnp.int32, (tm, LANES), 1)
    is_grp = (lane >= N_EXPERTS) & (lane < N_EXPERTS + N_GROUPS)
    l1 = jnp.where(is_grp, logits, -jnp.inf)
    m1 = jnp.max(l1, axis=-1, keepdims=True)
    grp = _first_lane(l1 == m1, lane) - N_EXPERTS
    p_grp = 1.0 / jnp.sum(jnp.exp(l1 - m1), axis=-1, keepdims=True)
    in_grp = (lane < N_EXPERTS) & ((lane // EXPERTS_PER_GROUP) == grp)
    l2 = jnp.where(in_grp, logits, -jnp.inf)
    va = jnp.max(l2, axis=-1, keepdims=True)
    ia = _first_lane(l2 == va, lane)
    l2b = jnp.where(lane == ia, -jnp.inf, l2)
    vb = jnp.max(l2b, axis=-1, keepdims=True)
    ib = _first_lane(l2b == vb, lane)
    eb = jnp.exp(vb - va)
    wa = p_grp * (1.0 / (1.0 + eb))
    wb = p_grp * (eb / (1.0 + eb))

    oh_a = jnp.where(lane == ia, 1.0, 0.0)
    oh_b = jnp.where(lane == ib, 1.0, 0.0)
    cnt = oh_a + oh_b
    r_i = lax.broadcasted_iota(jnp.int32, (tm, tm), 0)
    c_i = lax.broadcasted_iota(jnp.int32, (tm, tm), 1)
    strict_lower = jnp.where(c_i < r_i, 1.0, 0.0).astype(BF16)
    before = jnp.dot(strict_lower, cnt.astype(BF16), preferred_element_type=F32) + run_scr[...]
    rank_a = jnp.sum(oh_a * before, axis=-1, keepdims=True)
    rank_b = jnp.sum(oh_b * before, axis=-1, keepdims=True)
    run_new = run_scr[...] + jnp.sum(cnt, axis=0, keepdims=True)
    run_scr[...] = run_new
    cnt_ref[...] = jnp.broadcast_to(run_new, cnt_ref.shape).astype(jnp.int32)

    ri = jnp.where(lane == 0, ia, 0) + jnp.where(lane == 1, ib, 0)
    ri = ri + jnp.where(lane == 2, rank_a.astype(jnp.int32), 0) + jnp.where(lane == 3, rank_b.astype(jnp.int32), 0)
    ri_ref[...] = ri
    rw_ref[...] = jnp.where(lane == 0, wa, 0.0) + jnp.where(lane == 1, wb, 0.0)


def _outproj_router(oa2, ob2, x2, w_oa, w_ob, mod, ln, w_r, b_r, seq):
    t, d = x2.shape
    tm = 256
    row = lambda w: pl.BlockSpec((tm, w), lambda m: (m, 0))
    full = lambda a, c: pl.BlockSpec((a, c), lambda m: (0, 0))
    return pl.pallas_call(
        _outproj_router_kernel,
        grid=(t // tm,),
        in_specs=[
            row(A_WIDTH), row(G_VAL_WIDTH), row(d),
            full(A_WIDTH, d), full(G_VAL_WIDTH, d),
            pl.BlockSpec((None, 6, d), lambda m: ((m * tm) // seq, 0, 0)),
            full(1, d), full(d, LANES), full(1, LANES),
        ],
        out_specs=[row(d), row(d), row(LANES), row(LANES), full(8, LANES)],
        out_shape=[
            jax.ShapeDtypeStruct((t, d), F32),
            jax.ShapeDtypeStruct((t, d), F32),
            jax.ShapeDtypeStruct((t, LANES), jnp.int32),
            jax.ShapeDtypeStruct((t, LANES), F32),
            jax.ShapeDtypeStruct((8, LANES), jnp.int32),
        ],
        scratch_shapes=[pltpu.VMEM((1, LANES), F32)],
        compiler_params=_cparams(("arbitrary",)),
        name="outproj_router",
    )(oa2, ob2, x2, w_oa, w_ob, mod, ln, w_r, b_r)


def _start_row_gather(idx_ref, first, n_rows, src_hbm, dst_vmem, sem):
    def issue(i, carry):
        r = first + i
        pltpu.make_async_copy(src_hbm.at[pl.ds(idx_ref[0, r], 1), :],
                              dst_vmem.at[pl.ds(r, 1), :], sem).start()
        return carry

    lax.fori_loop(0, n_rows, issue, 0, unroll=8)


def _wait_row_gather(src_hbm, dst_vmem, sem):
    pltpu.make_async_copy(src_hbm.at[pl.ds(0, dst_vmem.shape[0]), :], dst_vmem, sem).wait()


def _gather_rows(idx_ref, n_rows, src_hbm, dst_vmem, sem):
    _start_row_gather(idx_ref, 0, n_rows, src_hbm, dst_vmem, sem)
    _wait_row_gather(src_hbm, dst_vmem, sem)


def _expert_kernel(cur_ref, nxt_ref, last_ref, nused_ref, tok_ref, h_hbm, wg_ref, wu_ref, wd_ref,
                   y_ref, wg_res, wu_res, wd_res, x_scr, xb_scr, sems):
    del cur_ref, nxt_ref
    b, j = pl.program_id(0), pl.program_id(1)
    n_used = nused_ref[0]
    rows_per_step = ROW_BLOCK // EXPERT_HIDDEN_CHUNKS

    @pl.when(b < n_used)
    def _():
        slot = b % 2
        _start_row_gather(tok_ref, j * rows_per_step, rows_per_step, h_hbm, x_scr.at[slot], sems.at[slot])

    @pl.when((b >= 1) & (b <= n_used))
    def _():
        @pl.when(j == 0)
        def _():
            slot = (b - 1) % 2
            _wait_row_gather(h_hbm, x_scr.at[slot], sems.at[slot])
            xb_scr[...] = x_scr[slot].astype(BF16)

        xb = xb_scr[...]
        g = jnp.dot(xb, wg_res[j], preferred_element_type=F32)
        u = jnp.dot(xb, wu_res[j], preferred_element_type=F32)
        a = ((g * jax.nn.sigmoid(g)) * u).astype(BF16)
        part = jnp.dot(a, wd_res[j], preferred_element_type=F32)

        @pl.when(j == 0)
        def _():
            y_ref[...] = part

        @pl.when(j > 0)
        def _():
            y_ref[...] += part

    @pl.when((b > n_used) & (j == 0))
    def _():
        y_ref[...] = jnp.zeros_like(y_ref)

    @pl.when(last_ref[b] == 1)
    def _():
        wg_res[j] = wg_ref[...].astype(BF16)
        wu_res[j] = wu_ref[...].astype(BF16)
        wd_res[j] = wd_ref[...].astype(BF16)


EXPERT_HIDDEN_CHUNKS = 4


def _expert_mlp(cur_e, nxt_e, last, n_used, slot_tok3, h2, wg, wu, wd, layer):
    nb = slot_tok3.shape[0]
    d = h2.shape[1]
    f = wg.shape[3]
    nj = EXPERT_HIDDEN_CHUNKS
    fc = f // nj

    def w_index(chunk_axis):
        def index(b, j, cur, nxt, lst, nu):
            e = jnp.where(lst[b] == 1, nxt[b], cur[b])
            jj = jnp.where(lst[b] == 1, j, nj - 1)
            return (layer, e, 0, jj) if chunk_axis == 2 else (layer, e, jj, 0)
        return index

    grid_spec = pltpu.PrefetchScalarGridSpec(
        num_scalar_prefetch=4,
        grid=(nb + 1, nj),
        in_specs=[
            pl.BlockSpec((None, 1, ROW_BLOCK), lambda b, j, *_: (jnp.minimum(b, nb - 1), 0, 0),
                         memory_space=pltpu.SMEM),
            pl.BlockSpec(memory_space=pl.ANY),
            pl.BlockSpec((None, None, d, fc), w_index(2)),
            pl.BlockSpec((None, None, d, fc), w_index(2)),
            pl.BlockSpec((None, None, fc, d), w_index(1)),
        ],
        out_specs=pl.BlockSpec((ROW_BLOCK, d), lambda b, j, *_: (jnp.maximum(b - 1, 0), 0)),
        scratch_shapes=[
            pltpu.VMEM((nj, d, fc), BF16),
            pltpu.VMEM((nj, d, fc), BF16),
            pltpu.VMEM((nj, fc, d), BF16),
            pltpu.VMEM((2, ROW_BLOCK, d), F32),
            pltpu.VMEM((ROW_BLOCK, d), BF16),
            pltpu.SemaphoreType.DMA((2,)),
        ],
    )
    return pl.pallas_call(
        _expert_kernel,
        grid_spec=grid_spec,
        out_shape=jax.ShapeDtypeStruct((nb * ROW_BLOCK, d), F32),
        compiler_params=_cparams(("arbitrary", "arbitrary")),
        name="expert_mlp",
    )(cur_e, nxt_e, last, n_used, slot_tok3, h2, wg, wu, wd)


def _combine_kernel(dest_ref, y_hbm, rw_ref, x_ref, mod_ref, lnf_ref, o_ref, y_scr, sem, *, final):
    tm = x_ref.shape[0]
    _gather_rows(dest_ref, 2 * tm, y_hbm, y_scr, sem)
    rw = rw_ref[...]
    y = rw[:, 0:1] * y_scr[0:tm, :] + rw[:, 1:2] * y_scr[tm:2 * tm, :]
    x_new = x_ref[...] + mod_ref[5:6, :] * y
    if final:
        x_new = x_new * lax.rsqrt(jnp.mean(x_new * x_new, axis=-1, keepdims=True) + EPS) * lnf_ref[...]
    o_ref[...] = x_new


def _combine(dest3, yg, rw, x2, mod, ln_f, seq, final):
    t, d = x2.shape
    tm = dest3.shape[2] // 2
    return pl.pallas_call(
        functools.partial(_combine_kernel, final=final),
        grid=(t // tm,),
        in_specs=[
            pl.BlockSpec((None, 1, 2 * tm), lambda m: (m, 0, 0), memory_space=pltpu.SMEM),
            pl.BlockSpec(memory_space=pl.ANY),
            pl.BlockSpec((tm, LANES), lambda m: (m, 0)),
            pl.BlockSpec((tm, d), lambda m: (m, 0)),
            pl.BlockSpec((None, 6, d), lambda m: ((m * tm) // seq, 0, 0)),
            pl.BlockSpec((1, d), lambda m: (0, 0)),
        ],
        out_specs=pl.BlockSpec((tm, d), lambda m: (m, 0)),
        out_shape=jax.ShapeDtypeStruct((t, d), F32),
        scratch_shapes=[pltpu.VMEM((2 * tm, d), F32), pltpu.SemaphoreType.DMA(())],
        compiler_params=_cparams(("arbitrary",)),
        name="moe_combine",
    )(dest3, yg, rw, x2, mod, ln_f)


def _routing_tables(ri, counts, t):
    n_blocks = (t * TOP_K) // ROW_BLOCK + N_EXPERTS
    cnt = counts[0, :N_EXPERTS]
    padded = (cnt + ROW_BLOCK - 1) // ROW_BLOCK * ROW_BLOCK
    ends = jnp.cumsum(padded)
    starts = ends - padded
    experts, ranks = ri[:, 0:TOP_K], ri[:, TOP_K:2 * TOP_K]
    dest = starts[experts] + ranks
    tokens = jnp.broadcast_to(jnp.arange(t, dtype=jnp.int32)[:, None], (t, TOP_K))
    slot_tok = jnp.zeros((n_blocks * ROW_BLOCK,), jnp.int32).at[dest.reshape(-1)].set(tokens.reshape(-1))
    n_used = (ends[-1] // ROW_BLOCK).astype(jnp.int32)
    blocks = jnp.minimum(jnp.arange(n_blocks + 1, dtype=jnp.int32), n_used - 1) * ROW_BLOCK
    block_exp = jnp.sum((blocks[:, None] >= ends[None, :]).astype(jnp.int32), axis=1)
    block_exp = jnp.minimum(block_exp, N_EXPERTS - 1)
    cur_e = jnp.concatenate([block_exp[:1], block_exp[:-1]])
    nxt_e = block_exp
    last = (cur_e != nxt_e).astype(jnp.int32).at[0].set(1)
    return (dest.astype(jnp.int32), slot_tok.reshape(n_blocks, 1, ROW_BLOCK),
            cur_e, nxt_e, last, n_used.reshape(1))


def kernel(x, c, ln1, ln2, w_ada, b_ada, w_in, w_gk, b_gk, g_onorm, w_out,
           w_r1, b_r1, w_r2, b_r2, w_e_gate, w_e_up, w_e_down, ln_f):
    b, seq, d = x.shape
    depth = w_ada.shape[0]
    t = b * seq
    mod_all = _adaln_mod(c, w_ada, b_ada).reshape(depth, b, 6, d)
    tables = _rope_tables(seq)
    x2 = x.reshape(t, d)
    tm_c = 256
    for l in range(depth):
        mod = mod_all[l]
        w_main = w_in[l][:, :PROJ_MAIN].astype(BF16)
        w_ga = jnp.pad(w_in[l][:, PROJ_MAIN:], ((0, 0), (0, LANES - GATE_RANK))).astype(BF16)
        w_gk_p = jnp.pad(w_gk[l], ((0, LANES - GATE_RANK), (0, 0)))
        proj, z = _in_projection(x2, ln1[l][None, :], mod, w_main, w_ga, w_gk_p, b_gk[l][None, :], seq)
        proj3 = proj.reshape(b, seq, PROJ_MAIN)
        oa = _moba_attention(proj3, tables)
        ob = _gla_mixer(proj3, z.reshape(b, seq, G_KEY_WIDTH), g_onorm[l][None, :])
        w_o = w_out[l].astype(BF16)
        w_r = jnp.pad(jnp.concatenate([w_r2[l], w_r1[l]], axis=1), ((0, 0), (0, LANES - N_EXPERTS - N_GROUPS)))
        b_r = jnp.pad(jnp.concatenate([b_r2[l], b_r1[l]]), (0, LANES - N_EXPERTS - N_GROUPS))[None, :]
        x2, h2, ri, rw, counts = _outproj_router(
            oa.reshape(t, A_WIDTH), ob.reshape(t, G_VAL_WIDTH), x2, w_o[:A_WIDTH], w_o[A_WIDTH:],
            mod, ln2[l][None, :], w_r, b_r, seq)
        dest, slot_tok3, cur_e, nxt_e, last, n_used = _routing_tables(ri, counts, t)
        yg = _expert_mlp(cur_e, nxt_e, last, n_used, slot_tok3, h2, w_e_gate, w_e_up, w_e_down, l)
        dest3 = dest.reshape(t // tm_c, tm_c, TOP_K).transpose(0, 2, 1).reshape(t // tm_c, 1, TOP_K * tm_c)
        x2 = _combine(dest3, yg, rw, x2, mod, ln_f[None, :], seq, final=(l == depth - 1))
    return x2.reshape(b, seq, d)
```

```python
import functools

import jax
import jax.numpy as jnp
import numpy as np
from jax import lax
from jax.experimental import pallas as pl
from jax.experimental.pallas import tpu as pltpu

F32 = jnp.float32
BF16 = jnp.bfloat16
HIGHEST = lax.Precision.HIGHEST

HEAD_DIM = 128
A_HEADS = 8
A_WIDTH = A_HEADS * HEAD_DIM
MOBA_BLOCK = 256
MOBA_TOPK = 3
ROPE_THETA = 500000.0
ROPE_DIM = HEAD_DIM // 4
ATTN_SCALE = HEAD_DIM ** -0.5
G_HEADS = 4
G_VAL_DIM = 256
G_KEY_DIM = 128
G_KEY_WIDTH = G_HEADS * G_KEY_DIM
G_VAL_WIDTH = G_HEADS * G_VAL_DIM
GATE_RANK = 16
GATE_TAU = 16.0
GLA_CHUNK = 64
N_GROUPS = 4
EXPERTS_PER_GROUP = 8
N_EXPERTS = N_GROUPS * EXPERTS_PER_GROUP
TOP_K = 2
EPS = 1e-6

LANES = 128
PROJ_MAIN = 3 * A_WIDTH + 2 * G_KEY_WIDTH + 2 * G_VAL_WIDTH
ROW_BLOCK = 256
VMEM_LIMIT = 52 * 1024 * 1024
EXPERT_VMEM_LIMIT = 58 * 1024 * 1024


def _cparams(sem, vmem_limit=VMEM_LIMIT):
    return pltpu.CompilerParams(dimension_semantics=sem, vmem_limit_bytes=vmem_limit)


def _mod_kernel(c_ref, w_ref, b_ref, o_ref):
    c = c_ref[...]
    ca = c * jax.nn.sigmoid(c)
    o_ref[...] = jnp.dot(ca, w_ref[...], precision=HIGHEST, preferred_element_type=F32) + b_ref[...]


def _adaln_mod(c, w_ada, b_ada):
    depth, d, n6 = w_ada.shape
    b = c.shape[0]
    tn = 1024
    return pl.pallas_call(
        _mod_kernel,
        grid=(depth, n6 // tn),
        in_specs=[
            pl.BlockSpec((b, d), lambda l, n: (0, 0)),
            pl.BlockSpec((None, d, tn), lambda l, n: (l, 0, n)),
            pl.BlockSpec((None, 1, tn), lambda l, n: (l, 0, n)),
        ],
        out_specs=pl.BlockSpec((None, b, tn), lambda l, n: (l, 0, n)),
        out_shape=jax.ShapeDtypeStruct((depth, b, n6), F32),
        compiler_params=_cparams(("arbitrary", "arbitrary")),
        name="adaln_mod",
    )(c, w_ada, b_ada.reshape(depth, 1, n6))


def _modulated_norm(x, g, shift, scale):
    y = x * lax.rsqrt(jnp.mean(x * x, axis=-1, keepdims=True) + EPS) * g
    return y * (1.0 + scale) + shift


def _inproj_kernel(x_ref, ln_ref, mod_ref, w_ref, wga_ref, wgk_ref, bgk_ref, proj_ref, z_ref, h_scr):
    @pl.when(pl.program_id(1) == 0)
    def _():
        h = _modulated_norm(x_ref[...], ln_ref[...], mod_ref[0:1, :], mod_ref[1:2, :])
        hb = h.astype(BF16)
        h_scr[...] = hb
        ga = jnp.dot(hb, wga_ref[...], preferred_element_type=F32)
        z_ref[...] = jnp.dot(ga, wgk_ref[...], precision=HIGHEST, preferred_element_type=F32) + bgk_ref[...]

    proj_ref[...] = jnp.dot(h_scr[...], w_ref[...], preferred_element_type=F32).astype(proj_ref.dtype)


def _in_projection(x2, ln, mod, w_main, w_ga, w_gk, b_gk, seq):
    t, d = x2.shape
    tm, tn = 1024, 1024
    return pl.pallas_call(
        _inproj_kernel,
        grid=(t // tm, PROJ_MAIN // tn),
        in_specs=[
            pl.BlockSpec((tm, d), lambda m, n: (m, 0)),
            pl.BlockSpec((1, d), lambda m, n: (0, 0)),
            pl.BlockSpec((None, 6, d), lambda m, n: ((m * tm) // seq, 0, 0)),
            pl.BlockSpec((d, tn), lambda m, n: (0, n)),
            pl.BlockSpec((d, LANES), lambda m, n: (0, 0)),
            pl.BlockSpec((LANES, G_KEY_WIDTH), lambda m, n: (0, 0)),
            pl.BlockSpec((1, G_KEY_WIDTH), lambda m, n: (0, 0)),
        ],
        out_specs=[
            pl.BlockSpec((tm, tn), lambda m, n: (m, n)),
            pl.BlockSpec((tm, G_KEY_WIDTH), lambda m, n: (m, 0)),
        ],
        out_shape=[
            jax.ShapeDtypeStruct((t, PROJ_MAIN), BF16),
            jax.ShapeDtypeStruct((t, G_KEY_WIDTH), F32),
        ],
        scratch_shapes=[pltpu.VMEM((tm, d), BF16)],
        compiler_params=_cparams(("arbitrary", "arbitrary")),
        name="in_projection",
    )(x2, ln, mod, w_main, w_ga, w_gk, b_gk)


def _rope(x, cos, sin_lo, sin_hi):
    half = ROPE_DIM // 2
    return x * cos + pltpu.roll(x, half, 1) * sin_hi + pltpu.roll(x, LANES - half, 1) * sin_lo


def _moba_kernel(q_ref, k_ref, v_ref, cos_ref, slo_ref, shi_ref, o_ref, s_scr):
    seq = q_ref.shape[0]
    nblk = seq // MOBA_BLOCK
    cos, slo, shi = cos_ref[...], slo_ref[...], shi_ref[...]
    q = _rope(q_ref[...].astype(F32), cos, slo, shi) * ATTN_SCALE
    k = _rope(k_ref[...].astype(F32), cos, slo, shi)
    kmean = jnp.mean(k.reshape(nblk, MOBA_BLOCK, HEAD_DIM), axis=1)
    gate = lax.dot_general(kmean, q, (((1,), (1,)), ((), ())), precision=HIGHEST,
                           preferred_element_type=F32)
    blk_of_q = lax.broadcasted_iota(jnp.int32, (nblk, seq), 1) // MOBA_BLOCK
    row = lax.broadcasted_iota(jnp.int32, (nblk, seq), 0)
    past = row < blk_of_q
    better = jnp.zeros((nblk, seq), F32)
    for m in range(nblk):
        gm = gate[m:m + 1, :]
        past_m = blk_of_q[m:m + 1, :] > m
        beats = (gm > gate) | ((gm == gate) & (row > m))
        better = better + jnp.where(beats & past_m, 1.0, 0.0)
    sel = past & (better < float(MOBA_TOPK))

    qb = q.astype(BF16)
    kb = k.astype(BF16)
    vt = v_ref[...].astype(F32).T.astype(BF16)
    key_i = lax.broadcasted_iota(jnp.int32, (MOBA_BLOCK, MOBA_BLOCK), 0)
    qry_i = lax.broadcasted_iota(jnp.int32, (MOBA_BLOCK, MOBA_BLOCK), 1)
    causal = key_i <= qry_i
    for i in range(nblk):
        qs = slice(i * MOBA_BLOCK, (i + 1) * MOBA_BLOCK)
        qi = qb[qs, :]
        mx = None
        for n in range(i + 1):
            ks = slice(n * MOBA_BLOCK, (n + 1) * MOBA_BLOCK)
            s = lax.dot_general(kb[ks, :], qi, (((1,), (1,)), ((), ())),
                                preferred_element_type=F32)
            keep = causal if n == i else sel[n:n + 1, qs]
            s = jnp.where(keep, s, -jnp.inf)
            s_scr[n] = s
            bm = jnp.max(s, axis=0, keepdims=True)
            mx = bm if mx is None else jnp.maximum(mx, bm)
        den = jnp.zeros((1, MOBA_BLOCK), F32)
        acc = jnp.zeros((HEAD_DIM, MOBA_BLOCK), F32)
        for n in range(i + 1):
            ks = slice(n * MOBA_BLOCK, (n + 1) * MOBA_BLOCK)
            p = jnp.exp(s_scr[n] - mx)
            den = den + jnp.sum(p, axis=0, keepdims=True)
            acc = acc + jnp.dot(vt[:, ks], p.astype(BF16), preferred_element_type=F32)
        o_ref[qs, :] = (acc / den).T.astype(o_ref.dtype)


def _rope_tables(seq):
    half = ROPE_DIM // 2
    inv = jnp.power(ROPE_THETA, -jnp.arange(half, dtype=F32) * (2.0 / ROPE_DIM))
    ang = jnp.arange(seq, dtype=F32)[:, None] * inv[None, :]
    cos, sin = jnp.cos(ang), jnp.sin(ang)
    ones = jnp.ones((seq, HEAD_DIM - ROPE_DIM), F32)
    zeros = jnp.zeros((seq, HEAD_DIM - half), F32)
    cos_t = jnp.concatenate([cos, cos, ones], axis=1)
    sin_lo = jnp.concatenate([-sin, zeros], axis=1)
    sin_hi = jnp.concatenate([jnp.zeros((seq, half), F32), sin, zeros[:, half:]], axis=1)
    return cos_t, sin_lo, sin_hi


def _moba_attention(proj3, tables):
    b, seq, _ = proj3.shape
    head = lambda off: pl.BlockSpec((None, seq, HEAD_DIM), lambda bi, h: (bi, 0, off + h))
    tab = pl.BlockSpec((seq, HEAD_DIM), lambda bi, h: (0, 0))
    return pl.pallas_call(
        _moba_kernel,
        grid=(b, A_HEADS),
        in_specs=[head(0), head(A_HEADS), head(2 * A_HEADS), tab, tab, tab],
        out_specs=pl.BlockSpec((None, seq, HEAD_DIM), lambda bi, h: (bi, 0, h)),
        out_shape=jax.ShapeDtypeStruct((b, seq, A_WIDTH), BF16),
        scratch_shapes=[pltpu.VMEM((seq // MOBA_BLOCK, MOBA_BLOCK, MOBA_BLOCK), F32)],
        compiler_params=_cparams(("arbitrary", "arbitrary")),
        name="moba_attention",
    )(proj3, proj3, proj3, *tables)


def _log_sigmoid(z):
    return jnp.minimum(z, 0.0) - jnp.log1p(jnp.exp(-jnp.abs(z)))


def _chunk_cumsum(x):
    pos = lax.broadcasted_iota(jnp.int32, x.shape, 0) % GLA_CHUNK
    shift = 1
    while shift < GLA_CHUNK:
        x = x + jnp.where(pos >= shift, pltpu.roll(x, shift, 0), 0.0)
        shift *= 2
    return x


def _gla_kernel(q_ref, k_ref, v_ref, og_ref, z_ref, gn_ref, o_ref,
                qd_scr, ki_scr, ke_scr, dec_scr, kv_scr, st_scr, oi_scr):
    seq = q_ref.shape[0]
    nc = seq // GLA_CHUNK
    log_a = _log_sigmoid(z_ref[...]) / GATE_TAU
    b = _chunk_cumsum(log_a)
    b3 = b.reshape(nc, GLA_CHUNK, G_KEY_DIM)
    b_last = b3[:, GLA_CHUNK - 1:GLA_CHUNK, :]
    k = k_ref[...].astype(F32)
    qd_scr[...] = ((q_ref[...].astype(F32) * (G_KEY_DIM ** -0.5)) * jnp.exp(b)).astype(BF16)
    ki_scr[...] = (k * jnp.exp(-b)).astype(BF16)
    ke3 = k.reshape(nc, GLA_CHUNK, G_KEY_DIM) * jnp.exp(b_last - b3)
    ke_scr[...] = ke3.reshape(seq, G_KEY_DIM).astype(BF16)
    dec_scr[...] = jnp.exp(b_last)
    t_i = lax.broadcasted_iota(jnp.int32, (GLA_CHUNK, GLA_CHUNK), 0)
    s_i = lax.broadcasted_iota(jnp.int32, (GLA_CHUNK, GLA_CHUNK), 1)
    causal = s_i <= t_i
    gn = gn_ref[...]

    def chunk_rows(n):
        return pl.ds(pl.multiple_of(n * GLA_CHUNK, GLA_CHUNK), GLA_CHUNK)

    def intra(n, carry):
        rows = chunk_rows(n)
        vb = v_ref[rows, :].astype(BF16)
        att = lax.dot_general(qd_scr[rows, :], ki_scr[rows, :], (((1,), (1,)), ((), ())),
                              preferred_element_type=F32)
        att = jnp.where(causal, att, 0.0).astype(BF16)
        oi_scr[rows, :] = jnp.dot(att, vb, preferred_element_type=F32)
        kv_scr[n] = lax.dot_general(vb, ke_scr[rows, :], (((0,), (0,)), ((), ())),
                                    preferred_element_type=F32)
        return carry

    lax.fori_loop(0, nc, intra, 0, unroll=GLA_UNROLL)

    def scan(n, state):
        st_scr[n] = state.astype(BF16)
        return state * dec_scr[n] + kv_scr[n]

    lax.fori_loop(0, nc, scan, jnp.zeros((G_VAL_DIM, G_KEY_DIM), F32), unroll=GLA_UNROLL)

    def inter(n, carry):
        rows = chunk_rows(n)
        o = oi_scr[rows, :] + lax.dot_general(qd_scr[rows, :], st_scr[n], (((1,), (1,)), ((), ())),
                                              preferred_element_type=F32)
        o = o * lax.rsqrt(jnp.mean(o * o, axis=-1, keepdims=True) + EPS) * gn
        og = og_ref[rows, :].astype(F32)
        o_ref[rows, :] = (o * (og * jax.nn.sigmoid(og))).astype(o_ref.dtype)
        return carry

    lax.fori_loop(0, nc, inter, 0, unroll=GLA_UNROLL)


GLA_UNROLL = 4


def _gla_mixer(proj3, z3, g_onorm):
    b, seq, _ = proj3.shape
    kq = 3 * A_WIDTH // G_KEY_DIM
    kv = (3 * A_WIDTH + 2 * G_KEY_WIDTH) // G_VAL_DIM
    key = lambda off: pl.BlockSpec((None, seq, G_KEY_DIM), lambda bi, h: (bi, 0, off + h))
    val = lambda off: pl.BlockSpec((None, seq, G_VAL_DIM), lambda bi, h: (bi, 0, off + h))
    return pl.pallas_call(
        _gla_kernel,
        grid=(b, G_HEADS),
        in_specs=[
            key(kq), key(kq + G_HEADS), val(kv), val(kv + G_HEADS),
            pl.BlockSpec((None, seq, G_KEY_DIM), lambda bi, h: (bi, 0, h)),
            pl.BlockSpec((1, G_VAL_DIM), lambda bi, h: (0, 0)),
        ],
        out_specs=pl.BlockSpec((None, seq, G_VAL_DIM), lambda bi, h: (bi, 0, h)),
        out_shape=jax.ShapeDtypeStruct((b, seq, G_VAL_WIDTH), BF16),
        scratch_shapes=[
            pltpu.VMEM((seq, G_KEY_DIM), BF16),
            pltpu.VMEM((seq, G_KEY_DIM), BF16),
            pltpu.VMEM((seq, G_KEY_DIM), BF16),
            pltpu.VMEM((seq // GLA_CHUNK, 1, G_KEY_DIM), F32),
            pltpu.VMEM((seq // GLA_CHUNK, G_VAL_DIM, G_KEY_DIM), F32),
            pltpu.VMEM((seq // GLA_CHUNK, G_VAL_DIM, G_KEY_DIM), BF16),
            pltpu.VMEM((seq, G_VAL_DIM), F32),
        ],
        compiler_params=_cparams(("arbitrary", "arbitrary")),
        name="gla_mixer",
    )(proj3, proj3, proj3, proj3, z3, g_onorm)


def _first_lane(mask, lane):
    return jnp.min(jnp.where(mask, lane, LANES), axis=-1, keepdims=True)


def _split_bf16(x):
    hi = x.astype(BF16)
    return hi, (x - hi.astype(F32)).astype(BF16)


def _outproj_router_kernel(oa_ref, ob_ref, x_ref, wa_ref, wb_ref, mod_ref, ln_ref, wr_ref, br_ref,
                           xo_ref, h_ref, ri_ref, rw_ref, cnt_ref, run_scr):
    tm = x_ref.shape[0]

    @pl.when(pl.program_id(0) == 0)
    def _():
        run_scr[...] = jnp.zeros_like(run_scr)

    mix = jnp.dot(oa_ref[...], wa_ref[...], preferred_element_type=F32)
    mix = mix + jnp.dot(ob_ref[...], wb_ref[...], preferred_element_type=F32)
    x_new = x_ref[...] + mod_ref[2:3, :] * mix
    xo_ref[...] = x_new
    h = _modulated_norm(x_new, ln_ref[...], mod_ref[3:4, :], mod_ref[4:5, :])
    h_ref[...] = h

    hh = jnp.concatenate(_split_bf16(h), axis=0)
    r = jnp.dot(hh, wr_ref[...], preferred_element_type=F32)
    logits = (r[:tm, :LANES] + r[tm:, :LANES]) + (r[:tm, LANES:] + r[tm:, LANES:]) + br_ref[...]
    lane = lax.broadcasted_iota(jnp.int32, (tm, LANES), 1)
    is_grp = (lane >= N_EXPERTS) & (lane < N_EXPERTS + N_GROUPS)
    l1 = jnp.where(is_grp, logits, -jnp.inf)
    m1 = jnp.max(l1, axis=-1, keepdims=True)
    grp = _first_lane(l1 == m1, lane) - N_EXPERTS
    p_grp = 1.0 / jnp.sum(jnp.exp(l1 - m1), axis=-1, keepdims=True)
    in_grp = (lane < N_EXPERTS) & ((lane // EXPERTS_PER_GROUP) == grp)
    l2 = jnp.where(in_grp, logits, -jnp.inf)
    va = jnp.max(l2, axis=-1, keepdims=True)
    ia = _first_lane(l2 == va, lane)
    l2b = jnp.where(lane == ia, -jnp.inf, l2)
    vb = jnp.max(l2b, axis=-1, keepdims=True)
    ib = _first_lane(l2b == vb, lane)
    eb = jnp.exp(vb - va)
    wa = p_grp * (1.0 / (1.0 + eb))
    wb = p_grp * (eb / (1.0 + eb))

    oh_a = jnp.where(lane == ia, 1.0, 0.0)
    oh_b = jnp.where(lane == ib, 1.0, 0.0)
    cnt = oh_a + oh_b
    r_i = lax.broadcasted_iota(jnp.int32, (tm, tm), 0)
    c_i = lax.broadcasted_iota(jnp.int32, (tm, tm), 1)
    strict_lower = jnp.where(c_i < r_i, 1.0, 0.0).astype(BF16)
    before = jnp.dot(strict_lower, cnt.astype(BF16), preferred_element_type=F32) + run_scr[...]
    rank_a = jnp.sum(oh_a * before, axis=-1, keepdims=True)
    rank_b = jnp.sum(oh_b * before, axis=-1, keepdims=True)
    run_new = run_scr[...] + jnp.sum(cnt, axis=0, keepdims=True)
    run_scr[...] = run_new
    cnt_ref[...] = jnp.broadcast_to(run_new, cnt_ref.shape).astype(jnp.int32)

    ri = jnp.where(lane == 0, ia, 0) + jnp.where(lane == 1, ib, 0)
    ri = ri + jnp.where(lane == 2, rank_a.astype(jnp.int32), 0) + jnp.where(lane == 3, rank_b.astype(jnp.int32), 0)
    ri_ref[...] = ri
    rw_ref[...] = jnp.where(lane == 0, wa, 0.0) + jnp.where(lane == 1, wb, 0.0)


def _outproj_router(oa2, ob2, x2, w_oa, w_ob, mod, ln, w_r, b_r, seq):
    t, d = x2.shape
    tm = 256
    row = lambda w: pl.BlockSpec((tm, w), lambda m: (m, 0))
    full = lambda a, c: pl.BlockSpec((a, c), lambda m: (0, 0))
    return pl.pallas_call(
        _outproj_router_kernel,
        grid=(t // tm,),
        in_specs=[
            row(A_WIDTH), row(G_VAL_WIDTH), row(d),
            full(A_WIDTH, d), full(G_VAL_WIDTH, d),
            pl.BlockSpec((None, 6, d), lambda m: ((m * tm) // seq, 0, 0)),
            full(1, d), full(d, 2 * LANES), full(1, LANES),
        ],
        out_specs=[row(d), row(d), row(LANES), row(LANES), full(8, LANES)],
        out_shape=[
            jax.ShapeDtypeStruct((t, d), F32),
            jax.ShapeDtypeStruct((t, d), F32),
            jax.ShapeDtypeStruct((t, LANES), jnp.int32),
            jax.ShapeDtypeStruct((t, LANES), F32),
            jax.ShapeDtypeStruct((8, LANES), jnp.int32),
        ],
        scratch_shapes=[pltpu.VMEM((1, LANES), F32)],
        compiler_params=_cparams(("arbitrary",)),
        name="outproj_router",
    )(oa2, ob2, x2, w_oa, w_ob, mod, ln, w_r, b_r)


def _start_row_gather(idx_ref, first, n_rows, src_hbm, dst_vmem, sem):
    def issue(i, carry):
        r = first + i
        pltpu.make_async_copy(src_hbm.at[pl.ds(idx_ref[0, r], 1), :],
                              dst_vmem.at[pl.ds(r, 1), :], sem).start()
        return carry

    lax.fori_loop(0, n_rows, issue, 0, unroll=8)


def _wait_row_gather(src_hbm, dst_vmem, sem):
    pltpu.make_async_copy(src_hbm.at[pl.ds(0, dst_vmem.shape[0]), :], dst_vmem, sem).wait()


def _gather_rows(idx_ref, n_rows, src_hbm, dst_vmem, sem):
    _start_row_gather(idx_ref, 0, n_rows, src_hbm, dst_vmem, sem)
    _wait_row_gather(src_hbm, dst_vmem, sem)


def _expert_kernel(cur_ref, nxt_ref, last_ref, nused_ref, tok_ref, h_hbm, wg_ref, wu_ref, wd_ref,
                   y_ref, wg_res, wu_res, wd_res, x_scr, xb_scr, sems):
    del cur_ref, nxt_ref
    b, j = pl.program_id(0), pl.program_id(1)
    n_used = nused_ref[0]
    rows_per_step = ROW_BLOCK // EXPERT_HIDDEN_CHUNKS

    for jj in range(EXPERT_HIDDEN_CHUNKS):
        @pl.when((b < n_used) & (j == jj))
        def _(jj=jj):
            slot = b % 2
            for r in range(jj * rows_per_step, (jj + 1) * rows_per_step):
                pltpu.make_async_copy(h_hbm.at[pl.ds(tok_ref[0, r], 1), :],
                                      x_scr.at[slot, pl.ds(r, 1), :], sems.at[slot]).start()

    @pl.when((b >= 1) & (b <= n_used))
    def _():
        @pl.when(j == 0)
        def _():
            slot = (b - 1) % 2
            _wait_row_gather(h_hbm, x_scr.at[slot], sems.at[slot])
            xb_scr[...] = x_scr[slot].astype(BF16)

        xb = xb_scr[...]
        g = jnp.dot(xb, wg_res[j], preferred_element_type=F32)
        u = jnp.dot(xb, wu_res[j], preferred_element_type=F32)
        a = ((g * jax.nn.sigmoid(g)) * u).astype(BF16)
        part = jnp.dot(a, wd_res[j], preferred_element_type=F32)

        @pl.when(j == 0)
        def _():
            y_ref[...] = part

        @pl.when(j > 0)
        def _():
            y_ref[...] += part

    @pl.when((b > n_used) & (j == 0))
    def _():
        y_ref[...] = jnp.zeros_like(y_ref)

    @pl.when(last_ref[b] == 1)
    def _():
        wg_res[j] = wg_ref[...].astype(BF16)
        wu_res[j] = wu_ref[...].astype(BF16)
        wd_res[j] = wd_ref[...].astype(BF16)


EXPERT_HIDDEN_CHUNKS = 2


def _expert_mlp(cur_e, nxt_e, last, n_used, slot_tok3, h2, wg, wu, wd, layer):
    nb = slot_tok3.shape[0]
    d = h2.shape[1]
    f = wg.shape[3]
    nj = EXPERT_HIDDEN_CHUNKS
    fc = f // nj

    def w_index(chunk_axis):
        def index(b, j, cur, nxt, lst, nu):
            e = jnp.where(lst[b] == 1, nxt[b], cur[b])
            jj = jnp.where(lst[b] == 1, j, nj - 1)
            return (layer, e, 0, jj) if chunk_axis == 2 else (layer, e, jj, 0)
        return index

    grid_spec = pltpu.PrefetchScalarGridSpec(
        num_scalar_prefetch=4,
        grid=(nb + 1, nj),
        in_specs=[
            pl.BlockSpec((None, 1, ROW_BLOCK), lambda b, j, *_: (jnp.minimum(b, nb - 1), 0, 0),
                         memory_space=pltpu.SMEM),
            pl.BlockSpec(memory_space=pl.ANY),
            pl.BlockSpec((None, None, d, fc), w_index(2)),
            pl.BlockSpec((None, None, d, fc), w_index(2)),
            pl.BlockSpec((None, None, fc, d), w_index(1)),
        ],
        out_specs=pl.BlockSpec((ROW_BLOCK, d), lambda b, j, *_: (jnp.maximum(b - 1, 0), 0)),
        scratch_shapes=[
            pltpu.VMEM((nj, d, fc), BF16),
            pltpu.VMEM((nj, d, fc), BF16),
            pltpu.VMEM((nj, fc, d), BF16),
            pltpu.VMEM((2, ROW_BLOCK, d), F32),
            pltpu.VMEM((ROW_BLOCK, d), BF16),
            pltpu.SemaphoreType.DMA((2,)),
        ],
    )
    return pl.pallas_call(
        _expert_kernel,
        grid_spec=grid_spec,
        out_shape=jax.ShapeDtypeStruct((nb * ROW_BLOCK, d), F32),
        compiler_params=_cparams(("arbitrary", "arbitrary"), EXPERT_VMEM_LIMIT),
        name="expert_mlp",
    )(cur_e, nxt_e, last, n_used, slot_tok3, h2, wg, wu, wd)


def _combine_kernel(dest_ref, y_hbm, rw_ref, x_ref, mod_ref, lnf_ref, o_ref, y_scr, sem, *, final):
    tm = x_ref.shape[0]
    _gather_rows(dest_ref, 2 * tm, y_hbm, y_scr, sem)
    rw = rw_ref[...]
    y = rw[:, 0:1] * y_scr[0:tm, :] + rw[:, 1:2] * y_scr[tm:2 * tm, :]
    x_new = x_ref[...] + mod_ref[5:6, :] * y
    if final:
        x_new = x_new * lax.rsqrt(jnp.mean(x_new * x_new, axis=-1, keepdims=True) + EPS) * lnf_ref[...]
    o_ref[...] = x_new


def _combine(dest3, yg, rw, x2, mod, ln_f, seq, final):
    t, d = x2.shape
    tm = dest3.shape[2] // 2
    return pl.pallas_call(
        functools.partial(_combine_kernel, final=final),
        grid=(t // tm,),
        in_specs=[
            pl.BlockSpec((None, 1, 2 * tm), lambda m: (m, 0, 0), memory_space=pltpu.SMEM),
            pl.BlockSpec(memory_space=pl.ANY),
            pl.BlockSpec((tm, LANES), lambda m: (m, 0)),
            pl.BlockSpec((tm, d), lambda m: (m, 0)),
            pl.BlockSpec((None, 6, d), lambda m: ((m * tm) // seq, 0, 0)),
            pl.BlockSpec((1, d), lambda m: (0, 0)),
        ],
        out_specs=pl.BlockSpec((tm, d), lambda m: (m, 0)),
        out_shape=jax.ShapeDtypeStruct((t, d), F32),
        scratch_shapes=[pltpu.VMEM((2 * tm, d), F32), pltpu.SemaphoreType.DMA(())],
        compiler_params=_cparams(("arbitrary",)),
        name="moe_combine",
    )(dest3, yg, rw, x2, mod, ln_f)


def _routing_tables(ri, counts, t):
    n_blocks = (t * TOP_K) // ROW_BLOCK + N_EXPERTS
    cnt = counts[0, :N_EXPERTS]
    padded = (cnt + ROW_BLOCK - 1) // ROW_BLOCK * ROW_BLOCK
    ends = jnp.cumsum(padded)
    starts = ends - padded
    experts, ranks = ri[:, 0:TOP_K], ri[:, TOP_K:2 * TOP_K]
    dest = starts[experts] + ranks
    tokens = jnp.broadcast_to(jnp.arange(t, dtype=jnp.int32)[:, None], (t, TOP_K))
    slot_tok = jnp.zeros((n_blocks * ROW_BLOCK,), jnp.int32).at[dest.reshape(-1)].set(tokens.reshape(-1))
    n_used = (ends[-1] // ROW_BLOCK).astype(jnp.int32)
    blocks = jnp.minimum(jnp.arange(n_blocks + 1, dtype=jnp.int32), n_used - 1) * ROW_BLOCK
    block_exp = jnp.sum((blocks[:, None] >= ends[None, :]).astype(jnp.int32), axis=1)
    block_exp = jnp.minimum(block_exp, N_EXPERTS - 1)
    cur_e = jnp.concatenate([block_exp[:1], block_exp[:-1]])
    nxt_e = block_exp
    last = (cur_e != nxt_e).astype(jnp.int32).at[0].set(1)
    return (dest.astype(jnp.int32), slot_tok.reshape(n_blocks, 1, ROW_BLOCK),
            cur_e, nxt_e, last, n_used.reshape(1))


def kernel(x, c, ln1, ln2, w_ada, b_ada, w_in, w_gk, b_gk, g_onorm, w_out,
           w_r1, b_r1, w_r2, b_r2, w_e_gate, w_e_up, w_e_down, ln_f):
    b, seq, d = x.shape
    depth = w_ada.shape[0]
    t = b * seq
    mod_all = _adaln_mod(c, w_ada, b_ada).reshape(depth, b, 6, d)
    tables = _rope_tables(seq)
    x2 = x.reshape(t, d)
    tm_c = 256
    for l in range(depth):
        mod = mod_all[l]
        w_main = w_in[l][:, :PROJ_MAIN].astype(BF16)
        w_ga = jnp.pad(w_in[l][:, PROJ_MAIN:], ((0, 0), (0, LANES - GATE_RANK))).astype(BF16)
        w_gk_p = jnp.pad(w_gk[l], ((0, LANES - GATE_RANK), (0, 0)))
        proj, z = _in_projection(x2, ln1[l][None, :], mod, w_main, w_ga, w_gk_p, b_gk[l][None, :], seq)
        proj3 = proj.reshape(b, seq, PROJ_MAIN)
        oa = _moba_attention(proj3, tables)
        ob = _gla_mixer(proj3, z.reshape(b, seq, G_KEY_WIDTH), g_onorm[l][None, :])
        w_o = w_out[l].astype(BF16)
        w_r = jnp.pad(jnp.concatenate([w_r2[l], w_r1[l]], axis=1), ((0, 0), (0, LANES - N_EXPERTS - N_GROUPS)))
        w_r_hi = w_r.astype(BF16)
        w_r = jnp.concatenate([w_r_hi, (w_r - w_r_hi.astype(F32)).astype(BF16)], axis=1)
        b_r = jnp.pad(jnp.concatenate([b_r2[l], b_r1[l]]), (0, LANES - N_EXPERTS - N_GROUPS))[None, :]
        x2, h2, ri, rw, counts = _outproj_router(
            oa.reshape(t, A_WIDTH), ob.reshape(t, G_VAL_WIDTH), x2, w_o[:A_WIDTH], w_o[A_WIDTH:],
            mod, ln2[l][None, :], w_r, b_r, seq)
        dest, slot_tok3, cur_e, nxt_e, last, n_used = _routing_tables(ri, counts, t)
        yg = _expert_mlp(cur_e, nxt_e, last, n_used, slot_tok3, h2, w_e_gate, w_e_up, w_e_down, l)
        dest3 = dest.reshape(t // tm_c, tm_c, TOP_K).transpose(0, 2, 1).reshape(t // tm_c, 1, TOP_K * tm_c)
        x2 = _combine(dest3, yg, rw, x2, mod, ln_f[None, :], seq, final=(l == depth - 1))
    return x2.reshape(b, seq, d)
```

```python
import functools

import jax
import jax.numpy as jnp
import numpy as np
from jax import lax
from jax.experimental import pallas as pl
from jax.experimental.pallas import tpu as pltpu

F32 = jnp.float32
BF16 = jnp.bfloat16
HIGHEST = lax.Precision.HIGHEST

HEAD_DIM = 128
A_HEADS = 8
A_WIDTH = A_HEADS * HEAD_DIM
MOBA_BLOCK = 256
MOBA_TOPK = 3
ROPE_THETA = 500000.0
ROPE_DIM = HEAD_DIM // 4
ATTN_SCALE = HEAD_DIM ** -0.5
G_HEADS = 4
G_VAL_DIM = 256
G_KEY_DIM = 128
G_KEY_WIDTH = G_HEADS * G_KEY_DIM
G_VAL_WIDTH = G_HEADS * G_VAL_DIM
GATE_RANK = 16
GATE_TAU = 16.0
GLA_CHUNK = 64
N_GROUPS = 4
EXPERTS_PER_GROUP = 8
N_EXPERTS = N_GROUPS * EXPERTS_PER_GROUP
TOP_K = 2
EPS = 1e-6

LANES = 128
PROJ_MAIN = 3 * A_WIDTH + 2 * G_KEY_WIDTH + 2 * G_VAL_WIDTH
ROW_BLOCK = 256
VMEM_LIMIT = 52 * 1024 * 1024
EXPERT_VMEM_LIMIT = 58 * 1024 * 1024


def _cparams(sem, vmem_limit=VMEM_LIMIT):
    return pltpu.CompilerParams(dimension_semantics=sem, vmem_limit_bytes=vmem_limit)


def _mod_kernel(c_ref, w_ref, b_ref, o_ref):
    c = c_ref[...]
    ca = c * jax.nn.sigmoid(c)
    o_ref[...] = jnp.dot(ca, w_ref[...], precision=HIGHEST, preferred_element_type=F32) + b_ref[...]


def _adaln_mod(c, w_ada, b_ada):
    depth, d, n6 = w_ada.shape
    b = c.shape[0]
    tn = 1024
    return pl.pallas_call(
        _mod_kernel,
        grid=(depth, n6 // tn),
        in_specs=[
            pl.BlockSpec((b, d), lambda l, n: (0, 0)),
            pl.BlockSpec((None, d, tn), lambda l, n: (l, 0, n)),
            pl.BlockSpec((None, 1, tn), lambda l, n: (l, 0, n)),
        ],
        out_specs=pl.BlockSpec((None, b, tn), lambda l, n: (l, 0, n)),
        out_shape=jax.ShapeDtypeStruct((depth, b, n6), F32),
        compiler_params=_cparams(("arbitrary", "arbitrary")),
        name="adaln_mod",
    )(c, w_ada, b_ada.reshape(depth, 1, n6))


def _modulated_norm(x, g, shift, scale):
    y = x * lax.rsqrt(jnp.mean(x * x, axis=-1, keepdims=True) + EPS) * g
    return y * (1.0 + scale) + shift


def _inproj_kernel(x_ref, ln_ref, mod_ref, w_ref, wga_ref, wgk_ref, bgk_ref, proj_ref, z_ref, h_scr):
    @pl.when(pl.program_id(1) == 0)
    def _():
        h = _modulated_norm(x_ref[...], ln_ref[...], mod_ref[0:1, :], mod_ref[1:2, :])
        hb = h.astype(BF16)
        h_scr[...] = hb
        ga = jnp.dot(hb, wga_ref[...], preferred_element_type=F32)
        z_ref[...] = jnp.dot(ga, wgk_ref[...], precision=HIGHEST, preferred_element_type=F32) + bgk_ref[...]

    proj_ref[...] = jnp.dot(h_scr[...], w_ref[...], preferred_element_type=F32).astype(proj_ref.dtype)


def _in_projection(x2, ln, mod, w_main, w_ga, w_gk, b_gk, seq):
    t, d = x2.shape
    tm, tn = 1024, 1024
    return pl.pallas_call(
        _inproj_kernel,
        grid=(t // tm, PROJ_MAIN // tn),
        in_specs=[
            pl.BlockSpec((tm, d), lambda m, n: (m, 0)),
            pl.BlockSpec((1, d), lambda m, n: (0, 0)),
            pl.BlockSpec((None, 6, d), lambda m, n: ((m * tm) // seq, 0, 0)),
            pl.BlockSpec((d, tn), lambda m, n: (0, n)),
            pl.BlockSpec((d, LANES), lambda m, n: (0, 0)),
            pl.BlockSpec((LANES, G_KEY_WIDTH), lambda m, n: (0, 0)),
            pl.BlockSpec((1, G_KEY_WIDTH), lambda m, n: (0, 0)),
        ],
        out_specs=[
            pl.BlockSpec((tm, tn), lambda m, n: (m, n)),
            pl.BlockSpec((tm, G_KEY_WIDTH), lambda m, n: (m, 0)),
        ],
        out_shape=[
            jax.ShapeDtypeStruct((t, PROJ_MAIN), BF16),
            jax.ShapeDtypeStruct((t, G_KEY_WIDTH), F32),
        ],
        scratch_shapes=[pltpu.VMEM((tm, d), BF16)],
        compiler_params=_cparams(("arbitrary", "arbitrary")),
        name="in_projection",
    )(x2, ln, mod, w_main, w_ga, w_gk, b_gk)


def _rope(x, cos, sin_lo, sin_hi):
    half = ROPE_DIM // 2
    return x * cos + pltpu.roll(x, half, 1) * sin_hi + pltpu.roll(x, LANES - half, 1) * sin_lo


def _moba_kernel(q_ref, k_ref, v_ref, cos_ref, slo_ref, shi_ref, o_ref, s_scr):
    seq = q_ref.shape[0]
    nblk = seq // MOBA_BLOCK
    cos, slo, shi = cos_ref[...], slo_ref[...], shi_ref[...]
    q = _rope(q_ref[...].astype(F32), cos, slo, shi) * ATTN_SCALE
    k = _rope(k_ref[...].astype(F32), cos, slo, shi)
    kmean = jnp.mean(k.reshape(nblk, MOBA_BLOCK, HEAD_DIM), axis=1)
    gate = lax.dot_general(kmean, q, (((1,), (1,)), ((), ())), precision=HIGHEST,
                           preferred_element_type=F32)
    blk_of_q = lax.broadcasted_iota(jnp.int32, (nblk, seq), 1) // MOBA_BLOCK
    row = lax.broadcasted_iota(jnp.int32, (nblk, seq), 0)
    past = row < blk_of_q
    better = jnp.zeros((nblk, seq), F32)
    for m in range(nblk):
        gm = gate[m:m + 1, :]
        past_m = blk_of_q[m:m + 1, :] > m
        beats = (gm > gate) | ((gm == gate) & (row > m))
        better = better + jnp.where(beats & past_m, 1.0, 0.0)
    sel = past & (better < float(MOBA_TOPK))

    qb = q.astype(BF16)
    kb = k.astype(BF16)
    vt = v_ref[...].astype(F32).T.astype(BF16)
    key_i = lax.broadcasted_iota(jnp.int32, (MOBA_BLOCK, MOBA_BLOCK), 0)
    qry_i = lax.broadcasted_iota(jnp.int32, (MOBA_BLOCK, MOBA_BLOCK), 1)
    causal = key_i <= qry_i
    for i in range(nblk):
        qs = slice(i * MOBA_BLOCK, (i + 1) * MOBA_BLOCK)
        qi = qb[qs, :]
        mx = None
        for n in range(i + 1):
            ks = slice(n * MOBA_BLOCK, (n + 1) * MOBA_BLOCK)
            s = lax.dot_general(kb[ks, :], qi, (((1,), (1,)), ((), ())),
                                preferred_element_type=F32)
            keep = causal if n == i else sel[n:n + 1, qs]
            s = jnp.where(keep, s, -jnp.inf)
            s_scr[n] = s
            bm = jnp.max(s, axis=0, keepdims=True)
            mx = bm if mx is None else jnp.maximum(mx, bm)
        den = jnp.zeros((1, MOBA_BLOCK), F32)
        acc = jnp.zeros((HEAD_DIM, MOBA_BLOCK), F32)
        for n in range(i + 1):
            ks = slice(n * MOBA_BLOCK, (n + 1) * MOBA_BLOCK)
            p = jnp.exp(s_scr[n] - mx)
            den = den + jnp.sum(p, axis=0, keepdims=True)
            acc = acc + jnp.dot(vt[:, ks], p.astype(BF16), preferred_element_type=F32)
        o_ref[qs, :] = (acc / den).T.astype(o_ref.dtype)


def _rope_tables(seq):
    half = ROPE_DIM // 2
    inv = jnp.power(ROPE_THETA, -jnp.arange(half, dtype=F32) * (2.0 / ROPE_DIM))
    ang = jnp.arange(seq, dtype=F32)[:, None] * inv[None, :]
    cos, sin = jnp.cos(ang), jnp.sin(ang)
    ones = jnp.ones((seq, HEAD_DIM - ROPE_DIM), F32)
    zeros = jnp.zeros((seq, HEAD_DIM - half), F32)
    cos_t = jnp.concatenate([cos, cos, ones], axis=1)
    sin_lo = jnp.concatenate([-sin, zeros], axis=1)
    sin_hi = jnp.concatenate([jnp.zeros((seq, half), F32), sin, zeros[:, half:]], axis=1)
    return cos_t, sin_lo, sin_hi


def _moba_attention(proj3, tables):
    b, seq, _ = proj3.shape
    head = lambda off: pl.BlockSpec((None, seq, HEAD_DIM), lambda bi, h: (bi, 0, off + h))
    tab = pl.BlockSpec((seq, HEAD_DIM), lambda bi, h: (0, 0))
    return pl.pallas_call(
        _moba_kernel,
        grid=(b, A_HEADS),
        in_specs=[head(0), head(A_HEADS), head(2 * A_HEADS), tab, tab, tab],
        out_specs=pl.BlockSpec((None, seq, HEAD_DIM), lambda bi, h: (bi, 0, h)),
        out_shape=jax.ShapeDtypeStruct((b, seq, A_WIDTH), BF16),
        scratch_shapes=[pltpu.VMEM((seq // MOBA_BLOCK, MOBA_BLOCK, MOBA_BLOCK), F32)],
        compiler_params=_cparams(("arbitrary", "arbitrary")),
        name="moba_attention",
    )(proj3, proj3, proj3, *tables)


def _log_sigmoid(z):
    return jnp.minimum(z, 0.0) - jnp.log1p(jnp.exp(-jnp.abs(z)))


def _chunk_cumsum(x):
    pos = lax.broadcasted_iota(jnp.int32, x.shape, 0) % GLA_CHUNK
    shift = 1
    while shift < GLA_CHUNK:
        x = x + jnp.where(pos >= shift, pltpu.roll(x, shift, 0), 0.0)
        shift *= 2
    return x


def _gla_kernel(q_ref, k_ref, v_ref, og_ref, z_ref, gn_ref, o_ref,
                qd_scr, ki_scr, ke_scr, dec_scr, kv_scr, st_scr, oi_scr):
    seq = q_ref.shape[0]
    nc = seq // GLA_CHUNK
    log_a = _log_sigmoid(z_ref[...]) / GATE_TAU
    b = _chunk_cumsum(log_a)
    b3 = b.reshape(nc, GLA_CHUNK, G_KEY_DIM)
    b_last = b3[:, GLA_CHUNK - 1:GLA_CHUNK, :]
    k = k_ref[...].astype(F32)
    qd_scr[...] = ((q_ref[...].astype(F32) * (G_KEY_DIM ** -0.5)) * jnp.exp(b)).astype(BF16)
    ki_scr[...] = (k * jnp.exp(-b)).astype(BF16)
    ke3 = k.reshape(nc, GLA_CHUNK, G_KEY_DIM) * jnp.exp(b_last - b3)
    ke_scr[...] = ke3.reshape(seq, G_KEY_DIM).astype(BF16)
    dec_scr[...] = jnp.exp(b_last)
    t_i = lax.broadcasted_iota(jnp.int32, (GLA_CHUNK, GLA_CHUNK), 0)
    s_i = lax.broadcasted_iota(jnp.int32, (GLA_CHUNK, GLA_CHUNK), 1)
    causal = s_i <= t_i
    gn = gn_ref[...]

    def chunk_rows(n):
        return pl.ds(pl.multiple_of(n * GLA_CHUNK, GLA_CHUNK), GLA_CHUNK)

    def intra(n, carry):
        rows = chunk_rows(n)
        vb = v_ref[rows, :].astype(BF16)
        att = lax.dot_general(qd_scr[rows, :], ki_scr[rows, :], (((1,), (1,)), ((), ())),
                              preferred_element_type=F32)
        att = jnp.where(causal, att, 0.0).astype(BF16)
        oi_scr[rows, :] = jnp.dot(att, vb, preferred_element_type=F32)
        kv_scr[n] = lax.dot_general(vb, ke_scr[rows, :], (((0,), (0,)), ((), ())),
                                    preferred_element_type=F32)
        return carry

    lax.fori_loop(0, nc, intra, 0, unroll=GLA_UNROLL)

    def scan(n, state):
        st_scr[n] = state.astype(BF16)
        return state * dec_scr[n] + kv_scr[n]

    lax.fori_loop(0, nc, scan, jnp.zeros((G_VAL_DIM, G_KEY_DIM), F32), unroll=GLA_UNROLL)

    def inter(n, carry):
        rows = chunk_rows(n)
        o = oi_scr[rows, :] + lax.dot_general(qd_scr[rows, :], st_scr[n], (((1,), (1,)), ((), ())),
                                              preferred_element_type=F32)
        o = o * lax.rsqrt(jnp.mean(o * o, axis=-1, keepdims=True) + EPS) * gn
        og = og_ref[rows, :].astype(F32)
        o_ref[rows, :] = (o * (og * jax.nn.sigmoid(og))).astype(o_ref.dtype)
        return carry

    lax.fori_loop(0, nc, inter, 0, unroll=GLA_UNROLL)


GLA_UNROLL = 4


def _gla_mixer(proj3, z3, g_onorm):
    b, seq, _ = proj3.shape
    kq = 3 * A_WIDTH // G_KEY_DIM
    kv = (3 * A_WIDTH + 2 * G_KEY_WIDTH) // G_VAL_DIM
    key = lambda off: pl.BlockSpec((None, seq, G_KEY_DIM), lambda bi, h: (bi, 0, off + h))
    val = lambda off: pl.BlockSpec((None, seq, G_VAL_DIM), lambda bi, h: (bi, 0, off + h))
    return pl.pallas_call(
        _gla_kernel,
        grid=(b, G_HEADS),
        in_specs=[
            key(kq), key(kq + G_HEADS), val(kv), val(kv + G_HEADS),
            pl.BlockSpec((None, seq, G_KEY_DIM), lambda bi, h: (bi, 0, h)),
            pl.BlockSpec((1, G_VAL_DIM), lambda bi, h: (0, 0)),
        ],
        out_specs=pl.BlockSpec((None, seq, G_VAL_DIM), lambda bi, h: (bi, 0, h)),
        out_shape=jax.ShapeDtypeStruct((b, seq, G_VAL_WIDTH), BF16),
        scratch_shapes=[
            pltpu.VMEM((seq, G_KEY_DIM), BF16),
            pltpu.VMEM((seq, G_KEY_DIM), BF16),
            pltpu.VMEM((seq, G_KEY_DIM), BF16),
            pltpu.VMEM((seq // GLA_CHUNK, 1, G_KEY_DIM), F32),
            pltpu.VMEM((seq // GLA_CHUNK, G_VAL_DIM, G_KEY_DIM), F32),
            pltpu.VMEM((seq // GLA_CHUNK, G_VAL_DIM, G_KEY_DIM), BF16),
            pltpu.VMEM((seq, G_VAL_DIM), F32),
        ],
        compiler_params=_cparams(("arbitrary", "arbitrary")),
        name="gla_mixer",
    )(proj3, proj3, proj3, proj3, z3, g_onorm)


def _first_lane(mask, lane):
    return jnp.min(jnp.where(mask, lane, LANES), axis=-1, keepdims=True)


def _split_bf16(x):
    hi = x.astype(BF16)
    return hi, (x - hi.astype(F32)).astype(BF16)


def _outproj_router_kernel(oa_ref, ob_ref, x_ref, wa_ref, wb_ref, mod_ref, ln_ref, wr_ref, br_ref,
                           xo_ref, h_ref, ri_ref, rw_ref, cnt_ref, run_scr):
    tm = x_ref.shape[0]

    @pl.when(pl.program_id(0) == 0)
    def _():
        run_scr[...] = jnp.zeros_like(run_scr)

    mix = jnp.dot(oa_ref[...], wa_ref[...], preferred_element_type=F32)
    mix = mix + jnp.dot(ob_ref[...], wb_ref[...], preferred_element_type=F32)
    x_new = x_ref[...] + mod_ref[2:3, :] * mix
    xo_ref[...] = x_new
    h = _modulated_norm(x_new, ln_ref[...], mod_ref[3:4, :], mod_ref[4:5, :])
    h_ref[...] = h

    hh = jnp.concatenate(_split_bf16(h), axis=0)
    r = jnp.dot(hh, wr_ref[...], preferred_element_type=F32)
    logits = (r[:tm, :LANES] + r[tm:, :LANES]) + (r[:tm, LANES:] + r[tm:, LANES:]) + br_ref[...]
    lane = lax.broadcasted_iota(jnp.int32, (tm, LANES), 1)
    is_grp = (lane >= N_EXPERTS) & (lane < N_EXPERTS + N_GROUPS)
    l1 = jnp.where(is_grp, logits, -jnp.inf)
    m1 = jnp.max(l1, axis=-1, keepdims=True)
    grp = _first_lane(l1 == m1, lane) - N_EXPERTS
    p_grp = 1.0 / jnp.sum(jnp.exp(l1 - m1), axis=-1, keepdims=True)
    in_grp = (lane < N_EXPERTS) & ((lane // EXPERTS_PER_GROUP) == grp)
    l2 = jnp.where(in_grp, logits, -jnp.inf)
    va = jnp.max(l2, axis=-1, keepdims=True)
    ia = _first_lane(l2 == va, lane)
    l2b = jnp.where(lane == ia, -jnp.inf, l2)
    vb = jnp.max(l2b, axis=-1, keepdims=True)
    ib = _first_lane(l2b == vb, lane)
    eb = jnp.exp(vb - va)
    wa = p_grp * (1.0 / (1.0 + eb))
    wb = p_grp * (eb / (1.0 + eb))

    oh_a = jnp.where(lane == ia, 1.0, 0.0)
    oh_b = jnp.where(lane == ib, 1.0, 0.0)
    cnt = oh_a + oh_b
    r_i = lax.broadcasted_iota(jnp.int32, (tm, tm), 0)
    c_i = lax.broadcasted_iota(jnp.int32, (tm, tm), 1)
    strict_lower = jnp.where(c_i < r_i, 1.0, 0.0).astype(BF16)
    before = jnp.dot(strict_lower, cnt.astype(BF16), preferred_element_type=F32) + run_scr[...]
    rank_a = jnp.sum(oh_a * before, axis=-1, keepdims=True)
    rank_b = jnp.sum(oh_b * before, axis=-1, keepdims=True)
    run_new = run_scr[...] + jnp.sum(cnt, axis=0, keepdims=True)
    run_scr[...] = run_new
    cnt_ref[...] = jnp.broadcast_to(run_new, cnt_ref.shape).astype(jnp.int32)

    ri = jnp.where(lane == 0, ia, 0) + jnp.where(lane == 1, ib, 0)
    ri = ri + jnp.where(lane == 2, rank_a.astype(jnp.int32), 0) + jnp.where(lane == 3, rank_b.astype(jnp.int32), 0)
    ri_ref[...] = ri
    rw_ref[...] = jnp.where(lane == 0, wa, 0.0) + jnp.where(lane == 1, wb, 0.0)


def _outproj_router(oa2, ob2, x2, w_oa, w_ob, mod, ln, w_r, b_r, seq):
    t, d = x2.shape
    tm = 256
    row = lambda w: pl.BlockSpec((tm, w), lambda m: (m, 0))
    full = lambda a, c: pl.BlockSpec((a, c), lambda m: (0, 0))
    return pl.pallas_call(
        _outproj_router_kernel,
        grid=(t // tm,),
        in_specs=[
            row(A_WIDTH), row(G_VAL_WIDTH), row(d),
            full(A_WIDTH, d), full(G_VAL_WIDTH, d),
            pl.BlockSpec((None, 6, d), lambda m: ((m * tm) // seq, 0, 0)),
            full(1, d), full(d, 2 * LANES), full(1, LANES),
        ],
        out_specs=[row(d), row(d), row(LANES), row(LANES), full(8, LANES)],
        out_shape=[
            jax.ShapeDtypeStruct((t, d), F32),
            jax.ShapeDtypeStruct((t, d), F32),
            jax.ShapeDtypeStruct((t, LANES), jnp.int32),
            jax.ShapeDtypeStruct((t, LANES), F32),
            jax.ShapeDtypeStruct((8, LANES), jnp.int32),
        ],
        scratch_shapes=[pltpu.VMEM((1, LANES), F32)],
        compiler_params=_cparams(("arbitrary",)),
        name="outproj_router",
    )(oa2, ob2, x2, w_oa, w_ob, mod, ln, w_r, b_r)


def _wait_row_gather(src_hbm, dst_vmem, sem):
    pltpu.make_async_copy(src_hbm.at[pl.ds(0, dst_vmem.shape[0]), :], dst_vmem, sem).wait()


def _expert_kernel(cur_ref, nxt_ref, last_ref, nused_ref, tok_ref, h_hbm, wg_ref, wu_ref, wd_ref,
                   y_ref, wg_res, wu_res, wd_res, x_scr, xb_scr, sems):
    del cur_ref, nxt_ref
    b, j = pl.program_id(0), pl.program_id(1)
    n_used = nused_ref[0]
    rows_per_step = ROW_BLOCK // EXPERT_HIDDEN_CHUNKS

    def start_gather(jj):
        slot = b % 2
        for r in range(jj * rows_per_step, (jj + 1) * rows_per_step):
            pltpu.make_async_copy(h_hbm.at[pl.ds(tok_ref[0, r], 1), :],
                                  x_scr.at[slot, pl.ds(r, 1), :], sems.at[slot]).start()

    def wait_gather(slot):
        _wait_row_gather(h_hbm, x_scr.at[slot], sems.at[slot])

    @pl.when((b >= 1) & (b <= n_used) & (j == 0))
    def _():
        slot = (b - 1) % 2
        wait_gather(slot)
        xb_scr[...] = x_scr[slot].astype(BF16)

    for jj in range(EXPERT_HIDDEN_CHUNKS):
        @pl.when((b == 0) & (j == jj))
        def _(jj=jj):
            start_gather(jj)

        @pl.when((b >= 1) & (b <= n_used) & (j == jj))
        def _(jj=jj):
            xb = xb_scr[...]
            g = jnp.dot(xb, wg_res[jj], preferred_element_type=F32)
            u = jnp.dot(xb, wu_res[jj], preferred_element_type=F32)
            a = ((g * jax.nn.sigmoid(g)) * u).astype(BF16)
            start_gather(jj)
            part = jnp.dot(a, wd_res[jj], preferred_element_type=F32)
            if jj == 0:
                y_ref[...] = part
            else:
                y_ref[...] += part

    @pl.when((b == n_used + 1) & (j == 0))
    def _():
        wait_gather(n_used % 2)

    @pl.when((b > n_used) & (b <= pl.num_programs(0) - 2) & (j == 0))
    def _():
        y_ref[...] = jnp.zeros_like(y_ref)

    @pl.when(last_ref[b] == 1)
    def _():
        wg_res[j] = wg_ref[...].astype(BF16)
        wu_res[j] = wu_ref[...].astype(BF16)
        wd_res[j] = wd_ref[...].astype(BF16)


EXPERT_HIDDEN_CHUNKS = 2


def _expert_mlp(cur_e, nxt_e, last, n_used, slot_tok3, h2, wg, wu, wd, layer):
    nb = slot_tok3.shape[0]
    d = h2.shape[1]
    f = wg.shape[3]
    nj = EXPERT_HIDDEN_CHUNKS
    fc = f // nj

    def w_index(chunk_axis):
        def index(b, j, cur, nxt, lst, nu):
            e = jnp.where(lst[b] == 1, nxt[b], cur[b])
            jj = jnp.where(lst[b] == 1, j, nj - 1)
            return (layer, e, 0, jj) if chunk_axis == 2 else (layer, e, jj, 0)
        return index

    grid_spec = pltpu.PrefetchScalarGridSpec(
        num_scalar_prefetch=4,
        grid=(nb + 2, nj),
        in_specs=[
            pl.BlockSpec((None, 1, ROW_BLOCK), lambda b, j, *_: (jnp.minimum(b, nb - 1), 0, 0),
                         memory_space=pltpu.SMEM),
            pl.BlockSpec(memory_space=pl.ANY),
            pl.BlockSpec((None, None, d, fc), w_index(2)),
            pl.BlockSpec((None, None, d, fc), w_index(2)),
            pl.BlockSpec((None, None, fc, d), w_index(1)),
        ],
        out_specs=pl.BlockSpec((ROW_BLOCK, d), lambda b, j, *_: (jnp.clip(b - 1, 0, nb - 1), 0)),
        scratch_shapes=[
            pltpu.VMEM((nj, d, fc), BF16),
            pltpu.VMEM((nj, d, fc), BF16),
            pltpu.VMEM((nj, fc, d), BF16),
            pltpu.VMEM((2, ROW_BLOCK, d), F32),
            pltpu.VMEM((ROW_BLOCK, d), BF16),
            pltpu.SemaphoreType.DMA((2,)),
        ],
    )
    return pl.pallas_call(
        _expert_kernel,
        grid_spec=grid_spec,
        out_shape=jax.ShapeDtypeStruct((nb * ROW_BLOCK, d), F32),
        compiler_params=_cparams(("arbitrary", "arbitrary"), EXPERT_VMEM_LIMIT),
        name="expert_mlp",
    )(cur_e, nxt_e, last, n_used, slot_tok3, h2, wg, wu, wd)


def _combine_kernel(dest_ref, y_hbm, rw_ref, x_ref, mod_ref, lnf_ref, o_ref, y_scr, sems, *, final):
    tm = x_ref.shape[0]
    i = pl.program_id(0)
    n_tiles = pl.num_programs(0) - 1

    def start_gather():
        slot = i % 2
        for r in range(2 * tm):
            pltpu.make_async_copy(y_hbm.at[pl.ds(dest_ref[0, r], 1), :],
                                  y_scr.at[slot, pl.ds(r, 1), :], sems.at[slot]).start()

    def combine(prefetch):
        slot = (i - 1) % 2
        _wait_row_gather(y_hbm, y_scr.at[slot], sems.at[slot])
        rw = rw_ref[...]
        y = rw[:, 0:1] * y_scr[slot, 0:tm, :] + rw[:, 1:2] * y_scr[slot, tm:2 * tm, :]
        if prefetch:
            start_gather()
        x_new = x_ref[...] + mod_ref[5:6, :] * y
        if final:
            x_new = x_new * lax.rsqrt(jnp.mean(x_new * x_new, axis=-1, keepdims=True) + EPS) * lnf_ref[...]
        o_ref[...] = x_new

    pl.when(i == 0)(start_gather)
    pl.when((i >= 1) & (i < n_tiles))(functools.partial(combine, True))
    pl.when(i == n_tiles)(functools.partial(combine, False))


def _combine(dest3, yg, rw, x2, mod, ln_f, seq, final):
    t, d = x2.shape
    tm = dest3.shape[2] // 2
    nt = t // tm
    prev = lambda m: jnp.maximum(m - 1, 0)
    return pl.pallas_call(
        functools.partial(_combine_kernel, final=final),
        grid=(nt + 1,),
        in_specs=[
            pl.BlockSpec((None, 1, 2 * tm), lambda m: (jnp.minimum(m, nt - 1), 0, 0), memory_space=pltpu.SMEM),
            pl.BlockSpec(memory_space=pl.ANY),
            pl.BlockSpec((tm, LANES), lambda m: (prev(m), 0)),
            pl.BlockSpec((tm, d), lambda m: (prev(m), 0)),
            pl.BlockSpec((None, 6, d), lambda m: ((prev(m) * tm) // seq, 0, 0)),
            pl.BlockSpec((1, d), lambda m: (0, 0)),
        ],
        out_specs=pl.BlockSpec((tm, d), lambda m: (prev(m), 0)),
        out_shape=jax.ShapeDtypeStruct((t, d), F32),
        scratch_shapes=[pltpu.VMEM((2, 2 * tm, d), F32), pltpu.SemaphoreType.DMA((2,))],
        compiler_params=_cparams(("arbitrary",)),
        name="moe_combine",
    )(dest3, yg, rw, x2, mod, ln_f)


def _routing_tables(ri, counts, t):
    n_blocks = (t * TOP_K) // ROW_BLOCK + N_EXPERTS
    cnt = counts[0, :N_EXPERTS]
    padded = (cnt + ROW_BLOCK - 1) // ROW_BLOCK * ROW_BLOCK
    ends = jnp.cumsum(padded)
    starts = ends - padded
    experts, ranks = ri[:, 0:TOP_K], ri[:, TOP_K:2 * TOP_K]
    dest = starts[experts] + ranks
    tokens = jnp.broadcast_to(jnp.arange(t, dtype=jnp.int32)[:, None], (t, TOP_K))
    slot_tok = jnp.zeros((n_blocks * ROW_BLOCK,), jnp.int32).at[dest.reshape(-1)].set(tokens.reshape(-1))
    n_used = (ends[-1] // ROW_BLOCK).astype(jnp.int32)
    blocks = jnp.minimum(jnp.arange(n_blocks + 2, dtype=jnp.int32), n_used - 1) * ROW_BLOCK
    block_exp = jnp.sum((blocks[:, None] >= ends[None, :]).astype(jnp.int32), axis=1)
    block_exp = jnp.minimum(block_exp, N_EXPERTS - 1)
    cur_e = jnp.concatenate([block_exp[:1], block_exp[:-1]])
    nxt_e = block_exp
    last = (cur_e != nxt_e).astype(jnp.int32).at[0].set(1)
    return (dest.astype(jnp.int32), slot_tok.reshape(n_blocks, 1, ROW_BLOCK),
            cur_e, nxt_e, last, n_used.reshape(1))


def kernel(x, c, ln1, ln2, w_ada, b_ada, w_in, w_gk, b_gk, g_onorm, w_out,
           w_r1, b_r1, w_r2, b_r2, w_e_gate, w_e_up, w_e_down, ln_f):
    b, seq, d = x.shape
    depth = w_ada.shape[0]
    t = b * seq
    mod_all = _adaln_mod(c, w_ada, b_ada).reshape(depth, b, 6, d)
    tables = _rope_tables(seq)
    x2 = x.reshape(t, d)
    tm_c = 256
    for l in range(depth):
        mod = mod_all[l]
        w_main = w_in[l][:, :PROJ_MAIN].astype(BF16)
        w_ga = jnp.pad(w_in[l][:, PROJ_MAIN:], ((0, 0), (0, LANES - GATE_RANK))).astype(BF16)
        w_gk_p = jnp.pad(w_gk[l], ((0, LANES - GATE_RANK), (0, 0)))
        proj, z = _in_projection(x2, ln1[l][None, :], mod, w_main, w_ga, w_gk_p, b_gk[l][None, :], seq)
        proj3 = proj.reshape(b, seq, PROJ_MAIN)
        oa = _moba_attention(proj3, tables)
        ob = _gla_mixer(proj3, z.reshape(b, seq, G_KEY_WIDTH), g_onorm[l][None, :])
        w_o = w_out[l].astype(BF16)
        w_r = jnp.pad(jnp.concatenate([w_r2[l], w_r1[l]], axis=1), ((0, 0), (0, LANES - N_EXPERTS - N_GROUPS)))
        w_r_hi = w_r.astype(BF16)
        w_r = jnp.concatenate([w_r_hi, (w_r - w_r_hi.astype(F32)).astype(BF16)], axis=1)
        b_r = jnp.pad(jnp.concatenate([b_r2[l], b_r1[l]]), (0, LANES - N_EXPERTS - N_GROUPS))[None, :]
        x2, h2, ri, rw, counts = _outproj_router(
            oa.reshape(t, A_WIDTH), ob.reshape(t, G_VAL_WIDTH), x2, w_o[:A_WIDTH], w_o[A_WIDTH:],
            mod, ln2[l][None, :], w_r, b_r, seq)
        dest, slot_tok3, cur_e, nxt_e, last, n_used = _routing_tables(ri, counts, t)
        yg = _expert_mlp(cur_e, nxt_e, last, n_used, slot_tok3, h2, w_e_gate, w_e_up, w_e_down, l)
        dest3 = dest.reshape(t // tm_c, tm_c, TOP_K).transpose(0, 2, 1).reshape(t // tm_c, 1, TOP_K * tm_c)
        x2 = _combine(dest3, yg, rw, x2, mod, ln_f[None, :], seq, final=(l == depth - 1))
    return x2.reshape(b, seq, d)
```

```python
import functools

import jax
import jax.numpy as jnp
import numpy as np
from jax import lax
from jax.experimental import pallas as pl
from jax.experimental.pallas import tpu as pltpu

F32 = jnp.float32
BF16 = jnp.bfloat16
HIGHEST = lax.Precision.HIGHEST

HEAD_DIM = 128
A_HEADS = 8
A_WIDTH = A_HEADS * HEAD_DIM
MOBA_BLOCK = 256
MOBA_TOPK = 3
ROPE_THETA = 500000.0
ROPE_DIM = HEAD_DIM // 4
ATTN_SCALE = HEAD_DIM ** -0.5
G_HEADS = 4
G_VAL_DIM = 256
G_KEY_DIM = 128
G_KEY_WIDTH = G_HEADS * G_KEY_DIM
G_VAL_WIDTH = G_HEADS * G_VAL_DIM
GATE_RANK = 16
GATE_TAU = 16.0
GLA_CHUNK = 64
N_GROUPS = 4
EXPERTS_PER_GROUP = 8
N_EXPERTS = N_GROUPS * EXPERTS_PER_GROUP
TOP_K = 2
EPS = 1e-6

LANES = 128
PROJ_MAIN = 3 * A_WIDTH + 2 * G_KEY_WIDTH + 2 * G_VAL_WIDTH
ROW_BLOCK = 256
VMEM_LIMIT = 52 * 1024 * 1024
EXPERT_VMEM_LIMIT = 58 * 1024 * 1024


def _cparams(sem, vmem_limit=VMEM_LIMIT):
    return pltpu.CompilerParams(dimension_semantics=sem, vmem_limit_bytes=vmem_limit)


def _pack_bf16_pairs(x):
    c = x.shape[1] // 2
    lo = pltpu.bitcast(x[:, :c].astype(BF16).astype(F32), jnp.uint32)
    hi = pltpu.bitcast(x[:, c:].astype(BF16).astype(F32), jnp.uint32)
    return (lo >> 16) | hi


def _unpack_bf16_pairs(w):
    lo = pltpu.bitcast(w << 16, F32)
    hi = pltpu.bitcast(w & jnp.uint32(0xFFFF0000), F32)
    return lo, hi


def _mod_kernel(c_ref, w_ref, b_ref, o_ref):
    c = c_ref[...]
    ca = c * jax.nn.sigmoid(c)
    o_ref[...] = jnp.dot(ca, w_ref[...], precision=HIGHEST, preferred_element_type=F32) + b_ref[...]


def _adaln_mod(c, w_ada, b_ada):
    depth, d, n6 = w_ada.shape
    b = c.shape[0]
    tn = 1024
    return pl.pallas_call(
        _mod_kernel,
        grid=(depth, n6 // tn),
        in_specs=[
            pl.BlockSpec((b, d), lambda l, n: (0, 0)),
            pl.BlockSpec((None, d, tn), lambda l, n: (l, 0, n)),
            pl.BlockSpec((None, 1, tn), lambda l, n: (l, 0, n)),
        ],
        out_specs=pl.BlockSpec((None, b, tn), lambda l, n: (l, 0, n)),
        out_shape=jax.ShapeDtypeStruct((depth, b, n6), F32),
        compiler_params=_cparams(("arbitrary", "arbitrary")),
        name="adaln_mod",
    )(c, w_ada, b_ada.reshape(depth, 1, n6))


def _modulated_norm(x, g, shift, scale):
    y = x * lax.rsqrt(jnp.mean(x * x, axis=-1, keepdims=True) + EPS) * g
    return y * (1.0 + scale) + shift


def _inproj_kernel(x_ref, ln_ref, mod_ref, w_ref, wga_ref, wgk_ref, bgk_ref, proj_ref, z_ref, h_scr):
    @pl.when(pl.program_id(1) == 0)
    def _():
        h = _modulated_norm(x_ref[...], ln_ref[...], mod_ref[0:1, :], mod_ref[1:2, :])
        hb = h.astype(BF16)
        h_scr[...] = hb
        ga = jnp.dot(hb, wga_ref[...], preferred_element_type=F32)
        z_ref[...] = jnp.dot(ga, wgk_ref[...], precision=HIGHEST, preferred_element_type=F32) + bgk_ref[...]

    proj_ref[...] = jnp.dot(h_scr[...], w_ref[...], preferred_element_type=F32).astype(proj_ref.dtype)


def _in_projection(x2, ln, mod, w_main, w_ga, w_gk, b_gk, seq):
    t, d = x2.shape
    tm, tn = 1024, 1024
    return pl.pallas_call(
        _inproj_kernel,
        grid=(t // tm, PROJ_MAIN // tn),
        in_specs=[
            pl.BlockSpec((tm, d), lambda m, n: (m, 0)),
            pl.BlockSpec((1, d), lambda m, n: (0, 0)),
            pl.BlockSpec((None, 6, d), lambda m, n: ((m * tm) // seq, 0, 0)),
            pl.BlockSpec((d, tn), lambda m, n: (0, n)),
            pl.BlockSpec((d, LANES), lambda m, n: (0, 0)),
            pl.BlockSpec((LANES, G_KEY_WIDTH), lambda m, n: (0, 0)),
            pl.BlockSpec((1, G_KEY_WIDTH), lambda m, n: (0, 0)),
        ],
        out_specs=[
            pl.BlockSpec((tm, tn), lambda m, n: (m, n)),
            pl.BlockSpec((tm, G_KEY_WIDTH), lambda m, n: (m, 0)),
        ],
        out_shape=[
            jax.ShapeDtypeStruct((t, PROJ_MAIN), BF16),
            jax.ShapeDtypeStruct((t, G_KEY_WIDTH), F32),
        ],
        scratch_shapes=[pltpu.VMEM((tm, d), BF16)],
        compiler_params=_cparams(("arbitrary", "arbitrary")),
        name="in_projection",
    )(x2, ln, mod, w_main, w_ga, w_gk, b_gk)


def _rope(x, cos, sin_lo, sin_hi):
    half = ROPE_DIM // 2
    return x * cos + pltpu.roll(x, half, 1) * sin_hi + pltpu.roll(x, LANES - half, 1) * sin_lo


def _moba_kernel(q_ref, k_ref, v_ref, cos_ref, slo_ref, shi_ref, o_ref, s_scr):
    seq = q_ref.shape[0]
    nblk = seq // MOBA_BLOCK
    cos, slo, shi = cos_ref[...], slo_ref[...], shi_ref[...]
    q = _rope(q_ref[...].astype(F32), cos, slo, shi) * ATTN_SCALE
    k = _rope(k_ref[...].astype(F32), cos, slo, shi)
    kmean = jnp.mean(k.reshape(nblk, MOBA_BLOCK, HEAD_DIM), axis=1)
    gate = lax.dot_general(kmean, q, (((1,), (1,)), ((), ())), precision=HIGHEST,
                           preferred_element_type=F32)
    blk_of_q = lax.broadcasted_iota(jnp.int32, (nblk, seq), 1) // MOBA_BLOCK
    row = lax.broadcasted_iota(jnp.int32, (nblk, seq), 0)
    past = row < blk_of_q
    better = jnp.zeros((nblk, seq), F32)
    for m in range(nblk):
        gm = gate[m:m + 1, :]
        past_m = blk_of_q[m:m + 1, :] > m
        beats = (gm > gate) | ((gm == gate) & (row > m))
        better = better + jnp.where(beats & past_m, 1.0, 0.0)
    sel = past & (better < float(MOBA_TOPK))

    qb = q.astype(BF16)
    kb = k.astype(BF16)
    vt = v_ref[...].astype(F32).T.astype(BF16)
    key_i = lax.broadcasted_iota(jnp.int32, (MOBA_BLOCK, MOBA_BLOCK), 0)
    qry_i = lax.broadcasted_iota(jnp.int32, (MOBA_BLOCK, MOBA_BLOCK), 1)
    causal = key_i <= qry_i
    for i in range(nblk):
        qs = slice(i * MOBA_BLOCK, (i + 1) * MOBA_BLOCK)
        qi = qb[qs, :]
        mx = None
        for n in range(i + 1):
            ks = slice(n * MOBA_BLOCK, (n + 1) * MOBA_BLOCK)
            s = lax.dot_general(kb[ks, :], qi, (((1,), (1,)), ((), ())),
                                preferred_element_type=F32)
            keep = causal if n == i else sel[n:n + 1, qs]
            s = jnp.where(keep, s, -jnp.inf)
            s_scr[n] = s
            bm = jnp.max(s, axis=0, keepdims=True)
            mx = bm if mx is None else jnp.maximum(mx, bm)
        den = jnp.zeros((1, MOBA_BLOCK), F32)
        acc = jnp.zeros((HEAD_DIM, MOBA_BLOCK), F32)
        for n in range(i + 1):
            ks = slice(n * MOBA_BLOCK, (n + 1) * MOBA_BLOCK)
            p = jnp.exp(s_scr[n] - mx)
            den = den + jnp.sum(p, axis=0, keepdims=True)
            acc = acc + jnp.dot(vt[:, ks], p.astype(BF16), preferred_element_type=F32)
        o_ref[qs, :] = (acc / den).T.astype(o_ref.dtype)


def _rope_tables(seq):
    half = ROPE_DIM // 2
    inv = jnp.power(ROPE_THETA, -jnp.arange(half, dtype=F32) * (2.0 / ROPE_DIM))
    ang = jnp.arange(seq, dtype=F32)[:, None] * inv[None, :]
    cos, sin = jnp.cos(ang), jnp.sin(ang)
    ones = jnp.ones((seq, HEAD_DIM - ROPE_DIM), F32)
    zeros = jnp.zeros((seq, HEAD_DIM - half), F32)
    cos_t = jnp.concatenate([cos, cos, ones], axis=1)
    sin_lo = jnp.concatenate([-sin, zeros], axis=1)
    sin_hi = jnp.concatenate([jnp.zeros((seq, half), F32), sin, zeros[:, half:]], axis=1)
    return cos_t, sin_lo, sin_hi


def _moba_attention(proj3, tables):
    b, seq, _ = proj3.shape
    head = lambda off: pl.BlockSpec((None, seq, HEAD_DIM), lambda bi, h: (bi, 0, off + h))
    tab = pl.BlockSpec((seq, HEAD_DIM), lambda bi, h: (0, 0))
    return pl.pallas_call(
        _moba_kernel,
        grid=(b, A_HEADS),
        in_specs=[head(0), head(A_HEADS), head(2 * A_HEADS), tab, tab, tab],
        out_specs=pl.BlockSpec((None, seq, HEAD_DIM), lambda bi, h: (bi, 0, h)),
        out_shape=jax.ShapeDtypeStruct((b, seq, A_WIDTH), BF16),
        scratch_shapes=[pltpu.VMEM((seq // MOBA_BLOCK, MOBA_BLOCK, MOBA_BLOCK), F32)],
        compiler_params=_cparams(("arbitrary", "arbitrary")),
        name="moba_attention",
    )(proj3, proj3, proj3, *tables)


def _log_sigmoid(z):
    return jnp.minimum(z, 0.0) - jnp.log1p(jnp.exp(-jnp.abs(z)))


def _chunk_cumsum(x):
    pos = lax.broadcasted_iota(jnp.int32, x.shape, 0) % GLA_CHUNK
    shift = 1
    while shift < GLA_CHUNK:
        x = x + jnp.where(pos >= shift, pltpu.roll(x, shift, 0), 0.0)
        shift *= 2
    return x


def _gla_kernel(q_ref, k_ref, v_ref, og_ref, z_ref, gn_ref, o_ref,
                qd_scr, ki_scr, ke_scr, dec_scr, kv_scr, st_scr, oi_scr):
    seq = q_ref.shape[0]
    nc = seq // GLA_CHUNK
    log_a = _log_sigmoid(z_ref[...]) / GATE_TAU
    b = _chunk_cumsum(log_a)
    b3 = b.reshape(nc, GLA_CHUNK, G_KEY_DIM)
    b_last = b3[:, GLA_CHUNK - 1:GLA_CHUNK, :]
    k = k_ref[...].astype(F32)
    qd_scr[...] = ((q_ref[...].astype(F32) * (G_KEY_DIM ** -0.5)) * jnp.exp(b)).astype(BF16)
    ki_scr[...] = (k * jnp.exp(-b)).astype(BF16)
    ke3 = k.reshape(nc, GLA_CHUNK, G_KEY_DIM) * jnp.exp(b_last - b3)
    ke_scr[...] = ke3.reshape(seq, G_KEY_DIM).astype(BF16)
    dec_scr[...] = jnp.exp(b_last)
    t_i = lax.broadcasted_iota(jnp.int32, (GLA_CHUNK, GLA_CHUNK), 0)
    s_i = lax.broadcasted_iota(jnp.int32, (GLA_CHUNK, GLA_CHUNK), 1)
    causal = s_i <= t_i
    gn = gn_ref[...]

    def chunk_rows(n):
        return pl.ds(pl.multiple_of(n * GLA_CHUNK, GLA_CHUNK), GLA_CHUNK)

    def intra(n, carry):
        rows = chunk_rows(n)
        vb = v_ref[rows, :].astype(BF16)
        att = lax.dot_general(qd_scr[rows, :], ki_scr[rows, :], (((1,), (1,)), ((), ())),
                              preferred_element_type=F32)
        att = jnp.where(causal, att, 0.0).astype(BF16)
        oi_scr[rows, :] = jnp.dot(att, vb, preferred_element_type=F32)
        kv_scr[n] = lax.dot_general(vb, ke_scr[rows, :], (((0,), (0,)), ((), ())),
                                    preferred_element_type=F32)
        return carry

    lax.fori_loop(0, nc, intra, 0, unroll=GLA_UNROLL)

    def scan(n, state):
        st_scr[n] = state.astype(BF16)
        return state * dec_scr[n] + kv_scr[n]

    lax.fori_loop(0, nc, scan, jnp.zeros((G_VAL_DIM, G_KEY_DIM), F32), unroll=GLA_UNROLL)

    def inter(n, carry):
        rows = chunk_rows(n)
        o = oi_scr[rows, :] + lax.dot_general(qd_scr[rows, :], st_scr[n], (((1,), (1,)), ((), ())),
                                              preferred_element_type=F32)
        o = o * lax.rsqrt(jnp.mean(o * o, axis=-1, keepdims=True) + EPS) * gn
        og = og_ref[rows, :].astype(F32)
        o_ref[rows, :] = (o * (og * jax.nn.sigmoid(og))).astype(o_ref.dtype)
        return carry

    lax.fori_loop(0, nc, inter, 0, unroll=GLA_UNROLL)


GLA_UNROLL = 4


def _gla_mixer(proj3, z3, g_onorm):
    b, seq, _ = proj3.shape
    kq = 3 * A_WIDTH // G_KEY_DIM
    kv = (3 * A_WIDTH + 2 * G_KEY_WIDTH) // G_VAL_DIM
    key = lambda off: pl.BlockSpec((None, seq, G_KEY_DIM), lambda bi, h: (bi, 0, off + h))
    val = lambda off: pl.BlockSpec((None, seq, G_VAL_DIM), lambda bi, h: (bi, 0, off + h))
    return pl.pallas_call(
        _gla_kernel,
        grid=(b, G_HEADS),
        in_specs=[
            key(kq), key(kq + G_HEADS), val(kv), val(kv + G_HEADS),
            pl.BlockSpec((None, seq, G_KEY_DIM), lambda bi, h: (bi, 0, h)),
            pl.BlockSpec((1, G_VAL_DIM), lambda bi, h: (0, 0)),
        ],
        out_specs=pl.BlockSpec((None, seq, G_VAL_DIM), lambda bi, h: (bi, 0, h)),
        out_shape=jax.ShapeDtypeStruct((b, seq, G_VAL_WIDTH), BF16),
        scratch_shapes=[
            pltpu.VMEM((seq, G_KEY_DIM), BF16),
            pltpu.VMEM((seq, G_KEY_DIM), BF16),
            pltpu.VMEM((seq, G_KEY_DIM), BF16),
            pltpu.VMEM((seq // GLA_CHUNK, 1, G_KEY_DIM), F32),
            pltpu.VMEM((seq // GLA_CHUNK, G_VAL_DIM, G_KEY_DIM), F32),
            pltpu.VMEM((seq // GLA_CHUNK, G_VAL_DIM, G_KEY_DIM), BF16),
            pltpu.VMEM((seq, G_VAL_DIM), F32),
        ],
        compiler_params=_cparams(("arbitrary", "arbitrary")),
        name="gla_mixer",
    )(proj3, proj3, proj3, proj3, z3, g_onorm)


def _first_lane(mask, lane):
    return jnp.min(jnp.where(mask, lane, LANES), axis=-1, keepdims=True)


def _split_bf16(x):
    hi = x.astype(BF16)
    return hi, (x - hi.astype(F32)).astype(BF16)


def _outproj_router_kernel(oa_ref, ob_ref, x_ref, wa_ref, wb_ref, mod_ref, ln_ref, wr_ref, br_ref,
                           xo_ref, h_ref, ri_ref, rw_ref, cnt_ref, run_scr):
    tm = x_ref.shape[0]

    @pl.when(pl.program_id(0) == 0)
    def _():
        run_scr[...] = jnp.zeros_like(run_scr)

    mix = jnp.dot(oa_ref[...], wa_ref[...], preferred_element_type=F32)
    mix = mix + jnp.dot(ob_ref[...], wb_ref[...], preferred_element_type=F32)
    x_new = x_ref[...] + mod_ref[2:3, :] * mix
    xo_ref[...] = x_new
    h = _modulated_norm(x_new, ln_ref[...], mod_ref[3:4, :], mod_ref[4:5, :])
    h_ref[...] = _pack_bf16_pairs(h)

    hh = jnp.concatenate(_split_bf16(h), axis=0)
    r = jnp.dot(hh, wr_ref[...], preferred_element_type=F32)
    logits = (r[:tm, :LANES] + r[tm:, :LANES]) + (r[:tm, LANES:] + r[tm:, LANES:]) + br_ref[...]
    lane = lax.broadcasted_iota(jnp.int32, (tm, LANES), 1)
    is_grp = (lane >= N_EXPERTS) & (lane < N_EXPERTS + N_GROUPS)
    l1 = jnp.where(is_grp, logits, -jnp.inf)
    m1 = jnp.max(l1, axis=-1, keepdims=True)
    grp = _first_lane(l1 == m1, lane) - N_EXPERTS
    p_grp = 1.0 / jnp.sum(jnp.exp(l1 - m1), axis=-1, keepdims=True)
    in_grp = (lane < N_EXPERTS) & ((lane // EXPERTS_PER_GROUP) == grp)
    l2 = jnp.where(in_grp, logits, -jnp.inf)
    va = jnp.max(l2, axis=-1, keepdims=True)
    ia = _first_lane(l2 == va, lane)
    l2b = jnp.where(lane == ia, -jnp.inf, l2)
    vb = jnp.max(l2b, axis=-1, keepdims=True)
    ib = _first_lane(l2b == vb, lane)
    eb = jnp.exp(vb - va)
    wa = p_grp * (1.0 / (1.0 + eb))
    wb = p_grp * (eb / (1.0 + eb))

    oh_a = jnp.where(lane == ia, 1.0, 0.0)
    oh_b = jnp.where(lane == ib, 1.0, 0.0)
    cnt = oh_a + oh_b
    r_i = lax.broadcasted_iota(jnp.int32, (tm, tm), 0)
    c_i = lax.broadcasted_iota(jnp.int32, (tm, tm), 1)
    strict_lower = jnp.where(c_i < r_i, 1.0, 0.0).astype(BF16)
    before = jnp.dot(strict_lower, cnt.astype(BF16), preferred_element_type=F32) + run_scr[...]
    rank_a = jnp.sum(oh_a * before, axis=-1, keepdims=True)
    rank_b = jnp.sum(oh_b * before, axis=-1, keepdims=True)
    run_new = run_scr[...] + jnp.sum(cnt, axis=0, keepdims=True)
    run_scr[...] = run_new
    cnt_ref[...] = jnp.broadcast_to(run_new, cnt_ref.shape).astype(jnp.int32)

    ri = jnp.where(lane == 0, ia, 0) + jnp.where(lane == 1, ib, 0)
    ri = ri + jnp.where(lane == 2, rank_a.astype(jnp.int32), 0) + jnp.where(lane == 3, rank_b.astype(jnp.int32), 0)
    ri_ref[...] = ri
    rw_ref[...] = jnp.where(lane == 0, wa, 0.0) + jnp.where(lane == 1, wb, 0.0)


def _outproj_router(oa2, ob2, x2, w_oa, w_ob, mod, ln, w_r, b_r, seq):
    t, d = x2.shape
    tm = 256
    row = lambda w: pl.BlockSpec((tm, w), lambda m: (m, 0))
    full = lambda a, c: pl.BlockSpec((a, c), lambda m: (0, 0))
    return pl.pallas_call(
        _outproj_router_kernel,
        grid=(t // tm,),
        in_specs=[
            row(A_WIDTH), row(G_VAL_WIDTH), row(d),
            full(A_WIDTH, d), full(G_VAL_WIDTH, d),
            pl.BlockSpec((None, 6, d), lambda m: ((m * tm) // seq, 0, 0)),
            full(1, d), full(d, 2 * LANES), full(1, LANES),
        ],
        out_specs=[row(d), row(d // 2), row(LANES), row(LANES), full(8, LANES)],
        out_shape=[
            jax.ShapeDtypeStruct((t, d), F32),
            jax.ShapeDtypeStruct((t, d // 2), jnp.uint32),
            jax.ShapeDtypeStruct((t, LANES), jnp.int32),
            jax.ShapeDtypeStruct((t, LANES), F32),
            jax.ShapeDtypeStruct((8, LANES), jnp.int32),
        ],
        scratch_shapes=[pltpu.VMEM((1, LANES), F32)],
        compiler_params=_cparams(("arbitrary",)),
        name="outproj_router",
    )(oa2, ob2, x2, w_oa, w_ob, mod, ln, w_r, b_r)


def _wait_row_gather(src_hbm, dst_vmem, sem):
    pltpu.make_async_copy(src_hbm.at[pl.ds(0, dst_vmem.shape[0]), :], dst_vmem, sem).wait()


def _expert_kernel(cur_ref, nxt_ref, last_ref, nused_ref, tok_ref, h_hbm, wg_ref, wu_ref, wd_ref,
                   y_ref, wg_res, wu_res, wd_res, x_scr, xb_scr, acc_scr, sems):
    del cur_ref, nxt_ref
    b, j = pl.program_id(0), pl.program_id(1)
    n_used = nused_ref[0]
    rows_per_step = ROW_BLOCK // EXPERT_HIDDEN_CHUNKS

    def start_gather(jj):
        slot = b % 2
        for r in range(jj * rows_per_step, (jj + 1) * rows_per_step):
            pltpu.make_async_copy(h_hbm.at[pl.ds(tok_ref[0, r], 1), :],
                                  x_scr.at[slot, pl.ds(r, 1), :], sems.at[slot]).start()

    def wait_gather(slot):
        _wait_row_gather(h_hbm, x_scr.at[slot], sems.at[slot])

    @pl.when((b >= 1) & (b <= n_used) & (j == 0))
    def _():
        slot = (b - 1) % 2
        wait_gather(slot)
        lo, hi = _unpack_bf16_pairs(x_scr[slot])
        half = lo.shape[1]
        xb_scr[:, :half] = lo.astype(BF16)
        xb_scr[:, half:] = hi.astype(BF16)

    for jj in range(EXPERT_HIDDEN_CHUNKS):
        @pl.when((b == 0) & (j == jj))
        def _(jj=jj):
            start_gather(jj)

        @pl.when((b >= 1) & (b <= n_used) & (j == jj))
        def _(jj=jj):
            xb = xb_scr[...]
            g = jnp.dot(xb, wg_res[jj], preferred_element_type=F32)
            u = jnp.dot(xb, wu_res[jj], preferred_element_type=F32)
            a = ((g * jax.nn.sigmoid(g)) * u).astype(BF16)
            start_gather(jj)
            part = jnp.dot(a, wd_res[jj], preferred_element_type=F32)
            if jj == 0:
                acc_scr[...] = part
            elif jj < EXPERT_HIDDEN_CHUNKS - 1:
                acc_scr[...] += part
            else:
                y_ref[...] = _pack_bf16_pairs(acc_scr[...] + part)

    @pl.when((b == n_used + 1) & (j == 0))
    def _():
        wait_gather(n_used % 2)

    @pl.when((b > n_used) & (b <= pl.num_programs(0) - 2) & (j == 0))
    def _():
        y_ref[...] = jnp.zeros_like(y_ref)

    @pl.when(last_ref[b] == 1)
    def _():
        wg_res[j] = wg_ref[...].astype(BF16)
        wu_res[j] = wu_ref[...].astype(BF16)
        wd_res[j] = wd_ref[...].astype(BF16)


EXPERT_HIDDEN_CHUNKS = 2


def _expert_mlp(cur_e, nxt_e, last, n_used, slot_tok3, h2, wg, wu, wd, layer):
    nb = slot_tok3.shape[0]
    d, f = wg.shape[2], wg.shape[3]
    dp = h2.shape[1]
    nj = EXPERT_HIDDEN_CHUNKS
    fc = f // nj

    def w_index(chunk_axis):
        def index(b, j, cur, nxt, lst, nu):
            e = jnp.where(lst[b] == 1, nxt[b], cur[b])
            jj = jnp.where(lst[b] == 1, j, nj - 1)
            return (layer, e, 0, jj) if chunk_axis == 2 else (layer, e, jj, 0)
        return index

    grid_spec = pltpu.PrefetchScalarGridSpec(
        num_scalar_prefetch=4,
        grid=(nb + 2, nj),
        in_specs=[
            pl.BlockSpec((None, 1, ROW_BLOCK), lambda b, j, *_: (jnp.minimum(b, nb - 1), 0, 0),
                         memory_space=pltpu.SMEM),
            pl.BlockSpec(memory_space=pl.ANY),
            pl.BlockSpec((None, None, d, fc), w_index(2)),
            pl.BlockSpec((None, None, d, fc), w_index(2)),
            pl.BlockSpec((None, None, fc, d), w_index(1)),
        ],
        out_specs=pl.BlockSpec((ROW_BLOCK, dp), lambda b, j, *_: (jnp.clip(b - 1, 0, nb - 1), 0)),
        scratch_shapes=[
            pltpu.VMEM((nj, d, fc), BF16),
            pltpu.VMEM((nj, d, fc), BF16),
            pltpu.VMEM((nj, fc, d), BF16),
            pltpu.VMEM((2, ROW_BLOCK, dp), jnp.uint32),
            pltpu.VMEM((ROW_BLOCK, d), BF16),
            pltpu.VMEM((ROW_BLOCK, d), F32),
            pltpu.SemaphoreType.DMA((2,)),
        ],
    )
    return pl.pallas_call(
        _expert_kernel,
        grid_spec=grid_spec,
        out_shape=jax.ShapeDtypeStruct((nb * ROW_BLOCK, dp), jnp.uint32),
        compiler_params=_cparams(("arbitrary", "arbitrary"), EXPERT_VMEM_LIMIT),
        name="expert_mlp",
    )(cur_e, nxt_e, last, n_used, slot_tok3, h2, wg, wu, wd)


def _combine_kernel(dest_ref, y_hbm, rw_ref, x_ref, mod_ref, lnf_ref, o_ref, y_scr, sems, *, final):
    tm = x_ref.shape[0]
    i = pl.program_id(0)
    n_tiles = pl.num_programs(0) - 1

    def start_gather():
        slot = i % 2
        for r in range(2 * tm):
            pltpu.make_async_copy(y_hbm.at[pl.ds(dest_ref[0, r], 1), :],
                                  y_scr.at[slot, pl.ds(r, 1), :], sems.at[slot]).start()

    def combine(prefetch):
        slot = (i - 1) % 2
        _wait_row_gather(y_hbm, y_scr.at[slot], sems.at[slot])
        rw = rw_ref[...]
        a_lo, a_hi = _unpack_bf16_pairs(y_scr[slot, 0:tm, :])
        b_lo, b_hi = _unpack_bf16_pairs(y_scr[slot, tm:2 * tm, :])
        y = jnp.concatenate([rw[:, 0:1] * a_lo + rw[:, 1:2] * b_lo,
                             rw[:, 0:1] * a_hi + rw[:, 1:2] * b_hi], axis=1)
        if prefetch:
            start_gather()
        x_new = x_ref[...] + mod_ref[5:6, :] * y
        if final:
            x_new = x_new * lax.rsqrt(jnp.mean(x_new * x_new, axis=-1, keepdims=True) + EPS) * lnf_ref[...]
        o_ref[...] = x_new

    pl.when(i == 0)(start_gather)
    pl.when((i >= 1) & (i < n_tiles))(functools.partial(combine, True))
    pl.when(i == n_tiles)(functools.partial(combine, False))


def _combine(dest3, yg, rw, x2, mod, ln_f, seq, final):
    t, d = x2.shape
    tm = dest3.shape[2] // 2
    nt = t // tm
    prev = lambda m: jnp.maximum(m - 1, 0)
    return pl.pallas_call(
        functools.partial(_combine_kernel, final=final),
        grid=(nt + 1,),
        in_specs=[
            pl.BlockSpec((None, 1, 2 * tm), lambda m: (jnp.minimum(m, nt - 1), 0, 0), memory_space=pltpu.SMEM),
            pl.BlockSpec(memory_space=pl.ANY),
            pl.BlockSpec((tm, LANES), lambda m: (prev(m), 0)),
            pl.BlockSpec((tm, d), lambda m: (prev(m), 0)),
            pl.BlockSpec((None, 6, d), lambda m: ((prev(m) * tm) // seq, 0, 0)),
            pl.BlockSpec((1, d), lambda m: (0, 0)),
        ],
        out_specs=pl.BlockSpec((tm, d), lambda m: (prev(m), 0)),
        out_shape=jax.ShapeDtypeStruct((t, d), F32),
        scratch_shapes=[pltpu.VMEM((2, 2 * tm, yg.shape[1]), jnp.uint32), pltpu.SemaphoreType.DMA((2,))],
        compiler_params=_cparams(("arbitrary",)),
        name="moe_combine",
    )(dest3, yg, rw, x2, mod, ln_f)


def _routing_tables(ri, counts, t):
    n_blocks = (t * TOP_K) // ROW_BLOCK + N_EXPERTS
    cnt = counts[0, :N_EXPERTS]
    padded = (cnt + ROW_BLOCK - 1) // ROW_BLOCK * ROW_BLOCK
    ends = jnp.cumsum(padded)
    starts = ends - padded
    experts, ranks = ri[:, 0:TOP_K], ri[:, TOP_K:2 * TOP_K]
    dest = starts[experts] + ranks
    tokens = jnp.broadcast_to(jnp.arange(t, dtype=jnp.int32)[:, None], (t, TOP_K))
    slot_tok = jnp.zeros((n_blocks * ROW_BLOCK,), jnp.int32).at[dest.reshape(-1)].set(tokens.reshape(-1))
    n_used = (ends[-1] // ROW_BLOCK).astype(jnp.int32)
    blocks = jnp.minimum(jnp.arange(n_blocks + 2, dtype=jnp.int32), n_used - 1) * ROW_BLOCK
    block_exp = jnp.sum((blocks[:, None] >= ends[None, :]).astype(jnp.int32), axis=1)
    block_exp = jnp.minimum(block_exp, N_EXPERTS - 1)
    cur_e = jnp.concatenate([block_exp[:1], block_exp[:-1]])
    nxt_e = block_exp
    last = (cur_e != nxt_e).astype(jnp.int32).at[0].set(1)
    return (dest.astype(jnp.int32), slot_tok.reshape(n_blocks, 1, ROW_BLOCK),
            cur_e, nxt_e, last, n_used.reshape(1))


def kernel(x, c, ln1, ln2, w_ada, b_ada, w_in, w_gk, b_gk, g_onorm, w_out,
           w_r1, b_r1, w_r2, b_r2, w_e_gate, w_e_up, w_e_down, ln_f):
    b, seq, d = x.shape
    depth = w_ada.shape[0]
    t = b * seq
    mod_all = _adaln_mod(c, w_ada, b_ada).reshape(depth, b, 6, d)
    tables = _rope_tables(seq)
    x2 = x.reshape(t, d)
    tm_c = 256
    for l in range(depth):
        mod = mod_all[l]
        w_main = w_in[l][:, :PROJ_MAIN].astype(BF16)
        w_ga = jnp.pad(w_in[l][:, PROJ_MAIN:], ((0, 0), (0, LANES - GATE_RANK))).astype(BF16)
        w_gk_p = jnp.pad(w_gk[l], ((0, LANES - GATE_RANK), (0, 0)))
        proj, z = _in_projection(x2, ln1[l][None, :], mod, w_main, w_ga, w_gk_p, b_gk[l][None, :], seq)
        proj3 = proj.reshape(b, seq, PROJ_MAIN)
        oa = _moba_attention(proj3, tables)
        ob = _gla_mixer(proj3, z.reshape(b, seq, G_KEY_WIDTH), g_onorm[l][None, :])
        w_o = w_out[l].astype(BF16)
        w_r = jnp.pad(jnp.concatenate([w_r2[l], w_r1[l]], axis=1), ((0, 0), (0, LANES - N_EXPERTS - N_GROUPS)))
        w_r_hi = w_r.astype(BF16)
        w_r = jnp.concatenate([w_r_hi, (w_r - w_r_hi.astype(F32)).astype(BF16)], axis=1)
        b_r = jnp.pad(jnp.concatenate([b_r2[l], b_r1[l]]), (0, LANES - N_EXPERTS - N_GROUPS))[None, :]
        x2, h2, ri, rw, counts = _outproj_router(
            oa.reshape(t, A_WIDTH), ob.reshape(t, G_VAL_WIDTH), x2, w_o[:A_WIDTH], w_o[A_WIDTH:],
            mod, ln2[l][None, :], w_r, b_r, seq)
        dest, slot_tok3, cur_e, nxt_e, last, n_used = _routing_tables(ri, counts, t)
        yg = _expert_mlp(cur_e, nxt_e, last, n_used, slot_tok3, h2, w_e_gate, w_e_up, w_e_down, l)
        dest3 = dest.reshape(t // tm_c, tm_c, TOP_K).transpose(0, 2, 1).reshape(t // tm_c, 1, TOP_K * tm_c)
        x2 = _combine(dest3, yg, rw, x2, mod, ln_f[None, :], seq, final=(l == depth - 1))
    return x2.reshape(b, seq, d)
```

```python
import functools

import jax
import jax.numpy as jnp
import numpy as np
from jax import lax
from jax.experimental import pallas as pl
from jax.experimental.pallas import tpu as pltpu

F32 = jnp.float32
BF16 = jnp.bfloat16
HIGHEST = lax.Precision.HIGHEST

HEAD_DIM = 128
A_HEADS = 8
A_WIDTH = A_HEADS * HEAD_DIM
MOBA_BLOCK = 256
MOBA_TOPK = 3
ROPE_THETA = 500000.0
ROPE_DIM = HEAD_DIM // 4
ATTN_SCALE = HEAD_DIM ** -0.5
G_HEADS = 4
G_VAL_DIM = 256
G_KEY_DIM = 128
G_KEY_WIDTH = G_HEADS * G_KEY_DIM
G_VAL_WIDTH = G_HEADS * G_VAL_DIM
GATE_RANK = 16
GATE_TAU = 16.0
GLA_CHUNK = 64
N_GROUPS = 4
EXPERTS_PER_GROUP = 8
N_EXPERTS = N_GROUPS * EXPERTS_PER_GROUP
TOP_K = 2
EPS = 1e-6

LANES = 128
PROJ_MAIN = 3 * A_WIDTH + 2 * G_KEY_WIDTH + 2 * G_VAL_WIDTH
ROW_BLOCK = 512
VMEM_LIMIT = 52 * 1024 * 1024
EXPERT_VMEM_LIMIT = 58 * 1024 * 1024


def _cparams(sem, vmem_limit=VMEM_LIMIT):
    return pltpu.CompilerParams(dimension_semantics=sem, vmem_limit_bytes=vmem_limit)


def _pack_bf16_pairs(x):
    c = x.shape[1] // 2
    lo = pltpu.bitcast(x[:, :c].astype(BF16).astype(F32), jnp.uint32)
    hi = pltpu.bitcast(x[:, c:].astype(BF16).astype(F32), jnp.uint32)
    return (lo >> 16) | hi


def _unpack_bf16_pairs(w):
    lo = pltpu.bitcast(w << 16, F32)
    hi = pltpu.bitcast(w & jnp.uint32(0xFFFF0000), F32)
    return lo, hi


def _mod_kernel(c_ref, w_ref, b_ref, o_ref):
    c = c_ref[...]
    ca = c * jax.nn.sigmoid(c)
    o_ref[...] = jnp.dot(ca, w_ref[...], precision=HIGHEST, preferred_element_type=F32) + b_ref[...]


def _adaln_mod(c, w_ada, b_ada):
    depth, d, n6 = w_ada.shape
    b = c.shape[0]
    tn = 1024
    return pl.pallas_call(
        _mod_kernel,
        grid=(depth, n6 // tn),
        in_specs=[
            pl.BlockSpec((b, d), lambda l, n: (0, 0)),
            pl.BlockSpec((None, d, tn), lambda l, n: (l, 0, n)),
            pl.BlockSpec((None, 1, tn), lambda l, n: (l, 0, n)),
        ],
        out_specs=pl.BlockSpec((None, b, tn), lambda l, n: (l, 0, n)),
        out_shape=jax.ShapeDtypeStruct((depth, b, n6), F32),
        compiler_params=_cparams(("arbitrary", "arbitrary")),
        name="adaln_mod",
    )(c, w_ada, b_ada.reshape(depth, 1, n6))


def _modulated_norm(x, g, shift, scale):
    y = x * lax.rsqrt(jnp.mean(x * x, axis=-1, keepdims=True) + EPS) * g
    return y * (1.0 + scale) + shift


def _inproj_kernel(x_ref, ln_ref, mod_ref, w_ref, wga_ref, wgk_ref, bgk_ref, proj_ref, z_ref, h_scr):
    @pl.when(pl.program_id(1) == 0)
    def _():
        h = _modulated_norm(x_ref[...], ln_ref[...], mod_ref[0:1, :], mod_ref[1:2, :])
        hb = h.astype(BF16)
        h_scr[...] = hb
        ga = jnp.dot(hb, wga_ref[...], preferred_element_type=F32)
        z_ref[...] = jnp.dot(ga, wgk_ref[...], precision=HIGHEST, preferred_element_type=F32) + bgk_ref[...]

    proj_ref[...] = jnp.dot(h_scr[...], w_ref[...], preferred_element_type=F32).astype(proj_ref.dtype)


def _in_projection(x2, ln, mod, w_main, w_ga, w_gk, b_gk, seq):
    t, d = x2.shape
    tm, tn = 1024, 1024
    return pl.pallas_call(
        _inproj_kernel,
        grid=(t // tm, PROJ_MAIN // tn),
        in_specs=[
            pl.BlockSpec((tm, d), lambda m, n: (m, 0)),
            pl.BlockSpec((1, d), lambda m, n: (0, 0)),
            pl.BlockSpec((None, 6, d), lambda m, n: ((m * tm) // seq, 0, 0)),
            pl.BlockSpec((d, tn), lambda m, n: (0, n)),
            pl.BlockSpec((d, LANES), lambda m, n: (0, 0)),
            pl.BlockSpec((LANES, G_KEY_WIDTH), lambda m, n: (0, 0)),
            pl.BlockSpec((1, G_KEY_WIDTH), lambda m, n: (0, 0)),
        ],
        out_specs=[
            pl.BlockSpec((tm, tn), lambda m, n: (m, n)),
            pl.BlockSpec((tm, G_KEY_WIDTH), lambda m, n: (m, 0)),
        ],
        out_shape=[
            jax.ShapeDtypeStruct((t, PROJ_MAIN), BF16),
            jax.ShapeDtypeStruct((t, G_KEY_WIDTH), F32),
        ],
        scratch_shapes=[pltpu.VMEM((tm, d), BF16)],
        compiler_params=_cparams(("arbitrary", "arbitrary")),
        name="in_projection",
    )(x2, ln, mod, w_main, w_ga, w_gk, b_gk)


def _rope(x, cos, sin_lo, sin_hi):
    half = ROPE_DIM // 2
    return x * cos + pltpu.roll(x, half, 1) * sin_hi + pltpu.roll(x, LANES - half, 1) * sin_lo


def _moba_kernel(q_ref, k_ref, v_ref, cos_ref, slo_ref, shi_ref, o_ref, s_scr):
    seq = q_ref.shape[0]
    nblk = seq // MOBA_BLOCK
    cos, slo, shi = cos_ref[...], slo_ref[...], shi_ref[...]
    q = _rope(q_ref[...].astype(F32), cos, slo, shi) * ATTN_SCALE
    k = _rope(k_ref[...].astype(F32), cos, slo, shi)
    kmean = jnp.mean(k.reshape(nblk, MOBA_BLOCK, HEAD_DIM), axis=1)
    gate = lax.dot_general(kmean, q, (((1,), (1,)), ((), ())), precision=HIGHEST,
                           preferred_element_type=F32)
    blk_of_q = lax.broadcasted_iota(jnp.int32, (nblk, seq), 1) // MOBA_BLOCK
    row = lax.broadcasted_iota(jnp.int32, (nblk, seq), 0)
    past = row < blk_of_q
    better = jnp.zeros((nblk, seq), F32)
    for m in range(nblk):
        gm = gate[m:m + 1, :]
        past_m = blk_of_q[m:m + 1, :] > m
        beats = (gm > gate) | ((gm == gate) & (row > m))
        better = better + jnp.where(beats & past_m, 1.0, 0.0)
    sel = past & (better < float(MOBA_TOPK))

    qb = q.astype(BF16)
    kb = k.astype(BF16)
    vt = v_ref[...].astype(F32).T.astype(BF16)
    key_i = lax.broadcasted_iota(jnp.int32, (MOBA_BLOCK, MOBA_BLOCK), 0)
    qry_i = lax.broadcasted_iota(jnp.int32, (MOBA_BLOCK, MOBA_BLOCK), 1)
    causal = key_i <= qry_i
    for i in range(nblk):
        qs = slice(i * MOBA_BLOCK, (i + 1) * MOBA_BLOCK)
        qi = qb[qs, :]
        mx = None
        for n in range(i + 1):
            ks = slice(n * MOBA_BLOCK, (n + 1) * MOBA_BLOCK)
            s = lax.dot_general(kb[ks, :], qi, (((1,), (1,)), ((), ())),
                                preferred_element_type=F32)
            keep = causal if n == i else sel[n:n + 1, qs]
            s = jnp.where(keep, s, -jnp.inf)
            s_scr[n] = s
            bm = jnp.max(s, axis=0, keepdims=True)
            mx = bm if mx is None else jnp.maximum(mx, bm)
        den = jnp.zeros((1, MOBA_BLOCK), F32)
        acc = jnp.zeros((HEAD_DIM, MOBA_BLOCK), F32)
        for n in range(i + 1):
            ks = slice(n * MOBA_BLOCK, (n + 1) * MOBA_BLOCK)
            p = jnp.exp(s_scr[n] - mx)
            den = den + jnp.sum(p, axis=0, keepdims=True)
            acc = acc + jnp.dot(vt[:, ks], p.astype(BF16), preferred_element_type=F32)
        o_ref[qs, :] = (acc / den).T.astype(o_ref.dtype)


def _rope_tables(seq):
    half = ROPE_DIM // 2
    inv = jnp.power(ROPE_THETA, -jnp.arange(half, dtype=F32) * (2.0 / ROPE_DIM))
    ang = jnp.arange(seq, dtype=F32)[:, None] * inv[None, :]
    cos, sin = jnp.cos(ang), jnp.sin(ang)
    ones = jnp.ones((seq, HEAD_DIM - ROPE_DIM), F32)
    zeros = jnp.zeros((seq, HEAD_DIM - half), F32)
    cos_t = jnp.concatenate([cos, cos, ones], axis=1)
    sin_lo = jnp.concatenate([-sin, zeros], axis=1)
    sin_hi = jnp.concatenate([jnp.zeros((seq, half), F32), sin, zeros[:, half:]], axis=1)
    return cos_t, sin_lo, sin_hi


def _moba_attention(proj3, tables):
    b, seq, _ = proj3.shape
    head = lambda off: pl.BlockSpec((None, seq, HEAD_DIM), lambda bi, h: (bi, 0, off + h))
    tab = pl.BlockSpec((seq, HEAD_DIM), lambda bi, h: (0, 0))
    return pl.pallas_call(
        _moba_kernel,
        grid=(b, A_HEADS),
        in_specs=[head(0), head(A_HEADS), head(2 * A_HEADS), tab, tab, tab],
        out_specs=pl.BlockSpec((None, seq, HEAD_DIM), lambda bi, h: (bi, 0, h)),
        out_shape=jax.ShapeDtypeStruct((b, seq, A_WIDTH), BF16),
        scratch_shapes=[pltpu.VMEM((seq // MOBA_BLOCK, MOBA_BLOCK, MOBA_BLOCK), F32)],
        compiler_params=_cparams(("arbitrary", "arbitrary")),
        name="moba_attention",
    )(proj3, proj3, proj3, *tables)


def _log_sigmoid(z):
    return jnp.minimum(z, 0.0) - jnp.log1p(jnp.exp(-jnp.abs(z)))


def _chunk_cumsum(x):
    pos = lax.broadcasted_iota(jnp.int32, x.shape, 0) % GLA_CHUNK
    shift = 1
    while shift < GLA_CHUNK:
        x = x + jnp.where(pos >= shift, pltpu.roll(x, shift, 0), 0.0)
        shift *= 2
    return x


def _gla_kernel(q_ref, k_ref, v_ref, og_ref, z_ref, gn_ref, o_ref,
                qd_scr, ki_scr, ke_scr, dec_scr, kv_scr, st_scr, oi_scr):
    seq = q_ref.shape[0]
    nc = seq // GLA_CHUNK
    log_a = _log_sigmoid(z_ref[...]) / GATE_TAU
    b = _chunk_cumsum(log_a)
    b3 = b.reshape(nc, GLA_CHUNK, G_KEY_DIM)
    b_last = b3[:, GLA_CHUNK - 1:GLA_CHUNK, :]
    k = k_ref[...].astype(F32)
    qd_scr[...] = ((q_ref[...].astype(F32) * (G_KEY_DIM ** -0.5)) * jnp.exp(b)).astype(BF16)
    ki_scr[...] = (k * jnp.exp(-b)).astype(BF16)
    ke3 = k.reshape(nc, GLA_CHUNK, G_KEY_DIM) * jnp.exp(b_last - b3)
    ke_scr[...] = ke3.reshape(seq, G_KEY_DIM).astype(BF16)
    dec_scr[...] = jnp.exp(b_last)
    t_i = lax.broadcasted_iota(jnp.int32, (GLA_CHUNK, GLA_CHUNK), 0)
    s_i = lax.broadcasted_iota(jnp.int32, (GLA_CHUNK, GLA_CHUNK), 1)
    causal = s_i <= t_i
    gn = gn_ref[...]

    def chunk_rows(n):
        return pl.ds(pl.multiple_of(n * GLA_CHUNK, GLA_CHUNK), GLA_CHUNK)

    def intra(n, carry):
        rows = chunk_rows(n)
        vb = v_ref[rows, :].astype(BF16)
        att = lax.dot_general(qd_scr[rows, :], ki_scr[rows, :], (((1,), (1,)), ((), ())),
                              preferred_element_type=F32)
        att = jnp.where(causal, att, 0.0).astype(BF16)
        oi_scr[rows, :] = jnp.dot(att, vb, preferred_element_type=F32)
        kv_scr[n] = lax.dot_general(vb, ke_scr[rows, :], (((0,), (0,)), ((), ())),
                                    preferred_element_type=F32)
        return carry

    lax.fori_loop(0, nc, intra, 0, unroll=GLA_UNROLL)

    def scan(n, state):
        st_scr[n] = state.astype(BF16)
        return state * dec_scr[n] + kv_scr[n]

    lax.fori_loop(0, nc, scan, jnp.zeros((G_VAL_DIM, G_KEY_DIM), F32), unroll=GLA_UNROLL)

    def inter(n, carry):
        rows = chunk_rows(n)
        o = oi_scr[rows, :] + lax.dot_general(qd_scr[rows, :], st_scr[n], (((1,), (1,)), ((), ())),
                                              preferred_element_type=F32)
        o = o * lax.rsqrt(jnp.mean(o * o, axis=-1, keepdims=True) + EPS) * gn
        og = og_ref[rows, :].astype(F32)
        o_ref[rows, :] = (o * (og * jax.nn.sigmoid(og))).astype(o_ref.dtype)
        return carry

    lax.fori_loop(0, nc, inter, 0, unroll=GLA_UNROLL)


GLA_UNROLL = 4


def _gla_mixer(proj3, z3, g_onorm):
    b, seq, _ = proj3.shape
    kq = 3 * A_WIDTH // G_KEY_DIM
    kv = (3 * A_WIDTH + 2 * G_KEY_WIDTH) // G_VAL_DIM
    key = lambda off: pl.BlockSpec((None, seq, G_KEY_DIM), lambda bi, h: (bi, 0, off + h))
    val = lambda off: pl.BlockSpec((None, seq, G_VAL_DIM), lambda bi, h: (bi, 0, off + h))
    return pl.pallas_call(
        _gla_kernel,
        grid=(b, G_HEADS),
        in_specs=[
            key(kq), key(kq + G_HEADS), val(kv), val(kv + G_HEADS),
            pl.BlockSpec((None, seq, G_KEY_DIM), lambda bi, h: (bi, 0, h)),
            pl.BlockSpec((1, G_VAL_DIM), lambda bi, h: (0, 0)),
        ],
        out_specs=pl.BlockSpec((None, seq, G_VAL_DIM), lambda bi, h: (bi, 0, h)),
        out_shape=jax.ShapeDtypeStruct((b, seq, G_VAL_WIDTH), BF16),
        scratch_shapes=[
            pltpu.VMEM((seq, G_KEY_DIM), BF16),
            pltpu.VMEM((seq, G_KEY_DIM), BF16),
            pltpu.VMEM((seq, G_KEY_DIM), BF16),
            pltpu.VMEM((seq // GLA_CHUNK, 1, G_KEY_DIM), F32),
            pltpu.VMEM((seq // GLA_CHUNK, G_VAL_DIM, G_KEY_DIM), F32),
            pltpu.VMEM((seq // GLA_CHUNK, G_VAL_DIM, G_KEY_DIM), BF16),
            pltpu.VMEM((seq, G_VAL_DIM), F32),
        ],
        compiler_params=_cparams(("arbitrary", "arbitrary")),
        name="gla_mixer",
    )(proj3, proj3, proj3, proj3, z3, g_onorm)


def _first_lane(mask, lane):
    return jnp.min(jnp.where(mask, lane, LANES), axis=-1, keepdims=True)


def _split_bf16(x):
    hi = x.astype(BF16)
    return hi, (x - hi.astype(F32)).astype(BF16)


def _outproj_router_kernel(oa_ref, ob_ref, x_ref, wa_ref, wb_ref, mod_ref, ln_ref, wr_ref, br_ref,
                           xo_ref, h_ref, ri_ref, rw_ref, cnt_ref, run_scr):
    tm = x_ref.shape[0]

    @pl.when(pl.program_id(0) == 0)
    def _():
        run_scr[...] = jnp.zeros_like(run_scr)

    mix = jnp.dot(oa_ref[...], wa_ref[...], preferred_element_type=F32)
    mix = mix + jnp.dot(ob_ref[...], wb_ref[...], preferred_element_type=F32)
    x_new = x_ref[...] + mod_ref[2:3, :] * mix
    xo_ref[...] = x_new
    h = _modulated_norm(x_new, ln_ref[...], mod_ref[3:4, :], mod_ref[4:5, :])
    h_ref[...] = _pack_bf16_pairs(h)

    hh = jnp.concatenate(_split_bf16(h), axis=0)
    r = jnp.dot(hh, wr_ref[...], preferred_element_type=F32)
    logits = (r[:tm, :LANES] + r[tm:, :LANES]) + (r[:tm, LANES:] + r[tm:, LANES:]) + br_ref[...]
    lane = lax.broadcasted_iota(jnp.int32, (tm, LANES), 1)
    is_grp = (lane >= N_EXPERTS) & (lane < N_EXPERTS + N_GROUPS)
    l1 = jnp.where(is_grp, logits, -jnp.inf)
    m1 = jnp.max(l1, axis=-1, keepdims=True)
    grp = _first_lane(l1 == m1, lane) - N_EXPERTS
    p_grp = 1.0 / jnp.sum(jnp.exp(l1 - m1), axis=-1, keepdims=True)
    in_grp = (lane < N_EXPERTS) & ((lane // EXPERTS_PER_GROUP) == grp)
    l2 = jnp.where(in_grp, logits, -jnp.inf)
    va = jnp.max(l2, axis=-1, keepdims=True)
    ia = _first_lane(l2 == va, lane)
    l2b = jnp.where(lane == ia, -jnp.inf, l2)
    vb = jnp.max(l2b, axis=-1, keepdims=True)
    ib = _first_lane(l2b == vb, lane)
    eb = jnp.exp(vb - va)
    wa = p_grp * (1.0 / (1.0 + eb))
    wb = p_grp * (eb / (1.0 + eb))

    oh_a = jnp.where(lane == ia, 1.0, 0.0)
    oh_b = jnp.where(lane == ib, 1.0, 0.0)
    cnt = oh_a + oh_b
    r_i = lax.broadcasted_iota(jnp.int32, (tm, tm), 0)
    c_i = lax.broadcasted_iota(jnp.int32, (tm, tm), 1)
    strict_lower = jnp.where(c_i < r_i, 1.0, 0.0).astype(BF16)
    before = jnp.dot(strict_lower, cnt.astype(BF16), preferred_element_type=F32) + run_scr[...]
    rank_a = jnp.sum(oh_a * before, axis=-1, keepdims=True)
    rank_b = jnp.sum(oh_b * before, axis=-1, keepdims=True)
    run_new = run_scr[...] + jnp.sum(cnt, axis=0, keepdims=True)
    run_scr[...] = run_new
    cnt_ref[...] = jnp.broadcast_to(run_new, cnt_ref.shape).astype(jnp.int32)

    ri = jnp.where(lane == 0, ia, 0) + jnp.where(lane == 1, ib, 0)
    ri = ri + jnp.where(lane == 2, rank_a.astype(jnp.int32), 0) + jnp.where(lane == 3, rank_b.astype(jnp.int32), 0)
    ri_ref[...] = ri
    rw_ref[...] = jnp.where(lane == 0, wa, 0.0) + jnp.where(lane == 1, wb, 0.0)


def _outproj_router(oa2, ob2, x2, w_oa, w_ob, mod, ln, w_r, b_r, seq):
    t, d = x2.shape
    tm = 256
    row = lambda w: pl.BlockSpec((tm, w), lambda m: (m, 0))
    full = lambda a, c: pl.BlockSpec((a, c), lambda m: (0, 0))
    return pl.pallas_call(
        _outproj_router_kernel,
        grid=(t // tm,),
        in_specs=[
            row(A_WIDTH), row(G_VAL_WIDTH), row(d),
            full(A_WIDTH, d), full(G_VAL_WIDTH, d),
            pl.BlockSpec((None, 6, d), lambda m: ((m * tm) // seq, 0, 0)),
            full(1, d), full(d, 2 * LANES), full(1, LANES),
        ],
        out_specs=[row(d), row(d // 2), row(LANES), row(LANES), full(8, LANES)],
        out_shape=[
            jax.ShapeDtypeStruct((t, d), F32),
            jax.ShapeDtypeStruct((t, d // 2), jnp.uint32),
            jax.ShapeDtypeStruct((t, LANES), jnp.int32),
            jax.ShapeDtypeStruct((t, LANES), F32),
            jax.ShapeDtypeStruct((8, LANES), jnp.int32),
        ],
        scratch_shapes=[pltpu.VMEM((1, LANES), F32)],
        compiler_params=_cparams(("arbitrary",)),
        name="outproj_router",
    )(oa2, ob2, x2, w_oa, w_ob, mod, ln, w_r, b_r)


def _wait_row_gather(src_hbm, dst_vmem, sem):
    pltpu.make_async_copy(src_hbm.at[pl.ds(0, dst_vmem.shape[0]), :], dst_vmem, sem).wait()


def _expert_kernel(cur_ref, nxt_ref, last_ref, nused_ref, tok_ref, h_hbm, wg_ref, wu_ref, wd_ref,
                   y_ref, wg_res, wu_res, wd_res, x_scr, xb_scr, acc_scr, sems):
    del cur_ref, nxt_ref
    b, j = pl.program_id(0), pl.program_id(1)
    n_used = nused_ref[0]

    def start_gather(jj):
        if jj != 0:
            return
        slot = b % 2
        for r in range(ROW_BLOCK):
            pltpu.make_async_copy(h_hbm.at[pl.ds(tok_ref[0, r], 1), :],
                                  x_scr.at[slot, pl.ds(r, 1), :], sems.at[slot]).start()

    def wait_gather(slot):
        _wait_row_gather(h_hbm, x_scr.at[slot], sems.at[slot])

    @pl.when((b >= 1) & (b <= n_used) & (j == 0))
    def _():
        slot = (b - 1) % 2
        wait_gather(slot)
        lo, hi = _unpack_bf16_pairs(x_scr[slot])
        half = lo.shape[1]
        xb_scr[:, :half] = lo.astype(BF16)
        xb_scr[:, half:] = hi.astype(BF16)

    for jj in range(EXPERT_HIDDEN_CHUNKS):
        @pl.when((b == 0) & (j == jj))
        def _(jj=jj):
            start_gather(jj)

        @pl.when((b >= 1) & (b <= n_used) & (j == jj))
        def _(jj=jj):
            xb = xb_scr[...]
            g = jnp.dot(xb, wg_res[jj], preferred_element_type=F32)
            u = jnp.dot(xb, wu_res[jj], preferred_element_type=F32)
            a = ((g * jax.nn.sigmoid(g)) * u).astype(BF16)
            start_gather(jj)
            part = jnp.dot(a, wd_res[jj], preferred_element_type=F32)
            if jj == 0:
                acc_scr[...] = part
            elif jj < EXPERT_HIDDEN_CHUNKS - 1:
                acc_scr[...] += part
            else:
                y_ref[...] = _pack_bf16_pairs(acc_scr[...] + part)

    @pl.when((b == n_used + 1) & (j == 0))
    def _():
        wait_gather(n_used % 2)

    @pl.when((b > n_used) & (b <= pl.num_programs(0) - 2) & (j == 0))
    def _():
        y_ref[...] = jnp.zeros_like(y_ref)

    @pl.when(last_ref[b] == 1)
    def _():
        wg_res[j] = wg_ref[...].astype(BF16)
        wu_res[j] = wu_ref[...].astype(BF16)
        wd_res[j] = wd_ref[...].astype(BF16)


EXPERT_HIDDEN_CHUNKS = 2


def _expert_mlp(cur_e, nxt_e, last, n_used, slot_tok3, h2, wg, wu, wd, layer):
    nb = slot_tok3.shape[0]
    d, f = wg.shape[2], wg.shape[3]
    dp = h2.shape[1]
    nj = EXPERT_HIDDEN_CHUNKS
    fc = f // nj

    def w_index(chunk_axis):
        def index(b, j, cur, nxt, lst, nu):
            e = jnp.where(lst[b] == 1, nxt[b], cur[b])
            jj = jnp.where(lst[b] == 1, j, nj - 1)
            return (layer, e, 0, jj) if chunk_axis == 2 else (layer, e, jj, 0)
        return index

    grid_spec = pltpu.PrefetchScalarGridSpec(
        num_scalar_prefetch=4,
        grid=(nb + 2, nj),
        in_specs=[
            pl.BlockSpec((None, 1, ROW_BLOCK), lambda b, j, *_: (jnp.minimum(b, nb - 1), 0, 0),
                         memory_space=pltpu.SMEM),
            pl.BlockSpec(memory_space=pl.ANY),
            pl.BlockSpec((None, None, d, fc), w_index(2)),
            pl.BlockSpec((None, None, d, fc), w_index(2)),
            pl.BlockSpec((None, None, fc, d), w_index(1)),
        ],
        out_specs=pl.BlockSpec((ROW_BLOCK, dp), lambda b, j, *_: (jnp.clip(b - 1, 0, nb - 1), 0)),
        scratch_shapes=[
            pltpu.VMEM((nj, d, fc), BF16),
            pltpu.VMEM((nj, d, fc), BF16),
            pltpu.VMEM((nj, fc, d), BF16),
            pltpu.VMEM((2, ROW_BLOCK, dp), jnp.uint32),
            pltpu.VMEM((ROW_BLOCK, d), BF16),
            pltpu.VMEM((ROW_BLOCK, d), F32),
            pltpu.SemaphoreType.DMA((2,)),
        ],
    )
    return pl.pallas_call(
        _expert_kernel,
        grid_spec=grid_spec,
        out_shape=jax.ShapeDtypeStruct((nb * ROW_BLOCK, dp), jnp.uint32),
        compiler_params=_cparams(("arbitrary", "arbitrary"), EXPERT_VMEM_LIMIT),
        name="expert_mlp",
    )(cur_e, nxt_e, last, n_used, slot_tok3, h2, wg, wu, wd)


def _combine_kernel(dest_ref, y_hbm, rw_ref, x_ref, mod_ref, lnf_ref, o_ref, y_scr, sems, *, final):
    tm = x_ref.shape[0]
    i = pl.program_id(0)
    n_tiles = pl.num_programs(0) - COMBINE_LOOKAHEAD
    n_slots = COMBINE_LOOKAHEAD + 1

    def start_gather():
        slot = i % n_slots
        for r in range(2 * tm):
            pltpu.make_async_copy(y_hbm.at[pl.ds(dest_ref[0, r], 1), :],
                                  y_scr.at[slot, pl.ds(r, 1), :], sems.at[slot]).start()

    def combine(prefetch):
        slot = (i - COMBINE_LOOKAHEAD) % n_slots
        _wait_row_gather(y_hbm, y_scr.at[slot], sems.at[slot])
        rw = rw_ref[...]
        a_lo, a_hi = _unpack_bf16_pairs(y_scr[slot, 0:tm, :])
        b_lo, b_hi = _unpack_bf16_pairs(y_scr[slot, tm:2 * tm, :])
        y = jnp.concatenate([rw[:, 0:1] * a_lo + rw[:, 1:2] * b_lo,
                             rw[:, 0:1] * a_hi + rw[:, 1:2] * b_hi], axis=1)
        if prefetch:
            start_gather()
        x_new = x_ref[...] + mod_ref[5:6, :] * y
        if final:
            x_new = x_new * lax.rsqrt(jnp.mean(x_new * x_new, axis=-1, keepdims=True) + EPS) * lnf_ref[...]
        o_ref[...] = x_new

    pl.when(i < COMBINE_LOOKAHEAD)(start_gather)
    pl.when((i >= COMBINE_LOOKAHEAD) & (i < n_tiles))(functools.partial(combine, True))
    pl.when(i >= n_tiles)(functools.partial(combine, False))


COMBINE_LOOKAHEAD = 2


def _combine(dest3, yg, rw, x2, mod, ln_f, seq, final):
    t, d = x2.shape
    tm = dest3.shape[2] // 2
    nt = t // tm
    prev = lambda m: jnp.maximum(m - COMBINE_LOOKAHEAD, 0)
    return pl.pallas_call(
        functools.partial(_combine_kernel, final=final),
        grid=(nt + COMBINE_LOOKAHEAD,),
        in_specs=[
            pl.BlockSpec((None, 1, 2 * tm), lambda m: (jnp.minimum(m, nt - 1), 0, 0), memory_space=pltpu.SMEM),
            pl.BlockSpec(memory_space=pl.ANY),
            pl.BlockSpec((tm, LANES), lambda m: (prev(m), 0)),
            pl.BlockSpec((tm, d), lambda m: (prev(m), 0)),
            pl.BlockSpec((None, 6, d), lambda m: ((prev(m) * tm) // seq, 0, 0)),
            pl.BlockSpec((1, d), lambda m: (0, 0)),
        ],
        out_specs=pl.BlockSpec((tm, d), lambda m: (prev(m), 0)),
        out_shape=jax.ShapeDtypeStruct((t, d), F32),
        scratch_shapes=[pltpu.VMEM((COMBINE_LOOKAHEAD + 1, 2 * tm, yg.shape[1]), jnp.uint32),
                        pltpu.SemaphoreType.DMA((COMBINE_LOOKAHEAD + 1,))],
        compiler_params=_cparams(("arbitrary",)),
        name="moe_combine",
    )(dest3, yg, rw, x2, mod, ln_f)


def _routing_tables(ri, counts, t):
    n_blocks = (t * TOP_K) // ROW_BLOCK + N_EXPERTS
    cnt = counts[0, :N_EXPERTS]
    padded = (cnt + ROW_BLOCK - 1) // ROW_BLOCK * ROW_BLOCK
    ends = jnp.cumsum(padded)
    starts = ends - padded
    experts, ranks = ri[:, 0:TOP_K], ri[:, TOP_K:2 * TOP_K]
    dest = starts[experts] + ranks
    tokens = jnp.broadcast_to(jnp.arange(t, dtype=jnp.int32)[:, None], (t, TOP_K))
    slot_tok = jnp.zeros((n_blocks * ROW_BLOCK,), jnp.int32).at[dest.reshape(-1)].set(tokens.reshape(-1))
    n_used = (ends[-1] // ROW_BLOCK).astype(jnp.int32)
    blocks = jnp.minimum(jnp.arange(n_blocks + 2, dtype=jnp.int32), n_used - 1) * ROW_BLOCK
    block_exp = jnp.sum((blocks[:, None] >= ends[None, :]).astype(jnp.int32), axis=1)
    block_exp = jnp.minimum(block_exp, N_EXPERTS - 1)
    cur_e = jnp.concatenate([block_exp[:1], block_exp[:-1]])
    nxt_e = block_exp
    last = (cur_e != nxt_e).astype(jnp.int32).at[0].set(1)
    return (dest.astype(jnp.int32), slot_tok.reshape(n_blocks, 1, ROW_BLOCK),
            cur_e, nxt_e, last, n_used.reshape(1))


def kernel(x, c, ln1, ln2, w_ada, b_ada, w_in, w_gk, b_gk, g_onorm, w_out,
           w_r1, b_r1, w_r2, b_r2, w_e_gate, w_e_up, w_e_down, ln_f):
    b, seq, d = x.shape
    depth = w_ada.shape[0]
    t = b * seq
    mod_all = _adaln_mod(c, w_ada, b_ada).reshape(depth, b, 6, d)
    tables = _rope_tables(seq)
    x2 = x.reshape(t, d)
    tm_c = 256
    for l in range(depth):
        mod = mod_all[l]
        w_main = w_in[l][:, :PROJ_MAIN].astype(BF16)
        w_ga = jnp.pad(w_in[l][:, PROJ_MAIN:], ((0, 0), (0, LANES - GATE_RANK))).astype(BF16)
        w_gk_p = jnp.pad(w_gk[l], ((0, LANES - GATE_RANK), (0, 0)))
        proj, z = _in_projection(x2, ln1[l][None, :], mod, w_main, w_ga, w_gk_p, b_gk[l][None, :], seq)
        proj3 = proj.reshape(b, seq, PROJ_MAIN)
        oa = _moba_attention(proj3, tables)
        ob = _gla_mixer(proj3, z.reshape(b, seq, G_KEY_WIDTH), g_onorm[l][None, :])
        w_o = w_out[l].astype(BF16)
        w_r = jnp.pad(jnp.concatenate([w_r2[l], w_r1[l]], axis=1), ((0, 0), (0, LANES - N_EXPERTS - N_GROUPS)))
        w_r_hi = w_r.astype(BF16)
        w_r = jnp.concatenate([w_r_hi, (w_r - w_r_hi.astype(F32)).astype(BF16)], axis=1)
        b_r = jnp.pad(jnp.concatenate([b_r2[l], b_r1[l]]), (0, LANES - N_EXPERTS - N_GROUPS))[None, :]
        x2, h2, ri, rw, counts = _outproj_router(
            oa.reshape(t, A_WIDTH), ob.reshape(t, G_VAL_WIDTH), x2, w_o[:A_WIDTH], w_o[A_WIDTH:],
            mod, ln2[l][None, :], w_r, b_r, seq)
        dest, slot_tok3, cur_e, nxt_e, last, n_used = _routing_tables(ri, counts, t)
        yg = _expert_mlp(cur_e, nxt_e, last, n_used, slot_tok3, h2, w_e_gate, w_e_up, w_e_down, l)
        dest3 = dest.reshape(t // tm_c, tm_c, TOP_K).transpose(0, 2, 1).reshape(t // tm_c, 1, TOP_K * tm_c)
        x2 = _combine(dest3, yg, rw, x2, mod, ln_f[None, :], seq, final=(l == depth - 1))
    return x2.reshape(b, seq, d)
```

```python
import functools

import jax
import jax.numpy as jnp
import numpy as np
from jax import lax
from jax.experimental import pallas as pl
from jax.experimental.pallas import tpu as pltpu

F32 = jnp.float32
BF16 = jnp.bfloat16
HIGHEST = lax.Precision.HIGHEST

HEAD_DIM = 128
A_HEADS = 8
A_WIDTH = A_HEADS * HEAD_DIM
MOBA_BLOCK = 256
MOBA_TOPK = 3
ROPE_THETA = 500000.0
ROPE_DIM = HEAD_DIM // 4
ATTN_SCALE = HEAD_DIM ** -0.5
G_HEADS = 4
G_VAL_DIM = 256
G_KEY_DIM = 128
G_KEY_WIDTH = G_HEADS * G_KEY_DIM
G_VAL_WIDTH = G_HEADS * G_VAL_DIM
GATE_RANK = 16
GATE_TAU = 16.0
GLA_CHUNK = 64
N_GROUPS = 4
EXPERTS_PER_GROUP = 8
N_EXPERTS = N_GROUPS * EXPERTS_PER_GROUP
TOP_K = 2
EPS = 1e-6

LANES = 128
PROJ_MAIN = 3 * A_WIDTH + 2 * G_KEY_WIDTH + 2 * G_VAL_WIDTH
ROW_BLOCK = 512
VMEM_LIMIT = 52 * 1024 * 1024
EXPERT_VMEM_LIMIT = 58 * 1024 * 1024


def _cparams(sem, vmem_limit=VMEM_LIMIT):
    return pltpu.CompilerParams(dimension_semantics=sem, vmem_limit_bytes=vmem_limit)


def _pack_bf16_pairs(x):
    c = x.shape[1] // 2
    lo = pltpu.bitcast(x[:, :c].astype(BF16).astype(F32), jnp.uint32)
    hi = pltpu.bitcast(x[:, c:].astype(BF16).astype(F32), jnp.uint32)
    return (lo >> 16) | hi


def _unpack_bf16_pairs(w):
    lo = pltpu.bitcast(w << 16, F32)
    hi = pltpu.bitcast(w & jnp.uint32(0xFFFF0000), F32)
    return lo, hi


def _mod_kernel(c_ref, w_ref, b_ref, o_ref):
    c = c_ref[...]
    ca = c * jax.nn.sigmoid(c)
    o_ref[...] = jnp.dot(ca, w_ref[...], precision=HIGHEST, preferred_element_type=F32) + b_ref[...]


def _adaln_mod(c, w_ada, b_ada):
    depth, d, n6 = w_ada.shape
    b = c.shape[0]
    tn = 1024
    return pl.pallas_call(
        _mod_kernel,
        grid=(depth, n6 // tn),
        in_specs=[
            pl.BlockSpec((b, d), lambda l, n: (0, 0)),
            pl.BlockSpec((None, d, tn), lambda l, n: (l, 0, n)),
            pl.BlockSpec((None, 1, tn), lambda l, n: (l, 0, n)),
        ],
        out_specs=pl.BlockSpec((None, b, tn), lambda l, n: (l, 0, n)),
        out_shape=jax.ShapeDtypeStruct((depth, b, n6), F32),
        compiler_params=_cparams(("arbitrary", "arbitrary")),
        name="adaln_mod",
    )(c, w_ada, b_ada.reshape(depth, 1, n6))


def _modulated_norm(x, g, shift, scale):
    y = x * lax.rsqrt(jnp.mean(x * x, axis=-1, keepdims=True) + EPS) * g
    return y * (1.0 + scale) + shift


def _inproj_kernel(x_ref, ln_ref, mod_ref, w_ref, wga_ref, wgk_ref, bgk_ref, proj_ref, z_ref, h_scr):
    @pl.when(pl.program_id(1) == 0)
    def _():
        h = _modulated_norm(x_ref[...], ln_ref[...], mod_ref[0:1, :], mod_ref[1:2, :])
        hb = h.astype(BF16)
        h_scr[...] = hb
        ga = jnp.dot(hb, wga_ref[...], preferred_element_type=F32)
        z_ref[...] = jnp.dot(ga, wgk_ref[...], precision=HIGHEST, preferred_element_type=F32) + bgk_ref[...]

    proj_ref[...] = jnp.dot(h_scr[...], w_ref[...], preferred_element_type=F32).astype(proj_ref.dtype)


def _in_projection(x2, ln, mod, w_main, w_ga, w_gk, b_gk, seq):
    t, d = x2.shape
    tm, tn = 1024, 1024
    return pl.pallas_call(
        _inproj_kernel,
        grid=(t // tm, PROJ_MAIN // tn),
        in_specs=[
            pl.BlockSpec((tm, d), lambda m, n: (m, 0)),
            pl.BlockSpec((1, d), lambda m, n: (0, 0)),
            pl.BlockSpec((None, 6, d), lambda m, n: ((m * tm) // seq, 0, 0)),
            pl.BlockSpec((d, tn), lambda m, n: (0, n)),
            pl.BlockSpec((d, LANES), lambda m, n: (0, 0)),
            pl.BlockSpec((LANES, G_KEY_WIDTH), lambda m, n: (0, 0)),
            pl.BlockSpec((1, G_KEY_WIDTH), lambda m, n: (0, 0)),
        ],
        out_specs=[
            pl.BlockSpec((tm, tn), lambda m, n: (m, n)),
            pl.BlockSpec((tm, G_KEY_WIDTH), lambda m, n: (m, 0)),
        ],
        out_shape=[
            jax.ShapeDtypeStruct((t, PROJ_MAIN), BF16),
            jax.ShapeDtypeStruct((t, G_KEY_WIDTH), F32),
        ],
        scratch_shapes=[pltpu.VMEM((tm, d), BF16)],
        compiler_params=_cparams(("arbitrary", "arbitrary")),
        name="in_projection",
    )(x2, ln, mod, w_main, w_ga, w_gk, b_gk)


def _rope(x, cos, sin_lo, sin_hi):
    half = ROPE_DIM // 2
    return x * cos + pltpu.roll(x, half, 1) * sin_hi + pltpu.roll(x, LANES - half, 1) * sin_lo


def _moba_kernel(q_ref, k_ref, v_ref, cos_ref, slo_ref, shi_ref, o_ref, s_scr):
    seq = q_ref.shape[0]
    nblk = seq // MOBA_BLOCK
    cos, slo, shi = cos_ref[...], slo_ref[...], shi_ref[...]
    q = _rope(q_ref[...].astype(F32), cos, slo, shi) * ATTN_SCALE
    k = _rope(k_ref[...].astype(F32), cos, slo, shi)
    kmean = jnp.mean(k.reshape(nblk, MOBA_BLOCK, HEAD_DIM), axis=1)
    gate = lax.dot_general(kmean, q, (((1,), (1,)), ((), ())), precision=HIGHEST,
                           preferred_element_type=F32)
    blk_of_q = lax.broadcasted_iota(jnp.int32, (nblk, seq), 1) // MOBA_BLOCK
    row = lax.broadcasted_iota(jnp.int32, (nblk, seq), 0)
    past = row < blk_of_q
    better = jnp.zeros((nblk, seq), F32)
    for m in range(nblk):
        gm = gate[m:m + 1, :]
        past_m = blk_of_q[m:m + 1, :] > m
        beats = (gm > gate) | ((gm == gate) & (row > m))
        better = better + jnp.where(beats & past_m, 1.0, 0.0)
    sel = past & (better < float(MOBA_TOPK))

    qb = q.astype(BF16)
    kb = k.astype(BF16)
    vt = v_ref[...].astype(F32).T.astype(BF16)
    key_i = lax.broadcasted_iota(jnp.int32, (MOBA_BLOCK, MOBA_BLOCK), 0)
    qry_i = lax.broadcasted_iota(jnp.int32, (MOBA_BLOCK, MOBA_BLOCK), 1)
    causal = key_i <= qry_i
    for i in range(nblk):
        qs = slice(i * MOBA_BLOCK, (i + 1) * MOBA_BLOCK)
        qi = qb[qs, :]
        mx = None
        for n in range(i + 1):
            ks = slice(n * MOBA_BLOCK, (n + 1) * MOBA_BLOCK)
            s = lax.dot_general(kb[ks, :], qi, (((1,), (1,)), ((), ())),
                                preferred_element_type=F32)
            keep = causal if n == i else sel[n:n + 1, qs]
            s = jnp.where(keep, s, -jnp.inf)
            s_scr[n] = s
            bm = jnp.max(s, axis=0, keepdims=True)
            mx = bm if mx is None else jnp.maximum(mx, bm)
        den = jnp.zeros((1, MOBA_BLOCK), F32)
        acc = jnp.zeros((HEAD_DIM, MOBA_BLOCK), F32)
        for n in range(i + 1):
            ks = slice(n * MOBA_BLOCK, (n + 1) * MOBA_BLOCK)
            p = jnp.exp(s_scr[n] - mx)
            den = den + jnp.sum(p, axis=0, keepdims=True)
            acc = acc + jnp.dot(vt[:, ks], p.astype(BF16), preferred_element_type=F32)
        o_ref[qs, :] = (acc / den).T.astype(o_ref.dtype)


def _rope_tables(seq):
    half = ROPE_DIM // 2
    inv = jnp.power(ROPE_THETA, -jnp.arange(half, dtype=F32) * (2.0 / ROPE_DIM))
    ang = jnp.arange(seq, dtype=F32)[:, None] * inv[None, :]
    cos, sin = jnp.cos(ang), jnp.sin(ang)
    ones = jnp.ones((seq, HEAD_DIM - ROPE_DIM), F32)
    zeros = jnp.zeros((seq, HEAD_DIM - half), F32)
    cos_t = jnp.concatenate([cos, cos, ones], axis=1)
    sin_lo = jnp.concatenate([-sin, zeros], axis=1)
    sin_hi = jnp.concatenate([jnp.zeros((seq, half), F32), sin, zeros[:, half:]], axis=1)
    return cos_t, sin_lo, sin_hi


def _moba_attention(proj3, tables):
    b, seq, _ = proj3.shape
    head = lambda off: pl.BlockSpec((None, seq, HEAD_DIM), lambda bi, h: (bi, 0, off + h))
    tab = pl.BlockSpec((seq, HEAD_DIM), lambda bi, h: (0, 0))
    return pl.pallas_call(
        _moba_kernel,
        grid=(b, A_HEADS),
        in_specs=[head(0), head(A_HEADS), head(2 * A_HEADS), tab, tab, tab],
        out_specs=pl.BlockSpec((None, seq, HEAD_DIM), lambda bi, h: (bi, 0, h)),
        out_shape=jax.ShapeDtypeStruct((b, seq, A_WIDTH), BF16),
        scratch_shapes=[pltpu.VMEM((seq // MOBA_BLOCK, MOBA_BLOCK, MOBA_BLOCK), F32)],
        compiler_params=_cparams(("arbitrary", "arbitrary")),
        name="moba_attention",
    )(proj3, proj3, proj3, *tables)


def _log_sigmoid(z):
    return jnp.minimum(z, 0.0) - jnp.log1p(jnp.exp(-jnp.abs(z)))


def _chunk_cumsum(x):
    pos = lax.broadcasted_iota(jnp.int32, x.shape, 0) % GLA_CHUNK
    shift = 1
    while shift < GLA_CHUNK:
        x = x + jnp.where(pos >= shift, pltpu.roll(x, shift, 0), 0.0)
        shift *= 2
    return x


def _gla_kernel(q_ref, k_ref, v_ref, og_ref, z_ref, gn_ref, o_ref,
                qd_scr, ki_scr, ke_scr, dec_scr, kv_scr, st_scr, oi_scr):
    seq = q_ref.shape[0]
    nc = seq // GLA_CHUNK
    log_a = _log_sigmoid(z_ref[...]) / GATE_TAU
    b = _chunk_cumsum(log_a)
    b3 = b.reshape(nc, GLA_CHUNK, G_KEY_DIM)
    b_last = b3[:, GLA_CHUNK - 1:GLA_CHUNK, :]
    k = k_ref[...].astype(F32)
    qd_scr[...] = ((q_ref[...].astype(F32) * (G_KEY_DIM ** -0.5)) * jnp.exp(b)).astype(BF16)
    ki_scr[...] = (k * jnp.exp(-b)).astype(BF16)
    ke3 = k.reshape(nc, GLA_CHUNK, G_KEY_DIM) * jnp.exp(b_last - b3)
    ke_scr[...] = ke3.reshape(seq, G_KEY_DIM).astype(BF16)
    dec_scr[...] = jnp.exp(b_last)
    t_i = lax.broadcasted_iota(jnp.int32, (GLA_CHUNK, GLA_CHUNK), 0)
    s_i = lax.broadcasted_iota(jnp.int32, (GLA_CHUNK, GLA_CHUNK), 1)
    causal = s_i <= t_i
    gn = gn_ref[...]

    def chunk_rows(n):
        return pl.ds(pl.multiple_of(n * GLA_CHUNK, GLA_CHUNK), GLA_CHUNK)

    def intra(n, carry):
        rows = chunk_rows(n)
        vb = v_ref[rows, :].astype(BF16)
        att = lax.dot_general(qd_scr[rows, :], ki_scr[rows, :], (((1,), (1,)), ((), ())),
                              preferred_element_type=F32)
        att = jnp.where(causal, att, 0.0).astype(BF16)
        oi_scr[rows, :] = jnp.dot(att, vb, preferred_element_type=F32)
        kv_scr[n] = lax.dot_general(vb, ke_scr[rows, :], (((0,), (0,)), ((), ())),
                                    preferred_element_type=F32)
        return carry

    lax.fori_loop(0, nc, intra, 0, unroll=GLA_UNROLL)

    def scan(n, state):
        st_scr[n] = state.astype(BF16)
        return state * dec_scr[n] + kv_scr[n]

    lax.fori_loop(0, nc, scan, jnp.zeros((G_VAL_DIM, G_KEY_DIM), F32), unroll=GLA_UNROLL)

    def inter(n, carry):
        rows = chunk_rows(n)
        o = oi_scr[rows, :] + lax.dot_general(qd_scr[rows, :], st_scr[n], (((1,), (1,)), ((), ())),
                                              preferred_element_type=F32)
        o = o * lax.rsqrt(jnp.mean(o * o, axis=-1, keepdims=True) + EPS) * gn
        og = og_ref[rows, :].astype(F32)
        o_ref[rows, :] = (o * (og * jax.nn.sigmoid(og))).astype(o_ref.dtype)
        return carry

    lax.fori_loop(0, nc, inter, 0, unroll=GLA_UNROLL)


GLA_UNROLL = 4


def _gla_mixer(proj3, z3, g_onorm):
    b, seq, _ = proj3.shape
    kq = 3 * A_WIDTH // G_KEY_DIM
    kv = (3 * A_WIDTH + 2 * G_KEY_WIDTH) // G_VAL_DIM
    key = lambda off: pl.BlockSpec((None, seq, G_KEY_DIM), lambda bi, h: (bi, 0, off + h))
    val = lambda off: pl.BlockSpec((None, seq, G_VAL_DIM), lambda bi, h: (bi, 0, off + h))
    return pl.pallas_call(
        _gla_kernel,
        grid=(b, G_HEADS),
        in_specs=[
            key(kq), key(kq + G_HEADS), val(kv), val(kv + G_HEADS),
            pl.BlockSpec((None, seq, G_KEY_DIM), lambda bi, h: (bi, 0, h)),
            pl.BlockSpec((1, G_VAL_DIM), lambda bi, h: (0, 0)),
        ],
        out_specs=pl.BlockSpec((None, seq, G_VAL_DIM), lambda bi, h: (bi, 0, h)),
        out_shape=jax.ShapeDtypeStruct((b, seq, G_VAL_WIDTH), BF16),
        scratch_shapes=[
            pltpu.VMEM((seq, G_KEY_DIM), BF16),
            pltpu.VMEM((seq, G_KEY_DIM), BF16),
            pltpu.VMEM((seq, G_KEY_DIM), BF16),
            pltpu.VMEM((seq // GLA_CHUNK, 1, G_KEY_DIM), F32),
            pltpu.VMEM((seq // GLA_CHUNK, G_VAL_DIM, G_KEY_DIM), F32),
            pltpu.VMEM((seq // GLA_CHUNK, G_VAL_DIM, G_KEY_DIM), BF16),
            pltpu.VMEM((seq, G_VAL_DIM), F32),
        ],
        compiler_params=_cparams(("arbitrary", "arbitrary")),
        name="gla_mixer",
    )(proj3, proj3, proj3, proj3, z3, g_onorm)


def _first_lane(mask, lane):
    return jnp.min(jnp.where(mask, lane, LANES), axis=-1, keepdims=True)


def _split_bf16(x):
    hi = x.astype(BF16)
    return hi, (x - hi.astype(F32)).astype(BF16)


def _outproj_router_kernel(oa_ref, ob_ref, x_ref, wa_ref, wb_ref, mod_ref, ln_ref, wr_ref, br_ref,
                           xo_ref, h_ref, ri_ref, rw_ref, cnt_ref, run_scr):
    tm = x_ref.shape[0]

    @pl.when(pl.program_id(0) == 0)
    def _():
        run_scr[...] = jnp.zeros_like(run_scr)

    mix = jnp.dot(oa_ref[...], wa_ref[...], preferred_element_type=F32)
    mix = mix + jnp.dot(ob_ref[...], wb_ref[...], preferred_element_type=F32)
    x_new = x_ref[...] + mod_ref[2:3, :] * mix
    xo_ref[...] = x_new
    h = _modulated_norm(x_new, ln_ref[...], mod_ref[3:4, :], mod_ref[4:5, :])
    h_ref[...] = _pack_bf16_pairs(h)

    hh = jnp.concatenate(_split_bf16(h), axis=0)
    r = jnp.dot(hh, wr_ref[...], preferred_element_type=F32)
    logits = (r[:tm, :LANES] + r[tm:, :LANES]) + (r[:tm, LANES:] + r[tm:, LANES:]) + br_ref[...]
    lane = lax.broadcasted_iota(jnp.int32, (tm, LANES), 1)
    is_grp = (lane >= N_EXPERTS) & (lane < N_EXPERTS + N_GROUPS)
    l1 = jnp.where(is_grp, logits, -jnp.inf)
    m1 = jnp.max(l1, axis=-1, keepdims=True)
    grp = _first_lane(l1 == m1, lane) - N_EXPERTS
    p_grp = 1.0 / jnp.sum(jnp.exp(l1 - m1), axis=-1, keepdims=True)
    in_grp = (lane < N_EXPERTS) & ((lane // EXPERTS_PER_GROUP) == grp)
    l2 = jnp.where(in_grp, logits, -jnp.inf)
    va = jnp.max(l2, axis=-1, keepdims=True)
    ia = _first_lane(l2 == va, lane)
    l2b = jnp.where(lane == ia, -jnp.inf, l2)
    vb = jnp.max(l2b, axis=-1, keepdims=True)
    ib = _first_lane(l2b == vb, lane)
    eb = jnp.exp(vb - va)
    wa = p_grp * (1.0 / (1.0 + eb))
    wb = p_grp * (eb / (1.0 + eb))

    oh_a = jnp.where(lane == ia, 1.0, 0.0)
    oh_b = jnp.where(lane == ib, 1.0, 0.0)
    cnt = oh_a + oh_b
    r_i = lax.broadcasted_iota(jnp.int32, (tm, tm), 0)
    c_i = lax.broadcasted_iota(jnp.int32, (tm, tm), 1)
    strict_lower = jnp.where(c_i < r_i, 1.0, 0.0).astype(BF16)
    before = jnp.dot(strict_lower, cnt.astype(BF16), preferred_element_type=F32) + run_scr[...]
    rank_a = jnp.sum(oh_a * before, axis=-1, keepdims=True)
    rank_b = jnp.sum(oh_b * before, axis=-1, keepdims=True)
    run_new = run_scr[...] + jnp.sum(cnt, axis=0, keepdims=True)
    run_scr[...] = run_new
    cnt_ref[...] = jnp.broadcast_to(run_new, cnt_ref.shape).astype(jnp.int32)

    ri = jnp.where(lane == 0, ia, 0) + jnp.where(lane == 1, ib, 0)
    ri = ri + jnp.where(lane == 2, rank_a.astype(jnp.int32), 0) + jnp.where(lane == 3, rank_b.astype(jnp.int32), 0)
    ri_ref[...] = ri
    rw_ref[...] = jnp.where(lane == 0, wa, 0.0) + jnp.where(lane == 1, wb, 0.0)


def _outproj_router(oa2, ob2, x2, w_oa, w_ob, mod, ln, w_r, b_r, seq):
    t, d = x2.shape
    tm = 256
    row = lambda w: pl.BlockSpec((tm, w), lambda m: (m, 0))
    full = lambda a, c: pl.BlockSpec((a, c), lambda m: (0, 0))
    return pl.pallas_call(
        _outproj_router_kernel,
        grid=(t // tm,),
        in_specs=[
            row(A_WIDTH), row(G_VAL_WIDTH), row(d),
            full(A_WIDTH, d), full(G_VAL_WIDTH, d),
            pl.BlockSpec((None, 6, d), lambda m: ((m * tm) // seq, 0, 0)),
            full(1, d), full(d, 2 * LANES), full(1, LANES),
        ],
        out_specs=[row(d), row(d // 2), row(LANES), row(LANES), full(8, LANES)],
        out_shape=[
            jax.ShapeDtypeStruct((t, d), F32),
            jax.ShapeDtypeStruct((t, d // 2), jnp.uint32),
            jax.ShapeDtypeStruct((t, LANES), jnp.int32),
            jax.ShapeDtypeStruct((t, LANES), F32),
            jax.ShapeDtypeStruct((8, LANES), jnp.int32),
        ],
        scratch_shapes=[pltpu.VMEM((1, LANES), F32)],
        compiler_params=_cparams(("arbitrary",)),
        name="outproj_router",
    )(oa2, ob2, x2, w_oa, w_ob, mod, ln, w_r, b_r)


def _dispatch_kernel(dest_ref, h_ref, xg_in, xg_out, sem):
    del xg_in
    tm = h_ref.shape[0]

    def issue(i, carry):
        for u in range(2):
            r = 2 * i + u
            for k in range(TOP_K):
                pltpu.make_async_copy(h_ref.at[pl.ds(r, 1), :],
                                      xg_out.at[pl.ds(dest_ref[0, k * tm + r], 1), :], sem).start(priority=u)
        return carry

    lax.fori_loop(0, tm // 2, issue, 0, unroll=4)
    done = xg_out.at[pl.ds(0, TOP_K * tm), :]
    pltpu.make_async_copy(done, done, sem).wait()


DISPATCH_TILE = 1024


def _dispatch(dest3, h2p, xg):
    t, dp = h2p.shape
    tm = dest3.shape[2] // TOP_K
    return pl.pallas_call(
        _dispatch_kernel,
        grid=(t // tm,),
        in_specs=[
            pl.BlockSpec((None, 1, TOP_K * tm), lambda m: (m, 0, 0), memory_space=pltpu.SMEM),
            pl.BlockSpec((tm, dp), lambda m: (m, 0)),
            pl.BlockSpec(memory_space=pl.ANY),
        ],
        out_specs=pl.BlockSpec(memory_space=pl.ANY),
        out_shape=jax.ShapeDtypeStruct(xg.shape, xg.dtype),
        input_output_aliases={2: 0},
        scratch_shapes=[pltpu.SemaphoreType.DMA(())],
        compiler_params=_cparams(("arbitrary",)),
        name="moe_dispatch",
    )(dest3, h2p, xg)


def _expert_kernel(cur_ref, nxt_ref, last_ref, nused_ref, x_ref, wg_ref, wu_ref, wd_ref,
                   y_ref, wg_res, wu_res, wd_res, xb_scr, acc_scr):
    del cur_ref, nxt_ref
    b, j = pl.program_id(0), pl.program_id(1)
    active = (b >= 1) & (b <= nused_ref[0])

    @pl.when(active & (j == 0))
    def _():
        lo, hi = _unpack_bf16_pairs(x_ref[...])
        half = lo.shape[1]
        xb_scr[:, :half] = lo.astype(BF16)
        xb_scr[:, half:] = hi.astype(BF16)

    for jj in range(EXPERT_HIDDEN_CHUNKS):
        @pl.when(active & (j == jj))
        def _(jj=jj):
            xb = xb_scr[...]
            g = jnp.dot(xb, wg_res[jj], preferred_element_type=F32)
            u = jnp.dot(xb, wu_res[jj], preferred_element_type=F32)
            a = ((g * jax.nn.sigmoid(g)) * u).astype(BF16)
            part = jnp.dot(a, wd_res[jj], preferred_element_type=F32)
            if jj == 0:
                acc_scr[...] = part
            elif jj < EXPERT_HIDDEN_CHUNKS - 1:
                acc_scr[...] += part
            else:
                y_ref[...] = _pack_bf16_pairs(acc_scr[...] + part)

    @pl.when((b > nused_ref[0]) & (j == 0))
    def _():
        y_ref[...] = jnp.zeros_like(y_ref)

    @pl.when(last_ref[b] == 1)
    def _():
        wg_res[j] = wg_ref[...].astype(BF16)
        wu_res[j] = wu_ref[...].astype(BF16)
        wd_res[j] = wd_ref[...].astype(BF16)


EXPERT_HIDDEN_CHUNKS = 2


def _expert_mlp(cur_e, nxt_e, last, n_used, xg, wg, wu, wd, layer):
    dp = xg.shape[1]
    nb = xg.shape[0] // ROW_BLOCK
    d, f = wg.shape[2], wg.shape[3]
    nj = EXPERT_HIDDEN_CHUNKS
    fc = f // nj

    def x_index(b, j, cur, nxt, lst, nu):
        return (jnp.clip(b - 1, 0, nu[0] - 1), 0)

    def y_index(b, j, cur, nxt, lst, nu):
        return (jnp.maximum(b - 1, 0), 0)

    def w_index(chunk_axis):
        def index(b, j, cur, nxt, lst, nu):
            e = jnp.where(lst[b] == 1, nxt[b], cur[b])
            jj = jnp.where(lst[b] == 1, j, nj - 1)
            return (layer, e, 0, jj) if chunk_axis == 2 else (layer, e, jj, 0)
        return index

    grid_spec = pltpu.PrefetchScalarGridSpec(
        num_scalar_prefetch=4,
        grid=(nb + 1, nj),
        in_specs=[
            pl.BlockSpec((ROW_BLOCK, dp), x_index),
            pl.BlockSpec((None, None, d, fc), w_index(2)),
            pl.BlockSpec((None, None, d, fc), w_index(2)),
            pl.BlockSpec((None, None, fc, d), w_index(1)),
        ],
        out_specs=pl.BlockSpec((ROW_BLOCK, dp), y_index),
        scratch_shapes=[
            pltpu.VMEM((nj, d, fc), BF16),
            pltpu.VMEM((nj, d, fc), BF16),
            pltpu.VMEM((nj, fc, d), BF16),
            pltpu.VMEM((ROW_BLOCK, d), BF16),
            pltpu.VMEM((ROW_BLOCK, d), F32),
        ],
    )
    return pl.pallas_call(
        _expert_kernel,
        grid_spec=grid_spec,
        out_shape=jax.ShapeDtypeStruct((nb * ROW_BLOCK, dp), jnp.uint32),
        compiler_params=_cparams(("arbitrary", "arbitrary"), EXPERT_VMEM_LIMIT),
        name="expert_mlp",
    )(cur_e, nxt_e, last, n_used, xg, wg, wu, wd)


def _wait_row_gather(src_hbm, dst_vmem, sem):
    pltpu.make_async_copy(src_hbm.at[pl.ds(0, dst_vmem.shape[0]), :], dst_vmem, sem).wait()


def _combine_kernel(dest_ref, y_hbm, rw_ref, x_ref, mod_ref, lnf_ref, o_ref, y_scr, sems, *, final):
    tm = x_ref.shape[0]
    i = pl.program_id(0)
    n_tiles = pl.num_programs(0) - COMBINE_LOOKAHEAD
    n_slots = COMBINE_LOOKAHEAD + 1

    def start_gather():
        slot = i % n_slots
        for r in range(2 * tm):
            pltpu.make_async_copy(y_hbm.at[pl.ds(dest_ref[0, r], 1), :],
                                  y_scr.at[slot, pl.ds(r, 1), :], sems.at[slot]).start(priority=r % 2)

    def combine(prefetch):
        slot = (i - COMBINE_LOOKAHEAD) % n_slots
        _wait_row_gather(y_hbm, y_scr.at[slot], sems.at[slot])
        rw = rw_ref[...]
        a_lo, a_hi = _unpack_bf16_pairs(y_scr[slot, 0:tm, :])
        b_lo, b_hi = _unpack_bf16_pairs(y_scr[slot, tm:2 * tm, :])
        y = jnp.concatenate([rw[:, 0:1] * a_lo + rw[:, 1:2] * b_lo,
                             rw[:, 0:1] * a_hi + rw[:, 1:2] * b_hi], axis=1)
        if prefetch:
            start_gather()
        x_new = x_ref[...] + mod_ref[5:6, :] * y
        if final:
            x_new = x_new * lax.rsqrt(jnp.mean(x_new * x_new, axis=-1, keepdims=True) + EPS) * lnf_ref[...]
        o_ref[...] = x_new

    pl.when(i < COMBINE_LOOKAHEAD)(start_gather)
    pl.when((i >= COMBINE_LOOKAHEAD) & (i < n_tiles))(functools.partial(combine, True))
    pl.when(i >= n_tiles)(functools.partial(combine, False))


COMBINE_LOOKAHEAD = 2


def _combine(dest3, yg, rw, x2, mod, ln_f, seq, final):
    t, d = x2.shape
    tm = dest3.shape[2] // 2
    nt = t // tm
    prev = lambda m: jnp.maximum(m - COMBINE_LOOKAHEAD, 0)
    return pl.pallas_call(
        functools.partial(_combine_kernel, final=final),
        grid=(nt + COMBINE_LOOKAHEAD,),
        in_specs=[
            pl.BlockSpec((None, 1, 2 * tm), lambda m: (jnp.minimum(m, nt - 1), 0, 0), memory_space=pltpu.SMEM),
            pl.BlockSpec(memory_space=pl.ANY),
            pl.BlockSpec((tm, LANES), lambda m: (prev(m), 0)),
            pl.BlockSpec((tm, d), lambda m: (prev(m), 0)),
            pl.BlockSpec((None, 6, d), lambda m: ((prev(m) * tm) // seq, 0, 0)),
            pl.BlockSpec((1, d), lambda m: (0, 0)),
        ],
        out_specs=pl.BlockSpec((tm, d), lambda m: (prev(m), 0)),
        out_shape=jax.ShapeDtypeStruct((t, d), F32),
        scratch_shapes=[pltpu.VMEM((COMBINE_LOOKAHEAD + 1, 2 * tm, yg.shape[1]), jnp.uint32),
                        pltpu.SemaphoreType.DMA((COMBINE_LOOKAHEAD + 1,))],
        compiler_params=_cparams(("arbitrary",)),
        name="moe_combine",
    )(dest3, yg, rw, x2, mod, ln_f)


def _routing_tables(ri, counts, t):
    n_blocks = (t * TOP_K) // ROW_BLOCK + N_EXPERTS
    cnt = counts[0, :N_EXPERTS]
    padded = (cnt + ROW_BLOCK - 1) // ROW_BLOCK * ROW_BLOCK
    ends = jnp.cumsum(padded)
    starts = ends - padded
    experts, ranks = ri[:, 0:TOP_K], ri[:, TOP_K:2 * TOP_K]
    dest = starts[experts] + ranks
    n_used = (ends[-1] // ROW_BLOCK).astype(jnp.int32)
    blocks = jnp.minimum(jnp.arange(n_blocks + 1, dtype=jnp.int32), n_used - 1) * ROW_BLOCK
    block_exp = jnp.sum((blocks[:, None] >= ends[None, :]).astype(jnp.int32), axis=1)
    block_exp = jnp.minimum(block_exp, N_EXPERTS - 1)
    cur_e = jnp.concatenate([block_exp[:1], block_exp[:-1]])
    nxt_e = block_exp
    last = (cur_e != nxt_e).astype(jnp.int32).at[0].set(1)
    return dest.astype(jnp.int32), cur_e, nxt_e, last, n_used.reshape(1)


def _tile_major(dest, tile):
    t = dest.shape[0]
    return dest.reshape(t // tile, tile, TOP_K).transpose(0, 2, 1).reshape(t // tile, 1, TOP_K * tile)


def kernel(x, c, ln1, ln2, w_ada, b_ada, w_in, w_gk, b_gk, g_onorm, w_out,
           w_r1, b_r1, w_r2, b_r2, w_e_gate, w_e_up, w_e_down, ln_f):
    b, seq, d = x.shape
    depth = w_ada.shape[0]
    t = b * seq
    mod_all = _adaln_mod(c, w_ada, b_ada).reshape(depth, b, 6, d)
    tables = _rope_tables(seq)
    x2 = x.reshape(t, d)
    tm_c = 256
    xg = jnp.zeros(((t * TOP_K // ROW_BLOCK + N_EXPERTS) * ROW_BLOCK, d // 2), jnp.uint32)
    for l in range(depth):
        mod = mod_all[l]
        w_main = w_in[l][:, :PROJ_MAIN].astype(BF16)
        w_ga = jnp.pad(w_in[l][:, PROJ_MAIN:], ((0, 0), (0, LANES - GATE_RANK))).astype(BF16)
        w_gk_p = jnp.pad(w_gk[l], ((0, LANES - GATE_RANK), (0, 0)))
        proj, z = _in_projection(x2, ln1[l][None, :], mod, w_main, w_ga, w_gk_p, b_gk[l][None, :], seq)
        proj3 = proj.reshape(b, seq, PROJ_MAIN)
        oa = _moba_attention(proj3, tables)
        ob = _gla_mixer(proj3, z.reshape(b, seq, G_KEY_WIDTH), g_onorm[l][None, :])
        w_o = w_out[l].astype(BF16)
        w_r = jnp.pad(jnp.concatenate([w_r2[l], w_r1[l]], axis=1), ((0, 0), (0, LANES - N_EXPERTS - N_GROUPS)))
        w_r_hi = w_r.astype(BF16)
        w_r = jnp.concatenate([w_r_hi, (w_r - w_r_hi.astype(F32)).astype(BF16)], axis=1)
        b_r = jnp.pad(jnp.concatenate([b_r2[l], b_r1[l]]), (0, LANES - N_EXPERTS - N_GROUPS))[None, :]
        x2, h2, ri, rw, counts = _outproj_router(
            oa.reshape(t, A_WIDTH), ob.reshape(t, G_VAL_WIDTH), x2, w_o[:A_WIDTH], w_o[A_WIDTH:],
            mod, ln2[l][None, :], w_r, b_r, seq)
        dest, cur_e, nxt_e, last, n_used = _routing_tables(ri, counts, t)
        xg = _dispatch(_tile_major(dest, DISPATCH_TILE), h2, xg)
        yg = _expert_mlp(cur_e, nxt_e, last, n_used, xg, w_e_gate, w_e_up, w_e_down, l)
        x2 = _combine(_tile_major(dest, tm_c), yg, rw, x2, mod, ln_f[None, :], seq, final=(l == depth - 1))
    return x2.reshape(b, seq, d)
```

```python
import functools

import jax
import jax.numpy as jnp
import numpy as np
from jax import lax
from jax.experimental import pallas as pl
from jax.experimental.pallas import tpu as pltpu

F32 = jnp.float32
BF16 = jnp.bfloat16
HIGHEST = lax.Precision.HIGHEST

HEAD_DIM = 128
A_HEADS = 8
A_WIDTH = A_HEADS * HEAD_DIM
MOBA_BLOCK = 256
MOBA_TOPK = 3
ROPE_THETA = 500000.0
ROPE_DIM = HEAD_DIM // 4
ATTN_SCALE = HEAD_DIM ** -0.5
G_HEADS = 4
G_VAL_DIM = 256
G_KEY_DIM = 128
G_KEY_WIDTH = G_HEADS * G_KEY_DIM
G_VAL_WIDTH = G_HEADS * G_VAL_DIM
GATE_RANK = 16
GATE_TAU = 16.0
GLA_CHUNK = 64
N_GROUPS = 4
EXPERTS_PER_GROUP = 8
N_EXPERTS = N_GROUPS * EXPERTS_PER_GROUP
TOP_K = 2
EPS = 1e-6

LANES = 128
PROJ_MAIN = 3 * A_WIDTH + 2 * G_KEY_WIDTH + 2 * G_VAL_WIDTH
ROW_BLOCK = 512
VMEM_LIMIT = 52 * 1024 * 1024
EXPERT_VMEM_LIMIT = 58 * 1024 * 1024


def _cparams(sem, vmem_limit=VMEM_LIMIT):
    return pltpu.CompilerParams(dimension_semantics=sem, vmem_limit_bytes=vmem_limit)


def _pack_bf16_pairs(x):
    c = x.shape[1] // 2
    lo = pltpu.bitcast(x[:, :c].astype(BF16).astype(F32), jnp.uint32)
    hi = pltpu.bitcast(x[:, c:].astype(BF16).astype(F32), jnp.uint32)
    return (lo >> 16) | hi


def _unpack_bf16_pairs(w):
    lo = pltpu.bitcast(w << 16, F32)
    hi = pltpu.bitcast(w & jnp.uint32(0xFFFF0000), F32)
    return lo, hi


def _mod_kernel(c_ref, w_ref, b_ref, o_ref):
    c = c_ref[...]
    ca = c * jax.nn.sigmoid(c)
    o_ref[...] = jnp.dot(ca, w_ref[...], precision=HIGHEST, preferred_element_type=F32) + b_ref[...]


def _adaln_mod(c, w_ada, b_ada):
    depth, d, n6 = w_ada.shape
    b = c.shape[0]
    tn = 1024
    return pl.pallas_call(
        _mod_kernel,
        grid=(depth, n6 // tn),
        in_specs=[
            pl.BlockSpec((b, d), lambda l, n: (0, 0)),
            pl.BlockSpec((None, d, tn), lambda l, n: (l, 0, n)),
            pl.BlockSpec((None, 1, tn), lambda l, n: (l, 0, n)),
        ],
        out_specs=pl.BlockSpec((None, b, tn), lambda l, n: (l, 0, n)),
        out_shape=jax.ShapeDtypeStruct((depth, b, n6), F32),
        compiler_params=_cparams(("arbitrary", "arbitrary")),
        name="adaln_mod",
    )(c, w_ada, b_ada.reshape(depth, 1, n6))


def _modulated_norm(x, g, shift, scale):
    y = x * lax.rsqrt(jnp.mean(x * x, axis=-1, keepdims=True) + EPS) * g
    return y * (1.0 + scale) + shift


def _inproj_kernel(x_ref, ln_ref, mod_ref, w_ref, wga_ref, wgk_ref, bgk_ref, proj_ref, z_ref, h_scr):
    @pl.when(pl.program_id(1) == 0)
    def _():
        h = _modulated_norm(x_ref[...], ln_ref[...], mod_ref[0:1, :], mod_ref[1:2, :])
        hb = h.astype(BF16)
        h_scr[...] = hb
        ga = jnp.dot(hb, wga_ref[...], preferred_element_type=F32)
        z_ref[...] = jnp.dot(ga, wgk_ref[...], precision=HIGHEST, preferred_element_type=F32) + bgk_ref[...]

    proj_ref[...] = jnp.dot(h_scr[...], w_ref[...], preferred_element_type=F32).astype(proj_ref.dtype)


def _in_projection(x2, ln, mod, w_main, w_ga, w_gk, b_gk, seq):
    t, d = x2.shape
    tm, tn = 1024, 1024
    return pl.pallas_call(
        _inproj_kernel,
        grid=(t // tm, PROJ_MAIN // tn),
        in_specs=[
            pl.BlockSpec((tm, d), lambda m, n: (m, 0)),
            pl.BlockSpec((1, d), lambda m, n: (0, 0)),
            pl.BlockSpec((None, 6, d), lambda m, n: ((m * tm) // seq, 0, 0)),
            pl.BlockSpec((d, tn), lambda m, n: (0, n)),
            pl.BlockSpec((d, LANES), lambda m, n: (0, 0)),
            pl.BlockSpec((LANES, G_KEY_WIDTH), lambda m, n: (0, 0)),
            pl.BlockSpec((1, G_KEY_WIDTH), lambda m, n: (0, 0)),
        ],
        out_specs=[
            pl.BlockSpec((tm, tn), lambda m, n: (m, n)),
            pl.BlockSpec((tm, G_KEY_WIDTH), lambda m, n: (m, 0)),
        ],
        out_shape=[
            jax.ShapeDtypeStruct((t, PROJ_MAIN), BF16),
            jax.ShapeDtypeStruct((t, G_KEY_WIDTH), F32),
        ],
        scratch_shapes=[pltpu.VMEM((tm, d), BF16)],
        compiler_params=_cparams(("arbitrary", "arbitrary")),
        name="in_projection",
    )(x2, ln, mod, w_main, w_ga, w_gk, b_gk)


def _rope(xb, cos, sin, swap):
    partner = jnp.dot(xb, swap, preferred_element_type=F32)
    return xb.astype(F32) * cos + partner * sin


LOG2E = 1.4426950408889634
DEN_ROWS = 16


def _moba_kernel(q_ref, k_ref, v_ref, cos_ref, sin_ref, swap_ref, o_ref, s_scr):
    seq = q_ref.shape[0]
    nblk = seq // MOBA_BLOCK
    cos, sin, swap = cos_ref[...], sin_ref[...], swap_ref[...]
    q = _rope(q_ref[...], cos, sin, swap) * (ATTN_SCALE * LOG2E)
    k = _rope(k_ref[...], cos, sin, swap)
    kmean = jnp.mean(k.reshape(nblk, MOBA_BLOCK, HEAD_DIM), axis=1)
    gate = lax.dot_general(kmean, q, (((1,), (1,)), ((), ())), precision=HIGHEST,
                           preferred_element_type=F32)
    blk_of_q = lax.broadcasted_iota(jnp.int32, (nblk, seq), 1) // MOBA_BLOCK
    row = lax.broadcasted_iota(jnp.int32, (nblk, seq), 0)
    past = row < blk_of_q
    better = jnp.zeros((nblk, seq), F32)
    for m in range(nblk):
        gm = gate[m:m + 1, :]
        past_m = blk_of_q[m:m + 1, :] > m
        beats = (gm > gate) | ((gm == gate) & (row > m))
        better = better + jnp.where(beats & past_m, 1.0, 0.0)
    sel = past & (better < float(MOBA_TOPK))

    qb = q.astype(BF16)
    kb = k.astype(BF16)
    vt = jnp.concatenate([v_ref[...].astype(F32).T, jnp.ones((DEN_ROWS, seq), F32)], axis=0).astype(BF16)
    key_i = lax.broadcasted_iota(jnp.int32, (MOBA_BLOCK, MOBA_BLOCK), 0)
    qry_i = lax.broadcasted_iota(jnp.int32, (MOBA_BLOCK, MOBA_BLOCK), 1)
    causal = key_i <= qry_i
    for i in range(nblk):
        qs = slice(i * MOBA_BLOCK, (i + 1) * MOBA_BLOCK)
        qi = qb[qs, :]
        mx = None
        for n in range(i + 1):
            ks = slice(n * MOBA_BLOCK, (n + 1) * MOBA_BLOCK)
            s = lax.dot_general(kb[ks, :], qi, (((1,), (1,)), ((), ())),
                                preferred_element_type=F32)
            keep = causal if n == i else sel[n:n + 1, qs]
            s = jnp.where(keep, s, -jnp.inf)
            s_scr[i % 2, n] = s
            bm = jnp.max(s, axis=0, keepdims=True)
            mx = bm if mx is None else jnp.maximum(mx, bm)
        acc = jnp.zeros((HEAD_DIM + DEN_ROWS, MOBA_BLOCK), F32)
        for n in range(i + 1):
            ks = slice(n * MOBA_BLOCK, (n + 1) * MOBA_BLOCK)
            p = jnp.exp2(s_scr[i % 2, n] - mx)
            acc = acc + jnp.dot(vt[:, ks], p.astype(BF16), preferred_element_type=F32)
        den = acc[HEAD_DIM:HEAD_DIM + 1, :]
        o_ref[qs, :] = (acc[:HEAD_DIM, :] / den).T.astype(o_ref.dtype)


def _rope_tables(seq):
    half = ROPE_DIM // 2
    inv = jnp.power(ROPE_THETA, -jnp.arange(half, dtype=F32) * (2.0 / ROPE_DIM))
    ang = jnp.arange(seq, dtype=F32)[:, None] * inv[None, :]
    cos, sin = jnp.cos(ang), jnp.sin(ang)
    ones = jnp.ones((seq, HEAD_DIM - ROPE_DIM), F32)
    zeros = jnp.zeros((seq, HEAD_DIM - half), F32)
    cos_t = jnp.concatenate([cos, cos, ones], axis=1)
    sin_t = jnp.concatenate([-sin, sin, zeros[:, half:]], axis=1)
    src = jnp.arange(HEAD_DIM)[:, None]
    dst = jnp.arange(HEAD_DIM)[None, :]
    swap = ((dst < ROPE_DIM) & (src == jnp.where(dst < half, dst + half, dst - half))).astype(BF16)
    return cos_t, sin_t, swap


def _moba_attention(proj3, tables):
    b, seq, _ = proj3.shape
    head = lambda off: pl.BlockSpec((None, seq, HEAD_DIM), lambda bi, h: (bi, 0, off + h))
    tab = pl.BlockSpec((seq, HEAD_DIM), lambda bi, h: (0, 0))
    return pl.pallas_call(
        _moba_kernel,
        grid=(b, A_HEADS),
        in_specs=[head(0), head(A_HEADS), head(2 * A_HEADS), tab, tab,
                  pl.BlockSpec((HEAD_DIM, HEAD_DIM), lambda bi, h: (0, 0))],
        out_specs=pl.BlockSpec((None, seq, HEAD_DIM), lambda bi, h: (bi, 0, h)),
        out_shape=jax.ShapeDtypeStruct((b, seq, A_WIDTH), BF16),
        scratch_shapes=[pltpu.VMEM((2, seq // MOBA_BLOCK, MOBA_BLOCK, MOBA_BLOCK), F32)],
        compiler_params=_cparams(("arbitrary", "arbitrary")),
        name="moba_attention",
    )(proj3, proj3, proj3, *tables)


def _log_sigmoid(z):
    return jnp.minimum(z, 0.0) - jnp.log1p(jnp.exp(-jnp.abs(z)))


def _chunk_cumsum(x):
    pos = lax.broadcasted_iota(jnp.int32, x.shape, 0) % GLA_CHUNK
    shift = 1
    while shift < GLA_CHUNK:
        x = x + jnp.where(pos >= shift, pltpu.roll(x, shift, 0), 0.0)
        shift *= 2
    return x


def _gla_kernel(q_ref, k_ref, v_ref, og_ref, z_ref, gn_ref, o_ref,
                qd_scr, ki_scr, ke_scr, dec_scr, kv_scr, st_scr, oi_scr):
    seq = q_ref.shape[0]
    nc = seq // GLA_CHUNK
    log_a = _log_sigmoid(z_ref[...]) / GATE_TAU
    b = _chunk_cumsum(log_a)
    b3 = b.reshape(nc, GLA_CHUNK, G_KEY_DIM)
    b_last = b3[:, GLA_CHUNK - 1:GLA_CHUNK, :]
    k = k_ref[...].astype(F32)
    qd_scr[...] = ((q_ref[...].astype(F32) * (G_KEY_DIM ** -0.5)) * jnp.exp(b)).astype(BF16)
    ki_scr[...] = (k * jnp.exp(-b)).astype(BF16)
    ke3 = k.reshape(nc, GLA_CHUNK, G_KEY_DIM) * jnp.exp(b_last - b3)
    ke_scr[...] = ke3.reshape(seq, G_KEY_DIM).astype(BF16)
    dec_scr[...] = jnp.exp(b_last)
    t_i = lax.broadcasted_iota(jnp.int32, (GLA_CHUNK, GLA_CHUNK), 0)
    s_i = lax.broadcasted_iota(jnp.int32, (GLA_CHUNK, GLA_CHUNK), 1)
    causal = s_i <= t_i
    gn = gn_ref[...]

    def chunk_rows(n):
        return pl.ds(pl.multiple_of(n * GLA_CHUNK, GLA_CHUNK), GLA_CHUNK)

    def intra(n, carry):
        rows = chunk_rows(n)
        vb = v_ref[rows, :].astype(BF16)
        att = lax.dot_general(qd_scr[rows, :], ki_scr[rows, :], (((1,), (1,)), ((), ())),
                              preferred_element_type=F32)
        att = jnp.where(causal, att, 0.0).astype(BF16)
        oi_scr[rows, :] = jnp.dot(att, vb, preferred_element_type=F32)
        kv_scr[n] = lax.dot_general(vb, ke_scr[rows, :], (((0,), (0,)), ((), ())),
                                    preferred_element_type=F32)
        return carry

    lax.fori_loop(0, nc, intra, 0, unroll=GLA_UNROLL)

    def scan(n, state):
        st_scr[n] = state.astype(BF16)
        return state * dec_scr[n] + kv_scr[n]

    lax.fori_loop(0, nc, scan, jnp.zeros((G_VAL_DIM, G_KEY_DIM), F32), unroll=GLA_UNROLL)

    def inter(n, carry):
        rows = chunk_rows(n)
        o = oi_scr[rows, :] + lax.dot_general(qd_scr[rows, :], st_scr[n], (((1,), (1,)), ((), ())),
                                              preferred_element_type=F32)
        o = o * lax.rsqrt(jnp.mean(o * o, axis=-1, keepdims=True) + EPS) * gn
        og = og_ref[rows, :].astype(F32)
        o_ref[rows, :] = (o * (og * jax.nn.sigmoid(og))).astype(o_ref.dtype)
        return carry

    lax.fori_loop(0, nc, inter, 0, unroll=GLA_UNROLL)


GLA_UNROLL = 16


def _gla_mixer(proj3, z3, g_onorm):
    b, seq, _ = proj3.shape
    kq = 3 * A_WIDTH // G_KEY_DIM
    kv = (3 * A_WIDTH + 2 * G_KEY_WIDTH) // G_VAL_DIM
    key = lambda off: pl.BlockSpec((None, seq, G_KEY_DIM), lambda bi, h: (bi, 0, off + h))
    val = lambda off: pl.BlockSpec((None, seq, G_VAL_DIM), lambda bi, h: (bi, 0, off + h))
    return pl.pallas_call(
        _gla_kernel,
        grid=(b, G_HEADS),
        in_specs=[
            key(kq), key(kq + G_HEADS), val(kv), val(kv + G_HEADS),
            pl.BlockSpec((None, seq, G_KEY_DIM), lambda bi, h: (bi, 0, h)),
            pl.BlockSpec((1, G_VAL_DIM), lambda bi, h: (0, 0)),
        ],
        out_specs=pl.BlockSpec((None, seq, G_VAL_DIM), lambda bi, h: (bi, 0, h)),
        out_shape=jax.ShapeDtypeStruct((b, seq, G_VAL_WIDTH), BF16),
        scratch_shapes=[
            pltpu.VMEM((seq, G_KEY_DIM), BF16),
            pltpu.VMEM((seq, G_KEY_DIM), BF16),
            pltpu.VMEM((seq, G_KEY_DIM), BF16),
            pltpu.VMEM((seq // GLA_CHUNK, 1, G_KEY_DIM), F32),
            pltpu.VMEM((seq // GLA_CHUNK, G_VAL_DIM, G_KEY_DIM), F32),
            pltpu.VMEM((seq // GLA_CHUNK, G_VAL_DIM, G_KEY_DIM), BF16),
            pltpu.VMEM((seq, G_VAL_DIM), F32),
        ],
        compiler_params=_cparams(("arbitrary", "arbitrary")),
        name="gla_mixer",
    )(proj3, proj3, proj3, proj3, z3, g_onorm)


def _first_lane(mask, lane):
    return jnp.min(jnp.where(mask, lane, LANES), axis=-1, keepdims=True)


def _split_bf16(x):
    hi = x.astype(BF16)
    return hi, (x - hi.astype(F32)).astype(BF16)


def _outproj_router_kernel(oa_ref, ob_ref, x_ref, wa_ref, wb_ref, mod_ref, ln_ref, wr_ref, br_ref,
                           xo_ref, h_ref, ri_ref, rw_ref, cnt_ref, run_scr):
    tm = x_ref.shape[0]

    @pl.when(pl.program_id(0) == 0)
    def _():
        run_scr[...] = jnp.zeros_like(run_scr)

    mix = jnp.dot(oa_ref[...], wa_ref[...], preferred_element_type=F32)
    mix = mix + jnp.dot(ob_ref[...], wb_ref[...], preferred_element_type=F32)
    x_new = x_ref[...] + mod_ref[2:3, :] * mix
    xo_ref[...] = x_new
    h = _modulated_norm(x_new, ln_ref[...], mod_ref[3:4, :], mod_ref[4:5, :])
    h_ref[...] = _pack_bf16_pairs(h)

    hh = jnp.concatenate(_split_bf16(h), axis=0)
    r = jnp.dot(hh, wr_ref[...], preferred_element_type=F32)
    logits = (r[:tm, :LANES] + r[tm:, :LANES]) + (r[:tm, LANES:] + r[tm:, LANES:]) + br_ref[...]
    lane = lax.broadcasted_iota(jnp.int32, (tm, LANES), 1)
    is_grp = (lane >= N_EXPERTS) & (lane < N_EXPERTS + N_GROUPS)
    l1 = jnp.where(is_grp, logits, -jnp.inf)
    m1 = jnp.max(l1, axis=-1, keepdims=True)
    grp = _first_lane(l1 == m1, lane) - N_EXPERTS
    p_grp = 1.0 / jnp.sum(jnp.exp(l1 - m1), axis=-1, keepdims=True)
    in_grp = (lane < N_EXPERTS) & ((lane // EXPERTS_PER_GROUP) == grp)
    l2 = jnp.where(in_grp, logits, -jnp.inf)
    va = jnp.max(l2, axis=-1, keepdims=True)
    ia = _first_lane(l2 == va, lane)
    l2b = jnp.where(lane == ia, -jnp.inf, l2)
    vb = jnp.max(l2b, axis=-1, keepdims=True)
    ib = _first_lane(l2b == vb, lane)
    eb = jnp.exp(vb - va)
    wa = p_grp * (1.0 / (1.0 + eb))
    wb = p_grp * (eb / (1.0 + eb))

    oh_a = jnp.where(lane == ia, 1.0, 0.0)
    oh_b = jnp.where(lane == ib, 1.0, 0.0)
    cnt = oh_a + oh_b
    r_i = lax.broadcasted_iota(jnp.int32, (tm, tm), 0)
    c_i = lax.broadcasted_iota(jnp.int32, (tm, tm), 1)
    strict_lower = jnp.where(c_i < r_i, 1.0, 0.0).astype(BF16)
    before = jnp.dot(strict_lower, cnt.astype(BF16), preferred_element_type=F32) + run_scr[...]
    rank_a = jnp.sum(oh_a * before, axis=-1, keepdims=True)
    rank_b = jnp.sum(oh_b * before, axis=-1, keepdims=True)
    run_new = run_scr[...] + jnp.sum(cnt, axis=0, keepdims=True)
    run_scr[...] = run_new
    cnt_ref[...] = jnp.broadcast_to(run_new, cnt_ref.shape).astype(jnp.int32)

    ri = jnp.where(lane == 0, ia, 0) + jnp.where(lane == 1, ib, 0)
    ri = ri + jnp.where(lane == 2, rank_a.astype(jnp.int32), 0) + jnp.where(lane == 3, rank_b.astype(jnp.int32), 0)
    ri_ref[...] = ri
    rw_ref[...] = jnp.where(lane == 0, wa, 0.0) + jnp.where(lane == 1, wb, 0.0)


def _outproj_router(oa2, ob2, x2, w_oa, w_ob, mod, ln, w_r, b_r, seq):
    t, d = x2.shape
    tm = 512
    row = lambda w: pl.BlockSpec((tm, w), lambda m: (m, 0))
    full = lambda a, c: pl.BlockSpec((a, c), lambda m: (0, 0))
    return pl.pallas_call(
        _outproj_router_kernel,
        grid=(t // tm,),
        in_specs=[
            row(A_WIDTH), row(G_VAL_WIDTH), row(d),
            full(A_WIDTH, d), full(G_VAL_WIDTH, d),
            pl.BlockSpec((None, 6, d), lambda m: ((m * tm) // seq, 0, 0)),
            full(1, d), full(d, 2 * LANES), full(1, LANES),
        ],
        out_specs=[row(d), row(d // 2), row(LANES), row(LANES), full(8, LANES)],
        out_shape=[
            jax.ShapeDtypeStruct((t, d), F32),
            jax.ShapeDtypeStruct((t, d // 2), jnp.uint32),
            jax.ShapeDtypeStruct((t, LANES), jnp.int32),
            jax.ShapeDtypeStruct((t, LANES), F32),
            jax.ShapeDtypeStruct((8, LANES), jnp.int32),
        ],
        scratch_shapes=[pltpu.VMEM((1, LANES), F32)],
        compiler_params=_cparams(("arbitrary",)),
        name="outproj_router",
    )(oa2, ob2, x2, w_oa, w_ob, mod, ln, w_r, b_r)


def _dispatch_kernel(dest_ref, h_ref, xg_in, xg_out, sem):
    del xg_in
    tm = h_ref.shape[0]

    for r in range(tm):
        for k in range(TOP_K):
            pltpu.make_async_copy(h_ref.at[pl.ds(r, 1), :],
                                  xg_out.at[pl.ds(dest_ref[0, k * tm + r], 1), :], sem).start(priority=r % 2)
    done = xg_out.at[pl.ds(0, TOP_K * tm), :]
    pltpu.make_async_copy(done, done, sem).wait()


DISPATCH_TILE = 1024


def _dispatch(dest3, h2p, xg):
    t, dp = h2p.shape
    tm = dest3.shape[2] // TOP_K
    return pl.pallas_call(
        _dispatch_kernel,
        grid=(t // tm,),
        in_specs=[
            pl.BlockSpec((None, 1, TOP_K * tm), lambda m: (m, 0, 0), memory_space=pltpu.SMEM),
            pl.BlockSpec((tm, dp), lambda m: (m, 0)),
            pl.BlockSpec(memory_space=pl.ANY),
        ],
        out_specs=pl.BlockSpec(memory_space=pl.ANY),
        out_shape=jax.ShapeDtypeStruct(xg.shape, xg.dtype),
        input_output_aliases={2: 0},
        scratch_shapes=[pltpu.SemaphoreType.DMA(())],
        compiler_params=_cparams(("arbitrary",)),
        name="moe_dispatch",
    )(dest3, h2p, xg)


def _expert_kernel(cur_ref, nxt_ref, last_ref, nused_ref, x_ref, wg_ref, wu_ref, wd_ref,
                   y_ref, wg_res, wu_res, wd_res, xb_scr, acc_scr):
    del cur_ref, nxt_ref
    b, j = pl.program_id(0), pl.program_id(1)
    active = (b >= 1) & (b <= nused_ref[0])

    @pl.when(active & (j == 0))
    def _():
        lo, hi = _unpack_bf16_pairs(x_ref[...])
        half = lo.shape[1]
        xb_scr[:, :half] = lo.astype(BF16)
        xb_scr[:, half:] = hi.astype(BF16)

    for jj in range(EXPERT_HIDDEN_CHUNKS):
        @pl.when(active & (j == jj))
        def _(jj=jj):
            xb = xb_scr[...]
            g = jnp.dot(xb, wg_res[jj], preferred_element_type=F32)
            u = jnp.dot(xb, wu_res[jj], preferred_element_type=F32)
            a = ((g * jax.nn.sigmoid(g)) * u).astype(BF16)
            part = jnp.dot(a, wd_res[jj], preferred_element_type=F32)
            if jj == 0:
                acc_scr[...] = part
            elif jj < EXPERT_HIDDEN_CHUNKS - 1:
                acc_scr[...] += part
            else:
                y_ref[...] = _pack_bf16_pairs(acc_scr[...] + part)

    @pl.when((b > nused_ref[0]) & (j == 0))
    def _():
        y_ref[...] = jnp.zeros_like(y_ref)

    @pl.when(last_ref[b] == 1)
    def _():
        wg_res[j] = wg_ref[...].astype(BF16)
        wu_res[j] = wu_ref[...].astype(BF16)
        wd_res[j] = wd_ref[...].astype(BF16)


EXPERT_HIDDEN_CHUNKS = 2


def _expert_mlp(cur_e, nxt_e, last, n_used, xg, wg, wu, wd, layer):
    dp = xg.shape[1]
    nb = xg.shape[0] // ROW_BLOCK
    d, f = wg.shape[2], wg.shape[3]
    nj = EXPERT_HIDDEN_CHUNKS
    fc = f // nj

    def x_index(b, j, cur, nxt, lst, nu):
        return (jnp.clip(b - 1, 0, nu[0] - 1), 0)

    def y_index(b, j, cur, nxt, lst, nu):
        return (jnp.maximum(b - 1, 0), 0)

    def w_index(chunk_axis):
        def index(b, j, cur, nxt, lst, nu):
            e = jnp.where(lst[b] == 1, nxt[b], cur[b])
            jj = jnp.where(lst[b] == 1, j, nj - 1)
            return (layer, e, 0, jj) if chunk_axis == 2 else (layer, e, jj, 0)
        return index

    grid_spec = pltpu.PrefetchScalarGridSpec(
        num_scalar_prefetch=4,
        grid=(nb + 1, nj),
        in_specs=[
            pl.BlockSpec((ROW_BLOCK, dp), x_index),
            pl.BlockSpec((None, None, d, fc), w_index(2)),
            pl.BlockSpec((None, None, d, fc), w_index(2)),
            pl.BlockSpec((None, None, fc, d), w_index(1)),
        ],
        out_specs=pl.BlockSpec((ROW_BLOCK, dp), y_index),
        scratch_shapes=[
            pltpu.VMEM((nj, d, fc), BF16),
            pltpu.VMEM((nj, d, fc), BF16),
            pltpu.VMEM((nj, fc, d), BF16),
            pltpu.VMEM((ROW_BLOCK, d), BF16),
            pltpu.VMEM((ROW_BLOCK, d), F32),
        ],
    )
    return pl.pallas_call(
        _expert_kernel,
        grid_spec=grid_spec,
        out_shape=jax.ShapeDtypeStruct((nb * ROW_BLOCK, dp), jnp.uint32),
        compiler_params=_cparams(("arbitrary", "arbitrary"), EXPERT_VMEM_LIMIT),
        name="expert_mlp",
    )(cur_e, nxt_e, last, n_used, xg, wg, wu, wd)


def _wait_row_gather(src_hbm, dst_vmem, sem):
    pltpu.make_async_copy(src_hbm.at[pl.ds(0, dst_vmem.shape[0]), :], dst_vmem, sem).wait()


def _combine_kernel(dest_ref, y_hbm, rw_ref, x_ref, mod_ref, lnf_ref, o_ref, y_scr, sems, *, final):
    tm = x_ref.shape[0]
    i = pl.program_id(0)
    n_tiles = pl.num_programs(0) - COMBINE_LOOKAHEAD
    n_slots = COMBINE_LOOKAHEAD + 1

    def start_gather():
        slot = i % n_slots
        for r in range(2 * tm):
            pltpu.make_async_copy(y_hbm.at[pl.ds(dest_ref[0, r], 1), :],
                                  y_scr.at[slot, pl.ds(r, 1), :], sems.at[slot]).start(priority=r % 2)

    def combine(prefetch):
        slot = (i - COMBINE_LOOKAHEAD) % n_slots
        _wait_row_gather(y_hbm, y_scr.at[slot], sems.at[slot])
        rw = rw_ref[...]
        a_lo, a_hi = _unpack_bf16_pairs(y_scr[slot, 0:tm, :])
        b_lo, b_hi = _unpack_bf16_pairs(y_scr[slot, tm:2 * tm, :])
        y = jnp.concatenate([rw[:, 0:1] * a_lo + rw[:, 1:2] * b_lo,
                             rw[:, 0:1] * a_hi + rw[:, 1:2] * b_hi], axis=1)
        if prefetch:
            start_gather()
        x_new = x_ref[...] + mod_ref[5:6, :] * y
        if final:
            x_new = x_new * lax.rsqrt(jnp.mean(x_new * x_new, axis=-1, keepdims=True) + EPS) * lnf_ref[...]
        o_ref[...] = x_new

    pl.when(i < COMBINE_LOOKAHEAD)(start_gather)
    pl.when((i >= COMBINE_LOOKAHEAD) & (i < n_tiles))(functools.partial(combine, True))
    pl.when(i >= n_tiles)(functools.partial(combine, False))


COMBINE_LOOKAHEAD = 2


def _combine(dest3, yg, rw, x2, mod, ln_f, seq, final):
    t, d = x2.shape
    tm = dest3.shape[2] // 2
    nt = t // tm
    prev = lambda m: jnp.maximum(m - COMBINE_LOOKAHEAD, 0)
    return pl.pallas_call(
        functools.partial(_combine_kernel, final=final),
        grid=(nt + COMBINE_LOOKAHEAD,),
        in_specs=[
            pl.BlockSpec((None, 1, 2 * tm), lambda m: (jnp.minimum(m, nt - 1), 0, 0), memory_space=pltpu.SMEM),
            pl.BlockSpec(memory_space=pl.ANY),
            pl.BlockSpec((tm, LANES), lambda m: (prev(m), 0)),
            pl.BlockSpec((tm, d), lambda m: (prev(m), 0)),
            pl.BlockSpec((None, 6, d), lambda m: ((prev(m) * tm) // seq, 0, 0)),
            pl.BlockSpec((1, d), lambda m: (0, 0)),
        ],
        out_specs=pl.BlockSpec((tm, d), lambda m: (prev(m), 0)),
        out_shape=jax.ShapeDtypeStruct((t, d), F32),
        scratch_shapes=[pltpu.VMEM((COMBINE_LOOKAHEAD + 1, 2 * tm, yg.shape[1]), jnp.uint32),
                        pltpu.SemaphoreType.DMA((COMBINE_LOOKAHEAD + 1,))],
        compiler_params=_cparams(("arbitrary",)),
        name="moe_combine",
    )(dest3, yg, rw, x2, mod, ln_f)


def _routing_tables(ri, counts, t):
    n_blocks = (t * TOP_K) // ROW_BLOCK + N_EXPERTS
    cnt = counts[0, :N_EXPERTS]
    padded = (cnt + ROW_BLOCK - 1) // ROW_BLOCK * ROW_BLOCK
    ends = jnp.cumsum(padded)
    starts = ends - padded
    experts, ranks = ri[:, 0:TOP_K], ri[:, TOP_K:2 * TOP_K]
    dest = starts[experts] + ranks
    n_used = (ends[-1] // ROW_BLOCK).astype(jnp.int32)
    blocks = jnp.minimum(jnp.arange(n_blocks + 1, dtype=jnp.int32), n_used - 1) * ROW_BLOCK
    block_exp = jnp.sum((blocks[:, None] >= ends[None, :]).astype(jnp.int32), axis=1)
    block_exp = jnp.minimum(block_exp, N_EXPERTS - 1)
    cur_e = jnp.concatenate([block_exp[:1], block_exp[:-1]])
    nxt_e = block_exp
    last = (cur_e != nxt_e).astype(jnp.int32).at[0].set(1)
    return dest.astype(jnp.int32), cur_e, nxt_e, last, n_used.reshape(1)


def _tile_major(dest, tile):
    t = dest.shape[0]
    return dest.reshape(t // tile, tile, TOP_K).transpose(0, 2, 1).reshape(t // tile, 1, TOP_K * tile)


def kernel(x, c, ln1, ln2, w_ada, b_ada, w_in, w_gk, b_gk, g_onorm, w_out,
           w_r1, b_r1, w_r2, b_r2, w_e_gate, w_e_up, w_e_down, ln_f):
    b, seq, d = x.shape
    depth = w_ada.shape[0]
    t = b * seq
    mod_all = _adaln_mod(c, w_ada, b_ada).reshape(depth, b, 6, d)
    tables = _rope_tables(seq)
    x2 = x.reshape(t, d)
    tm_c = 256
    xg = jnp.zeros(((t * TOP_K // ROW_BLOCK + N_EXPERTS) * ROW_BLOCK, d // 2), jnp.uint32)
    for l in range(depth):
        mod = mod_all[l]
        w_main = w_in[l][:, :PROJ_MAIN].astype(BF16)
        w_ga = jnp.pad(w_in[l][:, PROJ_MAIN:], ((0, 0), (0, LANES - GATE_RANK))).astype(BF16)
        w_gk_p = jnp.pad(w_gk[l], ((0, LANES - GATE_RANK), (0, 0)))
        proj, z = _in_projection(x2, ln1[l][None, :], mod, w_main, w_ga, w_gk_p, b_gk[l][None, :], seq)
        proj3 = proj.reshape(b, seq, PROJ_MAIN)
        oa = _moba_attention(proj3, tables)
        ob = _gla_mixer(proj3, z.reshape(b, seq, G_KEY_WIDTH), g_onorm[l][None, :])
        w_o = w_out[l].astype(BF16)
        w_r = jnp.pad(jnp.concatenate([w_r2[l], w_r1[l]], axis=1), ((0, 0), (0, LANES - N_EXPERTS - N_GROUPS)))
        w_r_hi = w_r.astype(BF16)
        w_r = jnp.concatenate([w_r_hi, (w_r - w_r_hi.astype(F32)).astype(BF16)], axis=1)
        b_r = jnp.pad(jnp.concatenate([b_r2[l], b_r1[l]]), (0, LANES - N_EXPERTS - N_GROUPS))[None, :]
        x2, h2, ri, rw, counts = _outproj_router(
            oa.reshape(t, A_WIDTH), ob.reshape(t, G_VAL_WIDTH), x2, w_o[:A_WIDTH], w_o[A_WIDTH:],
            mod, ln2[l][None, :], w_r, b_r, seq)
        dest, cur_e, nxt_e, last, n_used = _routing_tables(ri, counts, t)
        xg = _dispatch(_tile_major(dest, DISPATCH_TILE), h2, xg)
        yg = _expert_mlp(cur_e, nxt_e, last, n_used, xg, w_e_gate, w_e_up, w_e_down, l)
        x2 = _combine(_tile_major(dest, tm_c), yg, rw, x2, mod, ln_f[None, :], seq, final=(l == depth - 1))
    return x2.reshape(b, seq, d)
```

```python
import functools

import jax
import jax.numpy as jnp
import numpy as np
from jax import lax
from jax.experimental import pallas as pl
from jax.experimental.pallas import tpu as pltpu

F32 = jnp.float32
BF16 = jnp.bfloat16
HIGHEST = lax.Precision.HIGHEST

HEAD_DIM = 128
A_HEADS = 8
A_WIDTH = A_HEADS * HEAD_DIM
MOBA_BLOCK = 256
MOBA_TOPK = 3
ROPE_THETA = 500000.0
ROPE_DIM = HEAD_DIM // 4
ATTN_SCALE = HEAD_DIM ** -0.5
G_HEADS = 4
G_VAL_DIM = 256
G_KEY_DIM = 128
G_KEY_WIDTH = G_HEADS * G_KEY_DIM
G_VAL_WIDTH = G_HEADS * G_VAL_DIM
GATE_RANK = 16
GATE_TAU = 16.0
GLA_CHUNK = 64
N_GROUPS = 4
EXPERTS_PER_GROUP = 8
N_EXPERTS = N_GROUPS * EXPERTS_PER_GROUP
TOP_K = 2
EPS = 1e-6

LANES = 128
PROJ_MAIN = 3 * A_WIDTH + 2 * G_KEY_WIDTH + 2 * G_VAL_WIDTH
ROW_BLOCK = 512
VMEM_LIMIT = 52 * 1024 * 1024
EXPERT_VMEM_LIMIT = 58 * 1024 * 1024


def _cparams(sem, vmem_limit=VMEM_LIMIT, flags=None):
    return pltpu.CompilerParams(dimension_semantics=sem, vmem_limit_bytes=vmem_limit, flags=flags)


def _pack_bf16_pairs(x):
    c = x.shape[1] // 2
    lo = pltpu.bitcast(x[:, :c].astype(BF16).astype(F32), jnp.uint32)
    hi = pltpu.bitcast(x[:, c:].astype(BF16).astype(F32), jnp.uint32)
    return (lo >> 16) | hi


def _unpack_bf16_pairs(w):
    lo = pltpu.bitcast(w << 16, F32)
    hi = pltpu.bitcast(w & jnp.uint32(0xFFFF0000), F32)
    return lo, hi


def _mod_kernel(c_ref, w_ref, b_ref, o_ref):
    c = c_ref[...]
    nb = c.shape[0]
    cc = jnp.concatenate(_split_bf16(c * jax.nn.sigmoid(c)), axis=0)
    w_hi, w_lo = _split_bf16(w_ref[...])
    r = jnp.dot(cc, w_hi, preferred_element_type=F32) + jnp.dot(cc, w_lo, preferred_element_type=F32)
    o_ref[...] = (r[:nb] + r[nb:]) + b_ref[...]


def _adaln_mod(c, w_ada, b_ada):
    depth, d, n6 = w_ada.shape
    b = c.shape[0]
    tn = 1024
    return pl.pallas_call(
        _mod_kernel,
        grid=(depth, n6 // tn),
        in_specs=[
            pl.BlockSpec((b, d), lambda l, n: (0, 0)),
            pl.BlockSpec((None, d, tn), lambda l, n: (l, 0, n)),
            pl.BlockSpec((None, 1, tn), lambda l, n: (l, 0, n)),
        ],
        out_specs=pl.BlockSpec((None, b, tn), lambda l, n: (l, 0, n)),
        out_shape=jax.ShapeDtypeStruct((depth, b, n6), F32),
        compiler_params=_cparams(("arbitrary", "arbitrary")),
        name="adaln_mod",
    )(c, w_ada, b_ada.reshape(depth, 1, n6))


def _modulated_norm(x, g, shift, scale):
    y = x * lax.rsqrt(jnp.mean(x * x, axis=-1, keepdims=True) + EPS) * g
    return y * (1.0 + scale) + shift


def _inproj_kernel(x_ref, ln_ref, mod_ref, w_ref, wga_ref, wgk_ref, bgk_ref, proj_ref, z_ref, h_scr):
    @pl.when(pl.program_id(1) == 0)
    def _():
        h = _modulated_norm(x_ref[...], ln_ref[...], mod_ref[0:1, :], mod_ref[1:2, :])
        hb = h.astype(BF16)
        h_scr[...] = hb
        ga = jnp.dot(hb, wga_ref[...], preferred_element_type=F32)
        z_ref[...] = jnp.dot(ga, wgk_ref[...], precision=HIGHEST, preferred_element_type=F32) + bgk_ref[...]

    proj_ref[...] = jnp.dot(h_scr[...], w_ref[...], preferred_element_type=F32).astype(proj_ref.dtype)


def _in_projection(x2, ln, mod, w_main, w_ga, w_gk, b_gk, seq):
    t, d = x2.shape
    tm, tn = 1024, 1024
    return pl.pallas_call(
        _inproj_kernel,
        grid=(t // tm, PROJ_MAIN // tn),
        in_specs=[
            pl.BlockSpec((tm, d), lambda m, n: (m, 0)),
            pl.BlockSpec((1, d), lambda m, n: (0, 0)),
            pl.BlockSpec((None, 6, d), lambda m, n: ((m * tm) // seq, 0, 0)),
            pl.BlockSpec((d, tn), lambda m, n: (0, n)),
            pl.BlockSpec((d, LANES), lambda m, n: (0, 0)),
            pl.BlockSpec((LANES, G_KEY_WIDTH), lambda m, n: (0, 0)),
            pl.BlockSpec((1, G_KEY_WIDTH), lambda m, n: (0, 0)),
        ],
        out_specs=[
            pl.BlockSpec((tm, tn), lambda m, n: (m, n)),
            pl.BlockSpec((tm, G_KEY_WIDTH), lambda m, n: (m, 0)),
        ],
        out_shape=[
            jax.ShapeDtypeStruct((t, PROJ_MAIN), BF16),
            jax.ShapeDtypeStruct((t, G_KEY_WIDTH), F32),
        ],
        scratch_shapes=[pltpu.VMEM((tm, d), BF16)],
        compiler_params=_cparams(("arbitrary", "arbitrary")),
        name="in_projection",
    )(x2, ln, mod, w_main, w_ga, w_gk, b_gk)


def _rope(xb, cos, sin, swap):
    partner = jnp.dot(xb, swap, preferred_element_type=F32)
    return xb.astype(F32) * cos + partner * sin


LOG2E = 1.4426950408889634
DEN_ROWS = 16


def _moba_kernel(q_ref, k_ref, v_ref, cos_ref, sin_ref, swap_ref, o_ref, s_scr):
    seq = q_ref.shape[0]
    nblk = seq // MOBA_BLOCK
    cos, sin, swap = cos_ref[...], sin_ref[...], swap_ref[...]
    q = _rope(q_ref[...], cos, sin, swap) * (ATTN_SCALE * LOG2E)
    k = _rope(k_ref[...], cos, sin, swap)
    kmean = jnp.mean(k.reshape(nblk, MOBA_BLOCK, HEAD_DIM), axis=1)
    gate = lax.dot_general(kmean, q, (((1,), (1,)), ((), ())), precision=HIGHEST,
                           preferred_element_type=F32)
    blk_of_q = lax.broadcasted_iota(jnp.int32, (nblk, seq), 1) // MOBA_BLOCK
    row = lax.broadcasted_iota(jnp.int32, (nblk, seq), 0)
    past = row < blk_of_q
    better = jnp.zeros((nblk, seq), F32)
    for m in range(nblk):
        gm = gate[m:m + 1, :]
        past_m = blk_of_q[m:m + 1, :] > m
        beats = (gm > gate) | ((gm == gate) & (row > m))
        better = better + jnp.where(beats & past_m, 1.0, 0.0)
    sel = past & (better < float(MOBA_TOPK))

    qb = q.astype(BF16)
    kb = k.astype(BF16)
    vt = jnp.concatenate([v_ref[...].astype(F32).T, jnp.ones((DEN_ROWS, seq), F32)], axis=0).astype(BF16)
    key_i = lax.broadcasted_iota(jnp.int32, (MOBA_BLOCK, MOBA_BLOCK), 0)
    qry_i = lax.broadcasted_iota(jnp.int32, (MOBA_BLOCK, MOBA_BLOCK), 1)
    causal = key_i <= qry_i
    def blk(j):
        return slice(j * MOBA_BLOCK, (j + 1) * MOBA_BLOCK)

    def score_pass(i):
        mx = None
        for n in range(i + 1):
            s = lax.dot_general(kb[blk(n), :], qb[blk(i), :], (((1,), (1,)), ((), ())),
                                preferred_element_type=F32)
            s = jnp.where(causal if n == i else sel[n:n + 1, blk(i)], s, -jnp.inf)
            s_scr[i % 2, n] = s
            bm = jnp.max(s, axis=0, keepdims=True)
            mx = bm if mx is None else jnp.maximum(mx, bm)
            yield mx

    def value_pass(i, mx):
        acc = jnp.zeros((HEAD_DIM + DEN_ROWS, MOBA_BLOCK), F32)
        for n in range(i + 1):
            p = jnp.exp2(s_scr[i % 2, n] - mx)
            acc = acc + jnp.dot(vt[:, blk(n)], p.astype(BF16), preferred_element_type=F32)
            yield None
        den = acc[HEAD_DIM:HEAD_DIM + 1, :]
        o_ref[blk(i), :] = (acc[:HEAD_DIM, :] / den).T.astype(o_ref.dtype)
        yield None

    mx = list(score_pass(0))[-1]
    for i in range(nblk):
        values = value_pass(i, mx)
        scores = score_pass(i + 1) if i + 1 < nblk else iter(())
        for mx_next in scores:
            mx = mx_next
            next(values, None)
        for _ in values:
            pass


def _rope_tables(seq):
    half = ROPE_DIM // 2
    inv = jnp.power(ROPE_THETA, -jnp.arange(half, dtype=F32) * (2.0 / ROPE_DIM))
    ang = jnp.arange(seq, dtype=F32)[:, None] * inv[None, :]
    cos, sin = jnp.cos(ang), jnp.sin(ang)
    ones = jnp.ones((seq, HEAD_DIM - ROPE_DIM), F32)
    zeros = jnp.zeros((seq, HEAD_DIM - half), F32)
    cos_t = jnp.concatenate([cos, cos, ones], axis=1)
    sin_t = jnp.concatenate([-sin, sin, zeros[:, half:]], axis=1)
    src = jnp.arange(HEAD_DIM)[:, None]
    dst = jnp.arange(HEAD_DIM)[None, :]
    swap = ((dst < ROPE_DIM) & (src == jnp.where(dst < half, dst + half, dst - half))).astype(BF16)
    return cos_t, sin_t, swap


def _moba_attention(proj3, tables):
    b, seq, _ = proj3.shape
    head = lambda off: pl.BlockSpec((None, seq, HEAD_DIM), lambda bi, h: (bi, 0, off + h))
    tab = pl.BlockSpec((seq, HEAD_DIM), lambda bi, h: (0, 0))
    return pl.pallas_call(
        _moba_kernel,
        grid=(b, A_HEADS),
        in_specs=[head(0), head(A_HEADS), head(2 * A_HEADS), tab, tab,
                  pl.BlockSpec((HEAD_DIM, HEAD_DIM), lambda bi, h: (0, 0))],
        out_specs=pl.BlockSpec((None, seq, HEAD_DIM), lambda bi, h: (bi, 0, h)),
        out_shape=jax.ShapeDtypeStruct((b, seq, A_WIDTH), BF16),
        scratch_shapes=[pltpu.VMEM((2, seq // MOBA_BLOCK, MOBA_BLOCK, MOBA_BLOCK), F32)],
        compiler_params=_cparams(("arbitrary", "arbitrary")),
        name="moba_attention",
    )(proj3, proj3, proj3, *tables)


def _log_sigmoid(z):
    return jnp.minimum(z, 0.0) - jnp.log1p(jnp.exp(-jnp.abs(z)))


def _chunk_cumsum(x):
    pos = lax.broadcasted_iota(jnp.int32, x.shape, 0) % GLA_CHUNK
    shift = 1
    while shift < GLA_CHUNK:
        x = x + jnp.where(pos >= shift, pltpu.roll(x, shift, 0), 0.0)
        shift *= 2
    return x


def _gla_kernel(q_ref, k_ref, v_ref, og_ref, z_ref, gn_ref, o_ref,
                qd_scr, ki_scr, ke_scr, dec_scr, kv_scr, st_scr, oi_scr):
    seq = q_ref.shape[0]
    nc = seq // GLA_CHUNK
    log_a = _log_sigmoid(z_ref[...]) / GATE_TAU
    b = _chunk_cumsum(log_a)
    b3 = b.reshape(nc, GLA_CHUNK, G_KEY_DIM)
    b_last = b3[:, GLA_CHUNK - 1:GLA_CHUNK, :]
    k = k_ref[...].astype(F32)
    qd_scr[...] = ((q_ref[...].astype(F32) * (G_KEY_DIM ** -0.5)) * jnp.exp(b)).astype(BF16)
    ki_scr[...] = (k * jnp.exp(-b)).astype(BF16)
    ke3 = k.reshape(nc, GLA_CHUNK, G_KEY_DIM) * jnp.exp(b_last - b3)
    ke_scr[...] = ke3.reshape(seq, G_KEY_DIM).astype(BF16)
    dec_scr[...] = jnp.exp(b_last)
    t_i = lax.broadcasted_iota(jnp.int32, (GLA_CHUNK, GLA_CHUNK), 0)
    s_i = lax.broadcasted_iota(jnp.int32, (GLA_CHUNK, GLA_CHUNK), 1)
    causal = s_i <= t_i
    gn = gn_ref[...]

    def chunk_rows(n):
        return pl.ds(pl.multiple_of(n * GLA_CHUNK, GLA_CHUNK), GLA_CHUNK)

    def intra(n, carry):
        rows = chunk_rows(n)
        vb = v_ref[rows, :].astype(BF16)
        att = lax.dot_general(qd_scr[rows, :], ki_scr[rows, :], (((1,), (1,)), ((), ())),
                              preferred_element_type=F32)
        att = jnp.where(causal, att, 0.0).astype(BF16)
        oi_scr[rows, :] = jnp.dot(att, vb, preferred_element_type=F32)
        kv_scr[n] = lax.dot_general(vb, ke_scr[rows, :], (((0,), (0,)), ((), ())),
                                    preferred_element_type=F32)
        return carry

    lax.fori_loop(0, nc, intra, 0, unroll=GLA_UNROLL)

    def scan(n, state):
        st_scr[n] = state.astype(BF16)
        return state * dec_scr[n] + kv_scr[n]

    lax.fori_loop(0, nc, scan, jnp.zeros((G_VAL_DIM, G_KEY_DIM), F32), unroll=GLA_UNROLL)

    def inter(n, carry):
        rows = chunk_rows(n)
        o = oi_scr[rows, :] + lax.dot_general(qd_scr[rows, :], st_scr[n], (((1,), (1,)), ((), ())),
                                              preferred_element_type=F32)
        o = o * lax.rsqrt(jnp.mean(o * o, axis=-1, keepdims=True) + EPS) * gn
        og = og_ref[rows, :].astype(F32)
        o_ref[rows, :] = (o * (og * jax.nn.sigmoid(og))).astype(o_ref.dtype)
        return carry

    lax.fori_loop(0, nc, inter, 0, unroll=GLA_UNROLL)


GLA_UNROLL = 16


def _gla_mixer(proj3, z3, g_onorm):
    b, seq, _ = proj3.shape
    kq = 3 * A_WIDTH // G_KEY_DIM
    kv = (3 * A_WIDTH + 2 * G_KEY_WIDTH) // G_VAL_DIM
    key = lambda off: pl.BlockSpec((None, seq, G_KEY_DIM), lambda bi, h: (bi, 0, off + h))
    val = lambda off: pl.BlockSpec((None, seq, G_VAL_DIM), lambda bi, h: (bi, 0, off + h))
    return pl.pallas_call(
        _gla_kernel,
        grid=(b, G_HEADS),
        in_specs=[
            key(kq), key(kq + G_HEADS), val(kv), val(kv + G_HEADS),
            pl.BlockSpec((None, seq, G_KEY_DIM), lambda bi, h: (bi, 0, h)),
            pl.BlockSpec((1, G_VAL_DIM), lambda bi, h: (0, 0)),
        ],
        out_specs=pl.BlockSpec((None, seq, G_VAL_DIM), lambda bi, h: (bi, 0, h)),
        out_shape=jax.ShapeDtypeStruct((b, seq, G_VAL_WIDTH), BF16),
        scratch_shapes=[
            pltpu.VMEM((seq, G_KEY_DIM), BF16),
            pltpu.VMEM((seq, G_KEY_DIM), BF16),
            pltpu.VMEM((seq, G_KEY_DIM), BF16),
            pltpu.VMEM((seq // GLA_CHUNK, 1, G_KEY_DIM), F32),
            pltpu.VMEM((seq // GLA_CHUNK, G_VAL_DIM, G_KEY_DIM), F32),
            pltpu.VMEM((seq // GLA_CHUNK, G_VAL_DIM, G_KEY_DIM), BF16),
            pltpu.VMEM((seq, G_VAL_DIM), F32),
        ],
        compiler_params=_cparams(("arbitrary", "arbitrary")),
        name="gla_mixer",
    )(proj3, proj3, proj3, proj3, z3, g_onorm)


def _first_lane(mask, lane):
    return jnp.min(jnp.where(mask, lane, LANES), axis=-1, keepdims=True)


def _split_bf16(x):
    hi = x.astype(BF16)
    return hi, (x - hi.astype(F32)).astype(BF16)


def _outproj_router_kernel(oa_ref, ob_ref, x_ref, wa_ref, wb_ref, mod_ref, ln_ref, wr_ref, br_ref,
                           xo_ref, h_ref, ri_ref, rw_ref, cnt_ref, run_scr):
    tm = x_ref.shape[0]

    @pl.when(pl.program_id(0) == 0)
    def _():
        run_scr[...] = jnp.zeros_like(run_scr)

    mix = jnp.dot(oa_ref[...], wa_ref[...], preferred_element_type=F32)
    mix = mix + jnp.dot(ob_ref[...], wb_ref[...], preferred_element_type=F32)
    x_new = x_ref[...] + mod_ref[2:3, :] * mix
    xo_ref[...] = x_new
    h = _modulated_norm(x_new, ln_ref[...], mod_ref[3:4, :], mod_ref[4:5, :])
    h_ref[...] = _pack_bf16_pairs(h)

    hh = jnp.concatenate(_split_bf16(h), axis=0)
    r = jnp.dot(hh, wr_ref[...], preferred_element_type=F32)
    logits = (r[:tm, :LANES] + r[tm:, :LANES]) + (r[:tm, LANES:] + r[tm:, LANES:]) + br_ref[...]
    lane = lax.broadcasted_iota(jnp.int32, (tm, LANES), 1)
    is_grp = (lane >= N_EXPERTS) & (lane < N_EXPERTS + N_GROUPS)
    l1 = jnp.where(is_grp, logits, -jnp.inf)
    m1 = jnp.max(l1, axis=-1, keepdims=True)
    grp = _first_lane(l1 == m1, lane) - N_EXPERTS
    p_grp = 1.0 / jnp.sum(jnp.exp(l1 - m1), axis=-1, keepdims=True)
    in_grp = (lane < N_EXPERTS) & ((lane // EXPERTS_PER_GROUP) == grp)
    l2 = jnp.where(in_grp, logits, -jnp.inf)
    va = jnp.max(l2, axis=-1, keepdims=True)
    ia = _first_lane(l2 == va, lane)
    l2b = jnp.where(lane == ia, -jnp.inf, l2)
    vb = jnp.max(l2b, axis=-1, keepdims=True)
    ib = _first_lane(l2b == vb, lane)
    eb = jnp.exp(vb - va)
    wa = p_grp * (1.0 / (1.0 + eb))
    wb = p_grp * (eb / (1.0 + eb))

    oh_a = jnp.where(lane == ia, 1.0, 0.0)
    oh_b = jnp.where(lane == ib, 1.0, 0.0)
    cnt = oh_a + oh_b
    r_i = lax.broadcasted_iota(jnp.int32, (tm, tm), 0)
    c_i = lax.broadcasted_iota(jnp.int32, (tm, tm), 1)
    strict_lower = jnp.where(c_i < r_i, 1.0, 0.0).astype(BF16)
    before = jnp.dot(strict_lower, cnt.astype(BF16), preferred_element_type=F32) + run_scr[...]
    rank_a = jnp.sum(oh_a * before, axis=-1, keepdims=True)
    rank_b = jnp.sum(oh_b * before, axis=-1, keepdims=True)
    run_new = run_scr[...] + jnp.sum(cnt, axis=0, keepdims=True)
    run_scr[...] = run_new
    cnt_ref[...] = jnp.broadcast_to(run_new, cnt_ref.shape).astype(jnp.int32)

    ri = jnp.where(lane == 0, ia.astype(F32), 0.0) + jnp.where(lane == 1, ib.astype(F32), 0.0)
    ri = ri + jnp.where(lane == 2, rank_a, 0.0) + jnp.where(lane == 3, rank_b, 0.0)
    ri_ref[...] = ri.T[0:ROUTE_ROWS, :].astype(jnp.int32)
    rw_ref[...] = jnp.where(lane == 0, wa, 0.0) + jnp.where(lane == 1, wb, 0.0)


ROUTE_ROWS = 8
TOKEN_TILE = 512


def _outproj_router(oa2, ob2, x2, w_o, mod, ln, w_r, b_r, seq):
    t, d = x2.shape
    tm = TOKEN_TILE
    row = lambda w: pl.BlockSpec((tm, w), lambda m: (m, 0))
    full = lambda a, c: pl.BlockSpec((a, c), lambda m: (0, 0))
    return pl.pallas_call(
        _outproj_router_kernel,
        grid=(t // tm,),
        in_specs=[
            row(A_WIDTH), row(G_VAL_WIDTH), row(d),
            pl.BlockSpec((A_WIDTH, d), lambda m: (0, 0)),
            pl.BlockSpec((G_VAL_WIDTH, d), lambda m: (A_WIDTH // G_VAL_WIDTH, 0)),
            pl.BlockSpec((None, 6, d), lambda m: ((m * tm) // seq, 0, 0)),
            full(1, d), full(d, 2 * LANES), full(1, LANES),
        ],
        out_specs=[row(d), row(d // 2), pl.BlockSpec((None, ROUTE_ROWS, tm), lambda m: (m, 0, 0)),
                   row(LANES), full(8, LANES)],
        out_shape=[
            jax.ShapeDtypeStruct((t, d), F32),
            jax.ShapeDtypeStruct((t, d // 2), jnp.uint32),
            jax.ShapeDtypeStruct((t // tm, ROUTE_ROWS, tm), jnp.int32),
            jax.ShapeDtypeStruct((t, LANES), F32),
            jax.ShapeDtypeStruct((8, LANES), jnp.int32),
        ],
        scratch_shapes=[pltpu.VMEM((1, LANES), F32)],
        compiler_params=_cparams(("arbitrary",)),
        name="outproj_router",
    )(oa2, ob2, x2, w_o, w_o, mod, ln, w_r, b_r)


def _dispatch_kernel(dest_ref, h_ref, xg_in, xg_out, sem):
    del xg_in
    tm = h_ref.shape[0]

    for r in range(tm):
        for k in range(TOP_K):
            pltpu.make_async_copy(h_ref.at[pl.ds(r, 1), :],
                                  xg_out.at[pl.ds(dest_ref[0, k * tm + r], 1), :], sem).start(priority=r % 2)
    done = xg_out.at[pl.ds(0, TOP_K * tm), :]
    pltpu.make_async_copy(done, done, sem).wait()


def _dispatch(dest3, h2p, xg):
    t, dp = h2p.shape
    tm = dest3.shape[2] // TOP_K
    return pl.pallas_call(
        _dispatch_kernel,
        grid=(t // tm,),
        in_specs=[
            pl.BlockSpec((None, 1, TOP_K * tm), lambda m: (m, 0, 0), memory_space=pltpu.SMEM),
            pl.BlockSpec((tm, dp), lambda m: (m, 0)),
            pl.BlockSpec(memory_space=pl.ANY),
        ],
        out_specs=pl.BlockSpec(memory_space=pl.ANY),
        out_shape=jax.ShapeDtypeStruct(xg.shape, xg.dtype),
        input_output_aliases={2: 0},
        scratch_shapes=[pltpu.SemaphoreType.DMA(())],
        compiler_params=_cparams(("arbitrary",)),
        name="moe_dispatch",
    )(dest3, h2p, xg)


def _expert_kernel(cur_ref, nxt_ref, last_ref, nused_ref, x_ref, wg_ref, wu_ref, wd_ref,
                   y_ref, wg_res, wu_res, wd_res, xb_scr, acc_scr):
    del cur_ref, nxt_ref
    b, j = pl.program_id(0), pl.program_id(1)
    active = (b >= 1) & (b <= nused_ref[0])

    @pl.when(active & (j == 0))
    def _():
        lo, hi = _unpack_bf16_pairs(x_ref[...])
        half = lo.shape[1]
        xb_scr[:, :half] = lo.astype(BF16)
        xb_scr[:, half:] = hi.astype(BF16)

    for jj in range(EXPERT_HIDDEN_CHUNKS):
        @pl.when(active & (j == jj))
        def _(jj=jj):
            xb = xb_scr[...]
            g = jnp.dot(xb, wg_res[jj], preferred_element_type=F32)
            u = jnp.dot(xb, wu_res[jj], preferred_element_type=F32)
            a = ((g * jax.nn.sigmoid(g)) * u).astype(BF16)
            part = jnp.dot(a, wd_res[jj], preferred_element_type=F32)
            if jj == 0:
                acc_scr[...] = part
            elif jj < EXPERT_HIDDEN_CHUNKS - 1:
                acc_scr[...] += part
            else:
                y_ref[...] = _pack_bf16_pairs(acc_scr[...] + part)

    @pl.when((b > nused_ref[0]) & (j == 0))
    def _():
        y_ref[...] = jnp.zeros_like(y_ref)

    @pl.when(last_ref[b] == 1)
    def _():
        wg_res[j] = wg_ref[...].astype(BF16)
        wu_res[j] = wu_ref[...].astype(BF16)
        wd_res[j] = wd_ref[...].astype(BF16)


EXPERT_HIDDEN_CHUNKS = 2


def _expert_mlp(cur_e, nxt_e, last, n_used, xg, wg, wu, wd, layer):
    dp = xg.shape[1]
    nb = xg.shape[0] // ROW_BLOCK
    d, f = wg.shape[2], wg.shape[3]
    nj = EXPERT_HIDDEN_CHUNKS
    fc = f // nj

    def x_index(b, j, cur, nxt, lst, nu):
        return (jnp.clip(b - 1, 0, nu[0] - 1), 0)

    def y_index(b, j, cur, nxt, lst, nu):
        return (jnp.maximum(b - 1, 0), 0)

    def w_index(chunk_axis):
        def index(b, j, cur, nxt, lst, nu):
            e = jnp.where(lst[b] == 1, nxt[b], cur[b])
            jj = jnp.where(lst[b] == 1, j, nj - 1)
            return (layer, e, 0, jj) if chunk_axis == 2 else (layer, e, jj, 0)
        return index

    grid_spec = pltpu.PrefetchScalarGridSpec(
        num_scalar_prefetch=4,
        grid=(nb + 1, nj),
        in_specs=[
            pl.BlockSpec((ROW_BLOCK, dp), x_index),
            pl.BlockSpec((None, None, d, fc), w_index(2)),
            pl.BlockSpec((None, None, d, fc), w_index(2)),
            pl.BlockSpec((None, None, fc, d), w_index(1)),
        ],
        out_specs=pl.BlockSpec((ROW_BLOCK, dp), y_index),
        scratch_shapes=[
            pltpu.VMEM((nj, d, fc), BF16),
            pltpu.VMEM((nj, d, fc), BF16),
            pltpu.VMEM((nj, fc, d), BF16),
            pltpu.VMEM((ROW_BLOCK, d), BF16),
            pltpu.VMEM((ROW_BLOCK, d), F32),
        ],
    )
    return pl.pallas_call(
        _expert_kernel,
        grid_spec=grid_spec,
        out_shape=jax.ShapeDtypeStruct((nb * ROW_BLOCK, dp), jnp.uint32),
        compiler_params=_cparams(("arbitrary", "arbitrary"), EXPERT_VMEM_LIMIT),
        name="expert_mlp",
    )(cur_e, nxt_e, last, n_used, xg, wg, wu, wd)


def _wait_row_gather(src_hbm, dst_vmem, sem):
    pltpu.make_async_copy(src_hbm.at[pl.ds(0, dst_vmem.shape[0]), :], dst_vmem, sem).wait()


def _combine_kernel(dest_ref, y_hbm, rw_ref, x_ref, mod_ref, lnf_ref, o_ref, y_scr, sems, *, final):
    tm = x_ref.shape[0]
    i = pl.program_id(0)
    n_tiles = pl.num_programs(0) - COMBINE_LOOKAHEAD
    n_slots = COMBINE_LOOKAHEAD + 1

    def start_gather():
        slot = i % n_slots
        for r in range(2 * tm):
            pltpu.make_async_copy(y_hbm.at[pl.ds(dest_ref[0, r], 1), :],
                                  y_scr.at[slot, pl.ds(r, 1), :], sems.at[slot]).start(priority=r % 2)

    def combine(prefetch):
        slot = (i - COMBINE_LOOKAHEAD) % n_slots
        _wait_row_gather(y_hbm, y_scr.at[slot], sems.at[slot])
        rw = rw_ref[...]
        a_lo, a_hi = _unpack_bf16_pairs(y_scr[slot, 0:tm, :])
        b_lo, b_hi = _unpack_bf16_pairs(y_scr[slot, tm:2 * tm, :])
        y = jnp.concatenate([rw[:, 0:1] * a_lo + rw[:, 1:2] * b_lo,
                             rw[:, 0:1] * a_hi + rw[:, 1:2] * b_hi], axis=1)
        if prefetch:
            start_gather()
        x_new = x_ref[...] + mod_ref[5:6, :] * y
        if final:
            x_new = x_new * lax.rsqrt(jnp.mean(x_new * x_new, axis=-1, keepdims=True) + EPS) * lnf_ref[...]
        o_ref[...] = x_new

    pl.when(i < COMBINE_LOOKAHEAD)(start_gather)
    pl.when((i >= COMBINE_LOOKAHEAD) & (i < n_tiles))(functools.partial(combine, True))
    pl.when(i >= n_tiles)(functools.partial(combine, False))


COMBINE_LOOKAHEAD = 2


def _combine(dest3, yg, rw, x2, mod, ln_f, seq, final):
    t, d = x2.shape
    tm = dest3.shape[2] // 2
    nt = t // tm
    prev = lambda m: jnp.maximum(m - COMBINE_LOOKAHEAD, 0)
    return pl.pallas_call(
        functools.partial(_combine_kernel, final=final),
        grid=(nt + COMBINE_LOOKAHEAD,),
        in_specs=[
            pl.BlockSpec((None, 1, 2 * tm), lambda m: (jnp.minimum(m, nt - 1), 0, 0), memory_space=pltpu.SMEM),
            pl.BlockSpec(memory_space=pl.ANY),
            pl.BlockSpec((tm, LANES), lambda m: (prev(m), 0)),
            pl.BlockSpec((tm, d), lambda m: (prev(m), 0)),
            pl.BlockSpec((None, 6, d), lambda m: ((prev(m) * tm) // seq, 0, 0)),
            pl.BlockSpec((1, d), lambda m: (0, 0)),
        ],
        out_specs=pl.BlockSpec((tm, d), lambda m: (prev(m), 0)),
        out_shape=jax.ShapeDtypeStruct((t, d), F32),
        scratch_shapes=[pltpu.VMEM((COMBINE_LOOKAHEAD + 1, 2 * tm, yg.shape[1]), jnp.uint32),
                        pltpu.SemaphoreType.DMA((COMBINE_LOOKAHEAD + 1,))],
        compiler_params=_cparams(("arbitrary",)),
        name="moe_combine",
    )(dest3, yg, rw, x2, mod, ln_f)


def _routing_tables(ri, counts, t):
    n_blocks = (t * TOP_K) // ROW_BLOCK + N_EXPERTS
    cnt = counts[0, :N_EXPERTS]
    padded = (cnt + ROW_BLOCK - 1) // ROW_BLOCK * ROW_BLOCK
    ends = jnp.cumsum(padded)
    starts = ends - padded
    experts, ranks = ri[:, 0:TOP_K, :], ri[:, TOP_K:2 * TOP_K, :]
    dest = (starts[experts] + ranks).reshape(ri.shape[0], 1, TOP_K * ri.shape[2])
    n_used = (ends[-1] // ROW_BLOCK).astype(jnp.int32)
    blocks = jnp.minimum(jnp.arange(n_blocks + 1, dtype=jnp.int32), n_used - 1) * ROW_BLOCK
    block_exp = jnp.sum((blocks[:, None] >= ends[None, :]).astype(jnp.int32), axis=1)
    block_exp = jnp.minimum(block_exp, N_EXPERTS - 1)
    cur_e = jnp.concatenate([block_exp[:1], block_exp[:-1]])
    nxt_e = block_exp
    last = (cur_e != nxt_e).astype(jnp.int32).at[0].set(1)
    return dest.astype(jnp.int32), cur_e, nxt_e, last, n_used.reshape(1)


def kernel(x, c, ln1, ln2, w_ada, b_ada, w_in, w_gk, b_gk, g_onorm, w_out,
           w_r1, b_r1, w_r2, b_r2, w_e_gate, w_e_up, w_e_down, ln_f):
    b, seq, d = x.shape
    depth = w_ada.shape[0]
    t = b * seq
    mod_all = _adaln_mod(c, w_ada, b_ada).reshape(depth, b, 6, d)
    tables = _rope_tables(seq)
    x2 = x.reshape(t, d)
    xg = jnp.zeros(((t * TOP_K // ROW_BLOCK + N_EXPERTS) * ROW_BLOCK, d // 2), jnp.uint32)
    for l in range(depth):
        mod = mod_all[l]
        w_main = w_in[l][:, :PROJ_MAIN].astype(BF16)
        w_ga = jnp.pad(w_in[l][:, PROJ_MAIN:], ((0, 0), (0, LANES - GATE_RANK))).astype(BF16)
        w_gk_p = jnp.pad(w_gk[l], ((0, LANES - GATE_RANK), (0, 0)))
        proj, z = _in_projection(x2, ln1[l][None, :], mod, w_main, w_ga, w_gk_p, b_gk[l][None, :], seq)
        proj3 = proj.reshape(b, seq, PROJ_MAIN)
        oa = _moba_attention(proj3, tables)
        ob = _gla_mixer(proj3, z.reshape(b, seq, G_KEY_WIDTH), g_onorm[l][None, :])
        w_o = w_out[l].astype(BF16)
        w_r = jnp.pad(jnp.concatenate([w_r2[l], w_r1[l]], axis=1), ((0, 0), (0, LANES - N_EXPERTS - N_GROUPS)))
        w_r_hi = w_r.astype(BF16)
        w_r = jnp.concatenate([w_r_hi, (w_r - w_r_hi.astype(F32)).astype(BF16)], axis=1)
        b_r = jnp.pad(jnp.concatenate([b_r2[l], b_r1[l]]), (0, LANES - N_EXPERTS - N_GROUPS))[None, :]
        x2, h2, ri, rw, counts = _outproj_router(
            oa.reshape(t, A_WIDTH), ob.reshape(t, G_VAL_WIDTH), x2, w_o, mod, ln2[l][None, :], w_r, b_r, seq)
        dest, cur_e, nxt_e, last, n_used = _routing_tables(ri, counts, t)
        xg = _dispatch(dest, h2, xg)
        yg = _expert_mlp(cur_e, nxt_e, last, n_used, xg, w_e_gate, w_e_up, w_e_down, l)
        x2 = _combine(dest, yg, rw, x2, mod, ln_f[None, :], seq, final=(l == depth - 1))
    return x2.reshape(b, seq, d)
```

```python
import functools

import jax
import jax.numpy as jnp
import numpy as np
from jax import lax
from jax.experimental import pallas as pl
from jax.experimental.pallas import tpu as pltpu

F32 = jnp.float32
BF16 = jnp.bfloat16
HIGHEST = lax.Precision.HIGHEST

HEAD_DIM = 128
A_HEADS = 8
A_WIDTH = A_HEADS * HEAD_DIM
MOBA_BLOCK = 256
MOBA_TOPK = 3
ROPE_THETA = 500000.0
ROPE_DIM = HEAD_DIM // 4
ATTN_SCALE = HEAD_DIM ** -0.5
G_HEADS = 4
G_VAL_DIM = 256
G_KEY_DIM = 128
G_KEY_WIDTH = G_HEADS * G_KEY_DIM
G_VAL_WIDTH = G_HEADS * G_VAL_DIM
GATE_RANK = 16
GATE_TAU = 16.0
GLA_CHUNK = 64
N_GROUPS = 4
EXPERTS_PER_GROUP = 8
N_EXPERTS = N_GROUPS * EXPERTS_PER_GROUP
TOP_K = 2
EPS = 1e-6

LANES = 128
PROJ_MAIN = 3 * A_WIDTH + 2 * G_KEY_WIDTH + 2 * G_VAL_WIDTH
ROW_BLOCK = 512
VMEM_LIMIT = 52 * 1024 * 1024
EXPERT_VMEM_LIMIT = 58 * 1024 * 1024


def _cparams(sem, vmem_limit=VMEM_LIMIT, flags=None):
    return pltpu.CompilerParams(dimension_semantics=sem, vmem_limit_bytes=vmem_limit, flags=flags)


def _pack_bf16_pairs(x):
    c = x.shape[1] // 2
    lo = pltpu.bitcast(x[:, :c].astype(BF16).astype(F32), jnp.uint32)
    hi = pltpu.bitcast(x[:, c:].astype(BF16).astype(F32), jnp.uint32)
    return (lo >> 16) | hi


def _unpack_bf16_pairs(w):
    lo = pltpu.bitcast(w << 16, F32)
    hi = pltpu.bitcast(w & jnp.uint32(0xFFFF0000), F32)
    return lo, hi


def _mod_kernel(c_ref, w_ref, b_ref, o_ref):
    c = c_ref[...]
    nb = c.shape[0]
    cc = jnp.concatenate(_split_bf16(c * jax.nn.sigmoid(c)), axis=0)
    w_hi, w_lo = _split_bf16(w_ref[...])
    r = jnp.dot(cc, w_hi, preferred_element_type=F32) + jnp.dot(cc, w_lo, preferred_element_type=F32)
    o_ref[...] = (r[:nb] + r[nb:]) + b_ref[...]


def _adaln_mod(c, w_ada, b_ada):
    depth, d, n6 = w_ada.shape
    b = c.shape[0]
    tn = 1024
    return pl.pallas_call(
        _mod_kernel,
        grid=(depth, n6 // tn),
        in_specs=[
            pl.BlockSpec((b, d), lambda l, n: (0, 0)),
            pl.BlockSpec((None, d, tn), lambda l, n: (l, 0, n)),
            pl.BlockSpec((None, 1, tn), lambda l, n: (l, 0, n)),
        ],
        out_specs=pl.BlockSpec((None, b, tn), lambda l, n: (l, 0, n)),
        out_shape=jax.ShapeDtypeStruct((depth, b, n6), F32),
        compiler_params=_cparams(("arbitrary", "arbitrary")),
        name="adaln_mod",
    )(c, w_ada, b_ada.reshape(depth, 1, n6))


def _modulated_norm(x, g, shift, scale):
    y = x * lax.rsqrt(jnp.mean(x * x, axis=-1, keepdims=True) + EPS) * g
    return y * (1.0 + scale) + shift


def _inproj_kernel(x_ref, ln_ref, mod_ref, w_ref, wga_ref, wgk_ref, bgk_ref, proj_ref, z_ref, h_scr):
    @pl.when(pl.program_id(1) == 0)
    def _():
        h = _modulated_norm(x_ref[...], ln_ref[...], mod_ref[0:1, :], mod_ref[1:2, :])
        hb = h.astype(BF16)
        h_scr[...] = hb
        ga = jnp.dot(hb, wga_ref[...], preferred_element_type=F32)
        z_ref[...] = jnp.dot(ga, wgk_ref[...], precision=HIGHEST, preferred_element_type=F32) + bgk_ref[...]

    proj_ref[...] = jnp.dot(h_scr[...], w_ref[...], preferred_element_type=F32).astype(proj_ref.dtype)


def _in_projection(x2, ln, mod, w_main, w_ga, w_gk, b_gk, seq, layer):
    t, d = x2.shape
    tm, tn = 1024, 1024
    return pl.pallas_call(
        _inproj_kernel,
        grid=(t // tm, PROJ_MAIN // tn),
        in_specs=[
            pl.BlockSpec((tm, d), lambda m, n: (m, 0)),
            pl.BlockSpec((1, d), lambda m, n: (0, 0)),
            pl.BlockSpec((None, 6, d), lambda m, n: ((m * tm) // seq, 0, 0)),
            pl.BlockSpec((None, d, tn), lambda m, n: (layer, 0, n)),
            pl.BlockSpec((d, LANES), lambda m, n: (0, 0)),
            pl.BlockSpec((LANES, G_KEY_WIDTH), lambda m, n: (0, 0)),
            pl.BlockSpec((1, G_KEY_WIDTH), lambda m, n: (0, 0)),
        ],
        out_specs=[
            pl.BlockSpec((tm, tn), lambda m, n: (m, n)),
            pl.BlockSpec((tm, G_KEY_WIDTH), lambda m, n: (m, 0)),
        ],
        out_shape=[
            jax.ShapeDtypeStruct((t, PROJ_MAIN), BF16),
            jax.ShapeDtypeStruct((t, G_KEY_WIDTH), F32),
        ],
        scratch_shapes=[pltpu.VMEM((tm, d), BF16)],
        compiler_params=_cparams(("arbitrary", "arbitrary")),
        name="in_projection",
    )(x2, ln, mod, w_main, w_ga, w_gk, b_gk)


def _rope(xb, cos, sin, swap):
    partner = jnp.dot(xb, swap, preferred_element_type=F32)
    return xb.astype(F32) * cos + partner * sin


LOG2E = 1.4426950408889634
DEN_ROWS = 16


def _moba_kernel(q_ref, k_ref, v_ref, cos_ref, sin_ref, swap_ref, o_ref, s_scr):
    seq = q_ref.shape[0]
    nblk = seq // MOBA_BLOCK
    cos, sin, swap = cos_ref[...], sin_ref[...], swap_ref[...]
    q = _rope(q_ref[...], cos, sin, swap) * (ATTN_SCALE * LOG2E)
    k = _rope(k_ref[...], cos, sin, swap)
    kmean = jnp.mean(k.reshape(nblk, MOBA_BLOCK, HEAD_DIM), axis=1)
    gate = lax.dot_general(kmean, q, (((1,), (1,)), ((), ())), precision=HIGHEST,
                           preferred_element_type=F32)
    blk_of_q = lax.broadcasted_iota(jnp.int32, (nblk, seq), 1) // MOBA_BLOCK
    row = lax.broadcasted_iota(jnp.int32, (nblk, seq), 0)
    past = row < blk_of_q
    better = jnp.zeros((nblk, seq), F32)
    for m in range(nblk):
        gm = gate[m:m + 1, :]
        past_m = blk_of_q[m:m + 1, :] > m
        beats = (gm > gate) | ((gm == gate) & (row > m))
        better = better + jnp.where(beats & past_m, 1.0, 0.0)
    sel = past & (better < float(MOBA_TOPK))

    qb = q.astype(BF16)
    kb = k.astype(BF16)
    vt = jnp.concatenate([v_ref[...].astype(F32).T, jnp.ones((DEN_ROWS, seq), F32)], axis=0).astype(BF16)
    key_i = lax.broadcasted_iota(jnp.int32, (MOBA_BLOCK, MOBA_BLOCK), 0)
    qry_i = lax.broadcasted_iota(jnp.int32, (MOBA_BLOCK, MOBA_BLOCK), 1)
    causal = key_i <= qry_i
    def blk(j):
        return slice(j * MOBA_BLOCK, (j + 1) * MOBA_BLOCK)

    def score_pass(i):
        mx = None
        for n in range(i + 1):
            s = lax.dot_general(kb[blk(n), :], qb[blk(i), :], (((1,), (1,)), ((), ())),
                                preferred_element_type=F32)
            s = jnp.where(causal if n == i else sel[n:n + 1, blk(i)], s, -jnp.inf)
            s_scr[i % 2, n] = s
            bm = jnp.max(s, axis=0, keepdims=True)
            mx = bm if mx is None else jnp.maximum(mx, bm)
            yield mx

    def value_pass(i, mx):
        acc = jnp.zeros((HEAD_DIM + DEN_ROWS, MOBA_BLOCK), F32)
        for n in range(i + 1):
            p = jnp.exp2(s_scr[i % 2, n] - mx)
            acc = acc + jnp.dot(vt[:, blk(n)], p.astype(BF16), preferred_element_type=F32)
            yield None
        den = acc[HEAD_DIM:HEAD_DIM + 1, :]
        o_ref[blk(i), :] = (acc[:HEAD_DIM, :] / den).T.astype(o_ref.dtype)
        yield None

    mx = list(score_pass(0))[-1]
    for i in range(nblk):
        values = value_pass(i, mx)
        scores = score_pass(i + 1) if i + 1 < nblk else iter(())
        for mx_next in scores:
            mx = mx_next
            next(values, None)
        for _ in values:
            pass


def _rope_tables(seq):
    half = ROPE_DIM // 2
    inv = jnp.power(ROPE_THETA, -jnp.arange(half, dtype=F32) * (2.0 / ROPE_DIM))
    ang = jnp.arange(seq, dtype=F32)[:, None] * inv[None, :]
    cos, sin = jnp.cos(ang), jnp.sin(ang)
    ones = jnp.ones((seq, HEAD_DIM - ROPE_DIM), F32)
    zeros = jnp.zeros((seq, HEAD_DIM - half), F32)
    cos_t = jnp.concatenate([cos, cos, ones], axis=1)
    sin_t = jnp.concatenate([-sin, sin, zeros[:, half:]], axis=1)
    src = jnp.arange(HEAD_DIM)[:, None]
    dst = jnp.arange(HEAD_DIM)[None, :]
    swap = ((dst < ROPE_DIM) & (src == jnp.where(dst < half, dst + half, dst - half))).astype(BF16)
    return cos_t, sin_t, swap


def _moba_attention(proj3, tables):
    b, seq, _ = proj3.shape
    head = lambda off: pl.BlockSpec((None, seq, HEAD_DIM), lambda bi, h: (bi, 0, off + h))
    tab = pl.BlockSpec((seq, HEAD_DIM), lambda bi, h: (0, 0))
    return pl.pallas_call(
        _moba_kernel,
        grid=(b, A_HEADS),
        in_specs=[head(0), head(A_HEADS), head(2 * A_HEADS), tab, tab,
                  pl.BlockSpec((HEAD_DIM, HEAD_DIM), lambda bi, h: (0, 0))],
        out_specs=pl.BlockSpec((None, seq, HEAD_DIM), lambda bi, h: (bi, 0, h)),
        out_shape=jax.ShapeDtypeStruct((b, seq, A_WIDTH), BF16),
        scratch_shapes=[pltpu.VMEM((2, seq // MOBA_BLOCK, MOBA_BLOCK, MOBA_BLOCK), F32)],
        compiler_params=_cparams(("arbitrary", "arbitrary")),
        name="moba_attention",
    )(proj3, proj3, proj3, *tables)


def _log_sigmoid(z):
    return jnp.minimum(z, 0.0) - jnp.log1p(jnp.exp(-jnp.abs(z)))


def _chunk_cumsum(x):
    pos = lax.broadcasted_iota(jnp.int32, x.shape, 0) % GLA_CHUNK
    shift = 1
    while shift < GLA_CHUNK:
        x = x + jnp.where(pos >= shift, pltpu.roll(x, shift, 0), 0.0)
        shift *= 2
    return x


def _gla_kernel(q_ref, k_ref, v_ref, og_ref, z_ref, gn_ref, o_ref,
                qd_scr, ki_scr, ke_scr, dec_scr, kv_scr, st_scr, oi_scr):
    seq = q_ref.shape[0]
    nc = seq // GLA_CHUNK
    log_a = _log_sigmoid(z_ref[...]) / GATE_TAU
    b = _chunk_cumsum(log_a)
    b3 = b.reshape(nc, GLA_CHUNK, G_KEY_DIM)
    b_last = b3[:, GLA_CHUNK - 1:GLA_CHUNK, :]
    k = k_ref[...].astype(F32)
    qd_scr[...] = ((q_ref[...].astype(F32) * (G_KEY_DIM ** -0.5)) * jnp.exp(b)).astype(BF16)
    ki_scr[...] = (k * jnp.exp(-b)).astype(BF16)
    ke3 = k.reshape(nc, GLA_CHUNK, G_KEY_DIM) * jnp.exp(b_last - b3)
    ke_scr[...] = ke3.reshape(seq, G_KEY_DIM).astype(BF16)
    dec_scr[...] = jnp.exp(b_last)
    t_i = lax.broadcasted_iota(jnp.int32, (GLA_CHUNK, GLA_CHUNK), 0)
    s_i = lax.broadcasted_iota(jnp.int32, (GLA_CHUNK, GLA_CHUNK), 1)
    causal = s_i <= t_i
    gn = gn_ref[...]

    def chunk_rows(n):
        return pl.ds(pl.multiple_of(n * GLA_CHUNK, GLA_CHUNK), GLA_CHUNK)

    def intra(n, carry):
        rows = chunk_rows(n)
        vb = v_ref[rows, :].astype(BF16)
        att = lax.dot_general(qd_scr[rows, :], ki_scr[rows, :], (((1,), (1,)), ((), ())),
                              preferred_element_type=F32)
        att = jnp.where(causal, att, 0.0).astype(BF16)
        oi_scr[rows, :] = jnp.dot(att, vb, preferred_element_type=F32)
        kv_scr[n] = lax.dot_general(vb, ke_scr[rows, :], (((0,), (0,)), ((), ())),
                                    preferred_element_type=F32)
        return carry

    lax.fori_loop(0, nc, intra, 0, unroll=GLA_UNROLL)

    def scan(n, state):
        st_scr[n] = state.astype(BF16)
        return state * dec_scr[n] + kv_scr[n]

    lax.fori_loop(0, nc, scan, jnp.zeros((G_VAL_DIM, G_KEY_DIM), F32), unroll=GLA_UNROLL)

    def inter(n, carry):
        rows = chunk_rows(n)
        o = oi_scr[rows, :] + lax.dot_general(qd_scr[rows, :], st_scr[n], (((1,), (1,)), ((), ())),
                                              preferred_element_type=F32)
        o = o * lax.rsqrt(jnp.mean(o * o, axis=-1, keepdims=True) + EPS) * gn
        og = og_ref[rows, :].astype(F32)
        o_ref[rows, :] = (o * (og * jax.nn.sigmoid(og))).astype(o_ref.dtype)
        return carry

    lax.fori_loop(0, nc, inter, 0, unroll=GLA_UNROLL)


GLA_UNROLL = 16


def _gla_mixer(proj3, z3, g_onorm):
    b, seq, _ = proj3.shape
    kq = 3 * A_WIDTH // G_KEY_DIM
    kv = (3 * A_WIDTH + 2 * G_KEY_WIDTH) // G_VAL_DIM
    key = lambda off: pl.BlockSpec((None, seq, G_KEY_DIM), lambda bi, h: (bi, 0, off + h))
    val = lambda off: pl.BlockSpec((None, seq, G_VAL_DIM), lambda bi, h: (bi, 0, off + h))
    return pl.pallas_call(
        _gla_kernel,
        grid=(b, G_HEADS),
        in_specs=[
            key(kq), key(kq + G_HEADS), val(kv), val(kv + G_HEADS),
            pl.BlockSpec((None, seq, G_KEY_DIM), lambda bi, h: (bi, 0, h)),
            pl.BlockSpec((1, G_VAL_DIM), lambda bi, h: (0, 0)),
        ],
        out_specs=pl.BlockSpec((None, seq, G_VAL_DIM), lambda bi, h: (bi, 0, h)),
        out_shape=jax.ShapeDtypeStruct((b, seq, G_VAL_WIDTH), BF16),
        scratch_shapes=[
            pltpu.VMEM((seq, G_KEY_DIM), BF16),
            pltpu.VMEM((seq, G_KEY_DIM), BF16),
            pltpu.VMEM((seq, G_KEY_DIM), BF16),
            pltpu.VMEM((seq // GLA_CHUNK, 1, G_KEY_DIM), F32),
            pltpu.VMEM((seq // GLA_CHUNK, G_VAL_DIM, G_KEY_DIM), F32),
            pltpu.VMEM((seq // GLA_CHUNK, G_VAL_DIM, G_KEY_DIM), BF16),
            pltpu.VMEM((seq, G_VAL_DIM), F32),
        ],
        compiler_params=_cparams(("arbitrary", "arbitrary")),
        name="gla_mixer",
    )(proj3, proj3, proj3, proj3, z3, g_onorm)


def _first_lane(mask, lane):
    return jnp.min(jnp.where(mask, lane, LANES), axis=-1, keepdims=True)


def _split_bf16(x):
    hi = x.astype(BF16)
    return hi, (x - hi.astype(F32)).astype(BF16)


def _outproj_router_kernel(oa_ref, ob_ref, x_ref, wa_ref, wb_ref, mod_ref, ln_ref, wr_ref, br_ref,
                           xo_ref, h_ref, ri_ref, rw_ref, cnt_ref, run_scr):
    tm = x_ref.shape[0]

    @pl.when(pl.program_id(0) == 0)
    def _():
        run_scr[...] = jnp.zeros_like(run_scr)

    mix = jnp.dot(oa_ref[...], wa_ref[...], preferred_element_type=F32)
    mix = mix + jnp.dot(ob_ref[...], wb_ref[...], preferred_element_type=F32)
    x_new = x_ref[...] + mod_ref[2:3, :] * mix
    xo_ref[...] = x_new
    h = _modulated_norm(x_new, ln_ref[...], mod_ref[3:4, :], mod_ref[4:5, :])
    h_ref[...] = _pack_bf16_pairs(h)

    hh = jnp.concatenate(_split_bf16(h), axis=0)
    r = jnp.dot(hh, wr_ref[...], preferred_element_type=F32)
    logits = (r[:tm, :LANES] + r[tm:, :LANES]) + (r[:tm, LANES:] + r[tm:, LANES:]) + br_ref[...]
    lane = lax.broadcasted_iota(jnp.int32, (tm, LANES), 1)
    is_grp = (lane >= N_EXPERTS) & (lane < N_EXPERTS + N_GROUPS)
    l1 = jnp.where(is_grp, logits, -jnp.inf)
    m1 = jnp.max(l1, axis=-1, keepdims=True)
    grp = _first_lane(l1 == m1, lane) - N_EXPERTS
    p_grp = 1.0 / jnp.sum(jnp.exp(l1 - m1), axis=-1, keepdims=True)
    in_grp = (lane < N_EXPERTS) & ((lane // EXPERTS_PER_GROUP) == grp)
    l2 = jnp.where(in_grp, logits, -jnp.inf)
    va = jnp.max(l2, axis=-1, keepdims=True)
    ia = _first_lane(l2 == va, lane)
    l2b = jnp.where(lane == ia, -jnp.inf, l2)
    vb = jnp.max(l2b, axis=-1, keepdims=True)
    ib = _first_lane(l2b == vb, lane)
    eb = jnp.exp(vb - va)
    wa = p_grp * (1.0 / (1.0 + eb))
    wb = p_grp * (eb / (1.0 + eb))

    oh_a = jnp.where(lane == ia, 1.0, 0.0)
    oh_b = jnp.where(lane == ib, 1.0, 0.0)
    cnt = oh_a + oh_b
    r_i = lax.broadcasted_iota(jnp.int32, (tm, tm), 0)
    c_i = lax.broadcasted_iota(jnp.int32, (tm, tm), 1)
    strict_lower = jnp.where(c_i < r_i, 1.0, 0.0).astype(BF16)
    before = jnp.dot(strict_lower, cnt.astype(BF16), preferred_element_type=F32) + run_scr[...]
    rank_a = jnp.sum(oh_a * before, axis=-1, keepdims=True)
    rank_b = jnp.sum(oh_b * before, axis=-1, keepdims=True)
    run_new = run_scr[...] + jnp.sum(cnt, axis=0, keepdims=True)
    run_scr[...] = run_new
    cnt_ref[...] = jnp.broadcast_to(run_new, cnt_ref.shape).astype(jnp.int32)

    ri = jnp.where(lane == 0, ia.astype(F32), 0.0) + jnp.where(lane == 1, ib.astype(F32), 0.0)
    ri = ri + jnp.where(lane == 2, rank_a, 0.0) + jnp.where(lane == 3, rank_b, 0.0)
    ri_ref[...] = ri.T[0:ROUTE_ROWS, :].astype(jnp.int32)
    rw_ref[...] = jnp.where(lane == 0, wa, 0.0) + jnp.where(lane == 1, wb, 0.0)


ROUTE_ROWS = 8
TOKEN_TILE = 512


def _outproj_router(oa2, ob2, x2, w_o, mod, ln, w_r, b_r, seq, layer):
    t, d = x2.shape
    tm = TOKEN_TILE
    row = lambda w: pl.BlockSpec((tm, w), lambda m: (m, 0))
    full = lambda a, c: pl.BlockSpec((a, c), lambda m: (0, 0))
    return pl.pallas_call(
        _outproj_router_kernel,
        grid=(t // tm,),
        in_specs=[
            row(A_WIDTH), row(G_VAL_WIDTH), row(d),
            pl.BlockSpec((None, A_WIDTH, d), lambda m: (layer, 0, 0)),
            pl.BlockSpec((None, G_VAL_WIDTH, d), lambda m: (layer, A_WIDTH // G_VAL_WIDTH, 0)),
            pl.BlockSpec((None, 6, d), lambda m: ((m * tm) // seq, 0, 0)),
            full(1, d), full(d, 2 * LANES), full(1, LANES),
        ],
        out_specs=[row(d), row(d // 2), pl.BlockSpec((None, ROUTE_ROWS, tm), lambda m: (m, 0, 0)),
                   row(LANES), full(8, LANES)],
        out_shape=[
            jax.ShapeDtypeStruct((t, d), F32),
            jax.ShapeDtypeStruct((t, d // 2), jnp.uint32),
            jax.ShapeDtypeStruct((t // tm, ROUTE_ROWS, tm), jnp.int32),
            jax.ShapeDtypeStruct((t, LANES), F32),
            jax.ShapeDtypeStruct((8, LANES), jnp.int32),
        ],
        scratch_shapes=[pltpu.VMEM((1, LANES), F32)],
        compiler_params=_cparams(("arbitrary",)),
        name="outproj_router",
    )(oa2, ob2, x2, w_o, w_o, mod, ln, w_r, b_r)


def _dispatch_kernel(dest_ref, h_ref, xg_in, xg_out, sem):
    del xg_in
    tm = h_ref.shape[0]

    for r in range(tm):
        for k in range(TOP_K):
            pltpu.make_async_copy(h_ref.at[pl.ds(r, 1), :],
                                  xg_out.at[pl.ds(dest_ref[0, k * tm + r], 1), :], sem).start(priority=r % 2)
    done = xg_out.at[pl.ds(0, TOP_K * tm), :]
    pltpu.make_async_copy(done, done, sem).wait()


def _dispatch(dest3, h2p, xg):
    t, dp = h2p.shape
    tm = dest3.shape[2] // TOP_K
    return pl.pallas_call(
        _dispatch_kernel,
        grid=(t // tm,),
        in_specs=[
            pl.BlockSpec((None, 1, TOP_K * tm), lambda m: (m, 0, 0), memory_space=pltpu.SMEM),
            pl.BlockSpec((tm, dp), lambda m: (m, 0)),
            pl.BlockSpec(memory_space=pl.ANY),
        ],
        out_specs=pl.BlockSpec(memory_space=pl.ANY),
        out_shape=jax.ShapeDtypeStruct(xg.shape, xg.dtype),
        input_output_aliases={2: 0},
        scratch_shapes=[pltpu.SemaphoreType.DMA(())],
        compiler_params=_cparams(("arbitrary",)),
        name="moe_dispatch",
    )(dest3, h2p, xg)


def _expert_kernel(cur_ref, nxt_ref, last_ref, nused_ref, x_ref, wg_ref, wu_ref, wd_ref,
                   y_ref, wg_res, wu_res, wd_res, xb_scr, acc_scr):
    del cur_ref, nxt_ref
    b, j = pl.program_id(0), pl.program_id(1)
    active = (b >= 1) & (b <= nused_ref[0])

    @pl.when(active & (j == 0))
    def _():
        lo, hi = _unpack_bf16_pairs(x_ref[...])
        half = lo.shape[1]
        xb_scr[:, :half] = lo.astype(BF16)
        xb_scr[:, half:] = hi.astype(BF16)

    for jj in range(EXPERT_HIDDEN_CHUNKS):
        @pl.when(active & (j == jj))
        def _(jj=jj):
            xb = xb_scr[...]
            g = jnp.dot(xb, wg_res[jj], preferred_element_type=F32)
            u = jnp.dot(xb, wu_res[jj], preferred_element_type=F32)
            a = ((g * jax.nn.sigmoid(g)) * u).astype(BF16)
            part = jnp.dot(a, wd_res[jj], preferred_element_type=F32)
            if jj == 0:
                acc_scr[...] = part
            elif jj < EXPERT_HIDDEN_CHUNKS - 1:
                acc_scr[...] += part
            else:
                y_ref[...] = _pack_bf16_pairs(acc_scr[...] + part)

    @pl.when((b > nused_ref[0]) & (j == 0))
    def _():
        y_ref[...] = jnp.zeros_like(y_ref)

    @pl.when(last_ref[b] == 1)
    def _():
        wg_res[j] = wg_ref[...].astype(BF16)
        wu_res[j] = wu_ref[...].astype(BF16)
        wd_res[j] = wd_ref[...].astype(BF16)


EXPERT_HIDDEN_CHUNKS = 2


def _expert_mlp(cur_e, nxt_e, last, n_used, xg, wg, wu, wd, layer):
    dp = xg.shape[1]
    nb = xg.shape[0] // ROW_BLOCK
    d, f = wg.shape[2], wg.shape[3]
    nj = EXPERT_HIDDEN_CHUNKS
    fc = f // nj

    def x_index(b, j, cur, nxt, lst, nu):
        return (jnp.clip(b - 1, 0, nu[0] - 1), 0)

    def y_index(b, j, cur, nxt, lst, nu):
        return (jnp.maximum(b - 1, 0), 0)

    def w_index(chunk_axis):
        def index(b, j, cur, nxt, lst, nu):
            e = jnp.where(lst[b] == 1, nxt[b], cur[b])
            jj = jnp.where(lst[b] == 1, j, nj - 1)
            return (layer, e, 0, jj) if chunk_axis == 2 else (layer, e, jj, 0)
        return index

    grid_spec = pltpu.PrefetchScalarGridSpec(
        num_scalar_prefetch=4,
        grid=(nb + 1, nj),
        in_specs=[
            pl.BlockSpec((ROW_BLOCK, dp), x_index),
            pl.BlockSpec((None, None, d, fc), w_index(2)),
            pl.BlockSpec((None, None, d, fc), w_index(2)),
            pl.BlockSpec((None, None, fc, d), w_index(1)),
        ],
        out_specs=pl.BlockSpec((ROW_BLOCK, dp), y_index),
        scratch_shapes=[
            pltpu.VMEM((nj, d, fc), BF16),
            pltpu.VMEM((nj, d, fc), BF16),
            pltpu.VMEM((nj, fc, d), BF16),
            pltpu.VMEM((ROW_BLOCK, d), BF16),
            pltpu.VMEM((ROW_BLOCK, d), F32),
        ],
    )
    return pl.pallas_call(
        _expert_kernel,
        grid_spec=grid_spec,
        out_shape=jax.ShapeDtypeStruct((nb * ROW_BLOCK, dp), jnp.uint32),
        compiler_params=_cparams(("arbitrary", "arbitrary"), EXPERT_VMEM_LIMIT),
        name="expert_mlp",
    )(cur_e, nxt_e, last, n_used, xg, wg, wu, wd)


def _wait_row_gather(src_hbm, dst_vmem, sem):
    pltpu.make_async_copy(src_hbm.at[pl.ds(0, dst_vmem.shape[0]), :], dst_vmem, sem).wait()


def _combine_kernel(dest_ref, y_hbm, rw_ref, x_ref, mod_ref, lnf_ref, o_ref, y_scr, sems, *, final):
    tm = x_ref.shape[0]
    i = pl.program_id(0)
    n_tiles = pl.num_programs(0) - COMBINE_LOOKAHEAD
    n_slots = COMBINE_LOOKAHEAD + 1

    def start_gather():
        slot = i % n_slots
        for r in range(2 * tm):
            pltpu.make_async_copy(y_hbm.at[pl.ds(dest_ref[0, r], 1), :],
                                  y_scr.at[slot, pl.ds(r, 1), :], sems.at[slot]).start(priority=r % 2)

    def combine(prefetch):
        slot = (i - COMBINE_LOOKAHEAD) % n_slots
        _wait_row_gather(y_hbm, y_scr.at[slot], sems.at[slot])
        rw = rw_ref[...]
        a_lo, a_hi = _unpack_bf16_pairs(y_scr[slot, 0:tm, :])
        b_lo, b_hi = _unpack_bf16_pairs(y_scr[slot, tm:2 * tm, :])
        y = jnp.concatenate([rw[:, 0:1] * a_lo + rw[:, 1:2] * b_lo,
                             rw[:, 0:1] * a_hi + rw[:, 1:2] * b_hi], axis=1)
        if prefetch:
            start_gather()
        x_new = x_ref[...] + mod_ref[5:6, :] * y
        if final:
            x_new = x_new * lax.rsqrt(jnp.mean(x_new * x_new, axis=-1, keepdims=True) + EPS) * lnf_ref[...]
        o_ref[...] = x_new

    pl.when(i < COMBINE_LOOKAHEAD)(start_gather)
    pl.when((i >= COMBINE_LOOKAHEAD) & (i < n_tiles))(functools.partial(combine, True))
    pl.when(i >= n_tiles)(functools.partial(combine, False))


COMBINE_LOOKAHEAD = 2


def _combine(dest3, yg, rw, x2, mod, ln_f, seq, final):
    t, d = x2.shape
    tm = dest3.shape[2] // 2
    nt = t // tm
    prev = lambda m: jnp.maximum(m - COMBINE_LOOKAHEAD, 0)
    return pl.pallas_call(
        functools.partial(_combine_kernel, final=final),
        grid=(nt + COMBINE_LOOKAHEAD,),
        in_specs=[
            pl.BlockSpec((None, 1, 2 * tm), lambda m: (jnp.minimum(m, nt - 1), 0, 0), memory_space=pltpu.SMEM),
            pl.BlockSpec(memory_space=pl.ANY),
            pl.BlockSpec((tm, LANES), lambda m: (prev(m), 0)),
            pl.BlockSpec((tm, d), lambda m: (prev(m), 0)),
            pl.BlockSpec((None, 6, d), lambda m: ((prev(m) * tm) // seq, 0, 0)),
            pl.BlockSpec((1, d), lambda m: (0, 0)),
        ],
        out_specs=pl.BlockSpec((tm, d), lambda m: (prev(m), 0)),
        out_shape=jax.ShapeDtypeStruct((t, d), F32),
        scratch_shapes=[pltpu.VMEM((COMBINE_LOOKAHEAD + 1, 2 * tm, yg.shape[1]), jnp.uint32),
                        pltpu.SemaphoreType.DMA((COMBINE_LOOKAHEAD + 1,))],
        compiler_params=_cparams(("arbitrary",)),
        name="moe_combine",
    )(dest3, yg, rw, x2, mod, ln_f)


def _routing_tables(ri, counts, t):
    n_blocks = (t * TOP_K) // ROW_BLOCK + N_EXPERTS
    cnt = counts[0, :N_EXPERTS]
    padded = (cnt + ROW_BLOCK - 1) // ROW_BLOCK * ROW_BLOCK
    ends = jnp.cumsum(padded)
    starts = ends - padded
    experts, ranks = ri[:, 0:TOP_K, :], ri[:, TOP_K:2 * TOP_K, :]
    offsets = jnp.sum(jnp.where(experts[..., None] == jnp.arange(N_EXPERTS), starts, 0), axis=-1)
    dest = (offsets + ranks).reshape(ri.shape[0], 1, TOP_K * ri.shape[2])
    n_used = (ends[-1] // ROW_BLOCK).astype(jnp.int32)
    blocks = jnp.minimum(jnp.arange(n_blocks + 1, dtype=jnp.int32), n_used - 1) * ROW_BLOCK
    block_exp = jnp.sum((blocks[:, None] >= ends[None, :]).astype(jnp.int32), axis=1)
    block_exp = jnp.minimum(block_exp, N_EXPERTS - 1)
    cur_e = jnp.concatenate([block_exp[:1], block_exp[:-1]])
    nxt_e = block_exp
    last = (cur_e != nxt_e).astype(jnp.int32).at[0].set(1)
    return dest.astype(jnp.int32), cur_e, nxt_e, last, n_used.reshape(1)


def kernel(x, c, ln1, ln2, w_ada, b_ada, w_in, w_gk, b_gk, g_onorm, w_out,
           w_r1, b_r1, w_r2, b_r2, w_e_gate, w_e_up, w_e_down, ln_f):
    b, seq, d = x.shape
    depth = w_ada.shape[0]
    t = b * seq
    mod_all = _adaln_mod(c, w_ada, b_ada).reshape(depth, b, 6, d)
    tables = _rope_tables(seq)
    x2 = x.reshape(t, d)
    xg = jnp.zeros(((t * TOP_K // ROW_BLOCK + N_EXPERTS) * ROW_BLOCK, d // 2), jnp.uint32)
    w_main = w_in[:, :, :PROJ_MAIN].astype(BF16)
    w_o = w_out.astype(BF16)
    for l in range(depth):
        mod = mod_all[l]
        w_ga = jnp.pad(w_in[l][:, PROJ_MAIN:], ((0, 0), (0, LANES - GATE_RANK))).astype(BF16)
        w_gk_p = jnp.pad(w_gk[l], ((0, LANES - GATE_RANK), (0, 0)))
        proj, z = _in_projection(x2, ln1[l][None, :], mod, w_main, w_ga, w_gk_p, b_gk[l][None, :], seq, l)
        proj3 = proj.reshape(b, seq, PROJ_MAIN)
        oa = _moba_attention(proj3, tables)
        ob = _gla_mixer(proj3, z.reshape(b, seq, G_KEY_WIDTH), g_onorm[l][None, :])
        w_r = jnp.pad(jnp.concatenate([w_r2[l], w_r1[l]], axis=1), ((0, 0), (0, LANES - N_EXPERTS - N_GROUPS)))
        w_r_hi = w_r.astype(BF16)
        w_r = jnp.concatenate([w_r_hi, (w_r - w_r_hi.astype(F32)).astype(BF16)], axis=1)
        b_r = jnp.pad(jnp.concatenate([b_r2[l], b_r1[l]]), (0, LANES - N_EXPERTS - N_GROUPS))[None, :]
        x2, h2, ri, rw, counts = _outproj_router(
            oa.reshape(t, A_WIDTH), ob.reshape(t, G_VAL_WIDTH), x2, w_o, mod, ln2[l][None, :], w_r, b_r, seq, l)
        dest, cur_e, nxt_e, last, n_used = _routing_tables(ri, counts, t)
        xg = _dispatch(dest, h2, xg)
        yg = _expert_mlp(cur_e, nxt_e, last, n_used, xg, w_e_gate, w_e_up, w_e_down, l)
        x2 = _combine(dest, yg, rw, x2, mod, ln_f[None, :], seq, final=(l == depth - 1))
    return x2.reshape(b, seq, d)
```

```python
import functools

import jax
import jax.numpy as jnp
import numpy as np
from jax import lax
from jax.experimental import pallas as pl
from jax.experimental.pallas import tpu as pltpu

F32 = jnp.float32
BF16 = jnp.bfloat16
HIGHEST = lax.Precision.HIGHEST

HEAD_DIM = 128
A_HEADS = 8
A_WIDTH = A_HEADS * HEAD_DIM
MOBA_BLOCK = 256
MOBA_TOPK = 3
ROPE_THETA = 500000.0
ROPE_DIM = HEAD_DIM // 4
ATTN_SCALE = HEAD_DIM ** -0.5
G_HEADS = 4
G_VAL_DIM = 256
G_KEY_DIM = 128
G_KEY_WIDTH = G_HEADS * G_KEY_DIM
G_VAL_WIDTH = G_HEADS * G_VAL_DIM
GATE_RANK = 16
GATE_TAU = 16.0
GLA_CHUNK = 64
N_GROUPS = 4
EXPERTS_PER_GROUP = 8
N_EXPERTS = N_GROUPS * EXPERTS_PER_GROUP
TOP_K = 2
EPS = 1e-6

LANES = 128
PROJ_MAIN = 3 * A_WIDTH + 2 * G_KEY_WIDTH + 2 * G_VAL_WIDTH
ROW_BLOCK = 512
VMEM_LIMIT = 52 * 1024 * 1024
EXPERT_VMEM_LIMIT = 58 * 1024 * 1024


def _cparams(sem, vmem_limit=VMEM_LIMIT, flags=None):
    return pltpu.CompilerParams(dimension_semantics=sem, vmem_limit_bytes=vmem_limit, flags=flags)


def _pack_bf16_pairs(x):
    c = x.shape[1] // 2
    lo = pltpu.bitcast(x[:, :c].astype(BF16).astype(F32), jnp.uint32)
    hi = pltpu.bitcast(x[:, c:].astype(BF16).astype(F32), jnp.uint32)
    return (lo >> 16) | hi


def _unpack_bf16_pairs(w):
    lo = pltpu.bitcast(w << 16, F32)
    hi = pltpu.bitcast(w & jnp.uint32(0xFFFF0000), F32)
    return lo, hi


def _mod_kernel(c_ref, w_ref, b_ref, o_ref):
    c = c_ref[...]
    nb = c.shape[0]
    cc = jnp.concatenate(_split_bf16(c * jax.nn.sigmoid(c)), axis=0)
    w_hi, w_lo = _split_bf16(w_ref[...])
    r = jnp.dot(cc, w_hi, preferred_element_type=F32) + jnp.dot(cc, w_lo, preferred_element_type=F32)
    o_ref[...] = (r[:nb] + r[nb:]) + b_ref[...]


def _adaln_mod(c, w_ada, b_ada):
    depth, d, n6 = w_ada.shape
    b = c.shape[0]
    tn = 1024
    return pl.pallas_call(
        _mod_kernel,
        grid=(depth, n6 // tn),
        in_specs=[
            pl.BlockSpec((b, d), lambda l, n: (0, 0)),
            pl.BlockSpec((None, d, tn), lambda l, n: (l, 0, n)),
            pl.BlockSpec((None, 1, tn), lambda l, n: (l, 0, n)),
        ],
        out_specs=pl.BlockSpec((None, b, tn), lambda l, n: (l, 0, n)),
        out_shape=jax.ShapeDtypeStruct((depth, b, n6), F32),
        compiler_params=_cparams(("arbitrary", "arbitrary")),
        name="adaln_mod",
    )(c, w_ada, b_ada.reshape(depth, 1, n6))


def _modulated_norm(x, g, shift, scale):
    y = x * lax.rsqrt(jnp.mean(x * x, axis=-1, keepdims=True) + EPS) * g
    return y * (1.0 + scale) + shift


def _inproj_kernel(x_ref, ln_ref, mod_ref, w_ref, wga_ref, wgk_ref, bgk_ref, proj_ref, z_ref, h_scr):
    @pl.when(pl.program_id(1) == 0)
    def _():
        h = _modulated_norm(x_ref[...], ln_ref[...], mod_ref[0:1, :], mod_ref[1:2, :])
        hb = h.astype(BF16)
        h_scr[...] = hb
        ga = jnp.dot(hb, wga_ref[...], preferred_element_type=F32)
        z_ref[...] = jnp.dot(ga, wgk_ref[...], precision=HIGHEST, preferred_element_type=F32) + bgk_ref[...]

    proj_ref[...] = jnp.dot(h_scr[...], w_ref[...], preferred_element_type=F32).astype(proj_ref.dtype)


def _in_projection(x2, ln, mod, w_main, w_ga, w_gk, b_gk, seq, layer):
    t, d = x2.shape
    tm, tn = 1024, 1024
    return pl.pallas_call(
        _inproj_kernel,
        grid=(t // tm, PROJ_MAIN // tn),
        in_specs=[
            pl.BlockSpec((tm, d), lambda m, n: (m, 0)),
            pl.BlockSpec((1, d), lambda m, n: (0, 0)),
            pl.BlockSpec((None, 6, d), lambda m, n: ((m * tm) // seq, 0, 0)),
            pl.BlockSpec((None, d, tn), lambda m, n: (layer, 0, n)),
            pl.BlockSpec((d, LANES), lambda m, n: (0, 0)),
            pl.BlockSpec((LANES, G_KEY_WIDTH), lambda m, n: (0, 0)),
            pl.BlockSpec((1, G_KEY_WIDTH), lambda m, n: (0, 0)),
        ],
        out_specs=[
            pl.BlockSpec((tm, tn), lambda m, n: (m, n)),
            pl.BlockSpec((tm, G_KEY_WIDTH), lambda m, n: (m, 0)),
        ],
        out_shape=[
            jax.ShapeDtypeStruct((t, PROJ_MAIN), BF16),
            jax.ShapeDtypeStruct((t, G_KEY_WIDTH), F32),
        ],
        scratch_shapes=[pltpu.VMEM((tm, d), BF16)],
        compiler_params=_cparams(("arbitrary", "arbitrary")),
        name="in_projection",
    )(x2, ln, mod, w_main, w_ga, w_gk, b_gk)


def _rope(xb, cos, sin, swap):
    partner = jnp.dot(xb, swap, preferred_element_type=F32)
    return xb.astype(F32) * cos + partner * sin


LOG2E = 1.4426950408889634
DEN_ROWS = 16


def _moba_kernel(q_ref, k_ref, v_ref, cos_ref, sin_ref, swap_ref, o_ref, s_scr):
    seq = q_ref.shape[0]
    nblk = seq // MOBA_BLOCK
    cos, sin, swap = cos_ref[...], sin_ref[...], swap_ref[...]
    q = _rope(q_ref[...], cos, sin, swap) * (ATTN_SCALE * LOG2E)
    k = _rope(k_ref[...], cos, sin, swap)
    kmean = jnp.mean(k.reshape(nblk, MOBA_BLOCK, HEAD_DIM), axis=1)
    gate = lax.dot_general(kmean, q, (((1,), (1,)), ((), ())), precision=HIGHEST,
                           preferred_element_type=F32)
    blk_of_q = lax.broadcasted_iota(jnp.int32, (nblk, seq), 1) // MOBA_BLOCK
    row = lax.broadcasted_iota(jnp.int32, (nblk, seq), 0)
    past = row < blk_of_q
    better = jnp.zeros((nblk, seq), F32)
    for m in range(nblk):
        gm = gate[m:m + 1, :]
        past_m = blk_of_q[m:m + 1, :] > m
        beats = (gm > gate) | ((gm == gate) & (row > m))
        better = better + jnp.where(beats & past_m, 1.0, 0.0)
    sel = past & (better < float(MOBA_TOPK))

    qb = q.astype(BF16)
    kb = k.astype(BF16)
    vt = jnp.concatenate([v_ref[...].astype(F32).T, jnp.ones((DEN_ROWS, seq), F32)], axis=0).astype(BF16)
    key_i = lax.broadcasted_iota(jnp.int32, (MOBA_BLOCK, MOBA_BLOCK), 0)
    qry_i = lax.broadcasted_iota(jnp.int32, (MOBA_BLOCK, MOBA_BLOCK), 1)
    causal = key_i <= qry_i
    def blk(j):
        return slice(j * MOBA_BLOCK, (j + 1) * MOBA_BLOCK)

    def score_pass(i):
        mx = None
        for n in range(i + 1):
            s = lax.dot_general(kb[blk(n), :], qb[blk(i), :], (((1,), (1,)), ((), ())),
                                preferred_element_type=F32)
            s = jnp.where(causal if n == i else sel[n:n + 1, blk(i)], s, -jnp.inf)
            s_scr[i % 2, n] = s
            bm = jnp.max(s, axis=0, keepdims=True)
            mx = bm if mx is None else jnp.maximum(mx, bm)
            yield mx

    def value_pass(i, mx):
        acc = jnp.zeros((HEAD_DIM + DEN_ROWS, MOBA_BLOCK), F32)
        for n in range(i + 1):
            p = jnp.exp2(s_scr[i % 2, n] - mx)
            acc = acc + jnp.dot(vt[:, blk(n)], p.astype(BF16), preferred_element_type=F32)
            yield None
        den = acc[HEAD_DIM:HEAD_DIM + 1, :]
        o_ref[blk(i), :] = (acc[:HEAD_DIM, :] / den).T.astype(o_ref.dtype)
        yield None

    mx = list(score_pass(0))[-1]
    for i in range(nblk):
        values = value_pass(i, mx)
        scores = score_pass(i + 1) if i + 1 < nblk else iter(())
        for mx_next in scores:
            mx = mx_next
            next(values, None)
        for _ in values:
            pass


def _rope_tables(seq):
    half = ROPE_DIM // 2
    inv = jnp.power(ROPE_THETA, -jnp.arange(half, dtype=F32) * (2.0 / ROPE_DIM))
    ang = jnp.arange(seq, dtype=F32)[:, None] * inv[None, :]
    cos, sin = jnp.cos(ang), jnp.sin(ang)
    ones = jnp.ones((seq, HEAD_DIM - ROPE_DIM), F32)
    zeros = jnp.zeros((seq, HEAD_DIM - half), F32)
    cos_t = jnp.concatenate([cos, cos, ones], axis=1)
    sin_t = jnp.concatenate([-sin, sin, zeros[:, half:]], axis=1)
    src = jnp.arange(HEAD_DIM)[:, None]
    dst = jnp.arange(HEAD_DIM)[None, :]
    swap = ((dst < ROPE_DIM) & (src == jnp.where(dst < half, dst + half, dst - half))).astype(BF16)
    return cos_t, sin_t, swap


def _moba_attention(proj3, tables):
    b, seq, _ = proj3.shape
    head = lambda off: pl.BlockSpec((None, seq, HEAD_DIM), lambda bi, h: (bi, 0, off + h))
    tab = pl.BlockSpec((seq, HEAD_DIM), lambda bi, h: (0, 0))
    return pl.pallas_call(
        _moba_kernel,
        grid=(b, A_HEADS),
        in_specs=[head(0), head(A_HEADS), head(2 * A_HEADS), tab, tab,
                  pl.BlockSpec((HEAD_DIM, HEAD_DIM), lambda bi, h: (0, 0))],
        out_specs=pl.BlockSpec((None, seq, HEAD_DIM), lambda bi, h: (bi, 0, h)),
        out_shape=jax.ShapeDtypeStruct((b, seq, A_WIDTH), BF16),
        scratch_shapes=[pltpu.VMEM((2, seq // MOBA_BLOCK, MOBA_BLOCK, MOBA_BLOCK), F32)],
        compiler_params=_cparams(("arbitrary", "arbitrary")),
        name="moba_attention",
    )(proj3, proj3, proj3, *tables)


def _log_sigmoid(z):
    return jnp.minimum(z, 0.0) - jnp.log1p(jnp.exp(-jnp.abs(z)))


def _chunk_cumsum(x):
    pos = lax.broadcasted_iota(jnp.int32, x.shape, 0) % GLA_CHUNK
    shift = 1
    while shift < GLA_CHUNK:
        x = x + jnp.where(pos >= shift, pltpu.roll(x, shift, 0), 0.0)
        shift *= 2
    return x


def _gla_kernel(q_ref, k_ref, v_ref, og_ref, z_ref, gn_ref, o_ref,
                qd_scr, ki_scr, ke_scr, dec_scr, kv_scr, st_scr, oi_scr):
    seq = q_ref.shape[0]
    nc = seq // GLA_CHUNK
    log_a = _log_sigmoid(z_ref[...]) / GATE_TAU
    b = _chunk_cumsum(log_a)
    b3 = b.reshape(nc, GLA_CHUNK, G_KEY_DIM)
    b_last = b3[:, GLA_CHUNK - 1:GLA_CHUNK, :]
    k = k_ref[...].astype(F32)
    qd_scr[...] = ((q_ref[...].astype(F32) * (G_KEY_DIM ** -0.5)) * jnp.exp(b)).astype(BF16)
    ki_scr[...] = (k * jnp.exp(-b)).astype(BF16)
    ke3 = k.reshape(nc, GLA_CHUNK, G_KEY_DIM) * jnp.exp(b_last - b3)
    ke_scr[...] = ke3.reshape(seq, G_KEY_DIM).astype(BF16)
    dec_scr[...] = jnp.exp(b_last)
    group = GLA_GROUP * GLA_CHUNK
    t_i = lax.broadcasted_iota(jnp.int32, (group, group), 0)
    s_i = lax.broadcasted_iota(jnp.int32, (group, group), 1)
    keep = (s_i <= t_i) & ((s_i // GLA_CHUNK) == (t_i // GLA_CHUNK))
    gn = gn_ref[...]

    def chunk_rows(n):
        return pl.ds(pl.multiple_of(n * GLA_CHUNK, GLA_CHUNK), GLA_CHUNK)

    def intra(g, carry):
        rows = pl.ds(pl.multiple_of(g * group, group), group)
        vb = v_ref[rows, :].astype(BF16)
        ke = ke_scr[rows, :]
        att = lax.dot_general(qd_scr[rows, :], ki_scr[rows, :], (((1,), (1,)), ((), ())),
                              preferred_element_type=F32)
        att = jnp.where(keep, att, 0.0).astype(BF16)
        oi_scr[rows, :] = jnp.dot(att, vb, preferred_element_type=F32)
        for c in range(GLA_GROUP):
            cs = slice(c * GLA_CHUNK, (c + 1) * GLA_CHUNK)
            kv_scr[g * GLA_GROUP + c] = lax.dot_general(vb[cs, :], ke[cs, :], (((0,), (0,)), ((), ())),
                                                        preferred_element_type=F32)
        return carry

    lax.fori_loop(0, nc // GLA_GROUP, intra, 0, unroll=GLA_UNROLL // GLA_GROUP)

    def scan(n, state):
        st_scr[n] = state.astype(BF16)
        return state * dec_scr[n] + kv_scr[n]

    lax.fori_loop(0, nc, scan, jnp.zeros((G_VAL_DIM, G_KEY_DIM), F32), unroll=GLA_UNROLL)

    def inter(n, carry):
        rows = chunk_rows(n)
        o = oi_scr[rows, :] + lax.dot_general(qd_scr[rows, :], st_scr[n], (((1,), (1,)), ((), ())),
                                              preferred_element_type=F32)
        o = o * lax.rsqrt(jnp.mean(o * o, axis=-1, keepdims=True) + EPS) * gn
        og = og_ref[rows, :].astype(F32)
        o_ref[rows, :] = (o * (og * jax.nn.sigmoid(og))).astype(o_ref.dtype)
        return carry

    lax.fori_loop(0, nc, inter, 0, unroll=GLA_UNROLL)


GLA_UNROLL = 16
GLA_GROUP = 4


def _gla_mixer(proj3, z3, g_onorm):
    b, seq, _ = proj3.shape
    kq = 3 * A_WIDTH // G_KEY_DIM
    kv = (3 * A_WIDTH + 2 * G_KEY_WIDTH) // G_VAL_DIM
    key = lambda off: pl.BlockSpec((None, seq, G_KEY_DIM), lambda bi, h: (bi, 0, off + h))
    val = lambda off: pl.BlockSpec((None, seq, G_VAL_DIM), lambda bi, h: (bi, 0, off + h))
    return pl.pallas_call(
        _gla_kernel,
        grid=(b, G_HEADS),
        in_specs=[
            key(kq), key(kq + G_HEADS), val(kv), val(kv + G_HEADS),
            pl.BlockSpec((None, seq, G_KEY_DIM), lambda bi, h: (bi, 0, h)),
            pl.BlockSpec((1, G_VAL_DIM), lambda bi, h: (0, 0)),
        ],
        out_specs=pl.BlockSpec((None, seq, G_VAL_DIM), lambda bi, h: (bi, 0, h)),
        out_shape=jax.ShapeDtypeStruct((b, seq, G_VAL_WIDTH), BF16),
        scratch_shapes=[
            pltpu.VMEM((seq, G_KEY_DIM), BF16),
            pltpu.VMEM((seq, G_KEY_DIM), BF16),
            pltpu.VMEM((seq, G_KEY_DIM), BF16),
            pltpu.VMEM((seq // GLA_CHUNK, 1, G_KEY_DIM), F32),
            pltpu.VMEM((seq // GLA_CHUNK, G_VAL_DIM, G_KEY_DIM), F32),
            pltpu.VMEM((seq // GLA_CHUNK, G_VAL_DIM, G_KEY_DIM), BF16),
            pltpu.VMEM((seq, G_VAL_DIM), F32),
        ],
        compiler_params=_cparams(("arbitrary", "arbitrary")),
        name="gla_mixer",
    )(proj3, proj3, proj3, proj3, z3, g_onorm)


def _first_lane(mask, lane):
    return jnp.min(jnp.where(mask, lane, LANES), axis=-1, keepdims=True)


def _split_bf16(x):
    hi = x.astype(BF16)
    return hi, (x - hi.astype(F32)).astype(BF16)


def _outproj_router_kernel(oa_ref, ob_ref, x_ref, wa_ref, wb_ref, mod_ref, ln_ref, wr_ref, br_ref,
                           xo_ref, h_ref, ri_ref, rw_ref, cnt_ref, run_scr):
    tm = x_ref.shape[0]
    sub = tm // ROUTER_SUBTILES

    @pl.when(pl.program_id(0) == 0)
    def _():
        run_scr[...] = jnp.zeros_like(run_scr)

    lane = lax.broadcasted_iota(jnp.int32, (sub, LANES), 1)
    r_i = lax.broadcasted_iota(jnp.int32, (sub, sub), 0)
    c_i = lax.broadcasted_iota(jnp.int32, (sub, sub), 1)
    strict_lower = jnp.where(c_i < r_i, 1.0, 0.0).astype(BF16)
    state = {"run": run_scr[...]}

    def sub_tile(s):
        rows = slice(s * sub, (s + 1) * sub)
        mix = jnp.dot(oa_ref[rows, :], wa_ref[...], preferred_element_type=F32)
        mix = mix + jnp.dot(ob_ref[rows, :], wb_ref[...], preferred_element_type=F32)
        x_new = x_ref[rows, :] + mod_ref[2:3, :] * mix
        xo_ref[rows, :] = x_new
        h = _modulated_norm(x_new, ln_ref[...], mod_ref[3:4, :], mod_ref[4:5, :])
        h_ref[rows, :] = _pack_bf16_pairs(h)
        yield

        hh = jnp.concatenate(_split_bf16(h), axis=0)
        r = jnp.dot(hh, wr_ref[...], preferred_element_type=F32)
        logits = (r[:sub, :LANES] + r[sub:, :LANES]) + (r[:sub, LANES:] + r[sub:, LANES:]) + br_ref[...]
        is_grp = (lane >= N_EXPERTS) & (lane < N_EXPERTS + N_GROUPS)
        l1 = jnp.where(is_grp, logits, -jnp.inf)
        m1 = jnp.max(l1, axis=-1, keepdims=True)
        grp = _first_lane(l1 == m1, lane) - N_EXPERTS
        p_grp = 1.0 / jnp.sum(jnp.exp(l1 - m1), axis=-1, keepdims=True)
        in_grp = (lane < N_EXPERTS) & ((lane // EXPERTS_PER_GROUP) == grp)
        l2 = jnp.where(in_grp, logits, -jnp.inf)
        va = jnp.max(l2, axis=-1, keepdims=True)
        ia = _first_lane(l2 == va, lane)
        l2b = jnp.where(lane == ia, -jnp.inf, l2)
        vb = jnp.max(l2b, axis=-1, keepdims=True)
        ib = _first_lane(l2b == vb, lane)
        eb = jnp.exp(vb - va)
        wa = p_grp * (1.0 / (1.0 + eb))
        wb = p_grp * (eb / (1.0 + eb))
        rw_ref[rows, :] = jnp.where(lane == 0, wa, 0.0) + jnp.where(lane == 1, wb, 0.0)
        oh_a = jnp.where(lane == ia, 1.0, 0.0)
        oh_b = jnp.where(lane == ib, 1.0, 0.0)
        cnt = oh_a + oh_b
        within = jnp.dot(strict_lower, cnt.astype(BF16), preferred_element_type=F32)
        yield

        before = within + state["run"]
        state["run"] = state["run"] + jnp.sum(cnt, axis=0, keepdims=True)
        rank_a = jnp.sum(oh_a * before, axis=-1, keepdims=True)
        rank_b = jnp.sum(oh_b * before, axis=-1, keepdims=True)
        ri = jnp.where(lane == 0, ia.astype(F32), 0.0) + jnp.where(lane == 1, ib.astype(F32), 0.0)
        ri = ri + jnp.where(lane == 2, rank_a, 0.0) + jnp.where(lane == 3, rank_b, 0.0)
        ri_ref[:, rows] = ri.T[0:ROUTE_ROWS, :].astype(jnp.int32)
        yield

    stages = [sub_tile(s) for s in range(ROUTER_SUBTILES)]
    for _ in range(3):
        for g in stages:
            next(g)

    run_scr[...] = state["run"]
    cnt_ref[...] = jnp.broadcast_to(state["run"], cnt_ref.shape).astype(jnp.int32)


ROUTER_SUBTILES = 4


ROUTE_ROWS = 8
TOKEN_TILE = 512


def _outproj_router(oa2, ob2, x2, w_o, mod, ln, w_r, b_r, seq, layer):
    t, d = x2.shape
    tm = TOKEN_TILE
    row = lambda w: pl.BlockSpec((tm, w), lambda m: (m, 0))
    full = lambda a, c: pl.BlockSpec((a, c), lambda m: (0, 0))
    return pl.pallas_call(
        _outproj_router_kernel,
        grid=(t // tm,),
        in_specs=[
            row(A_WIDTH), row(G_VAL_WIDTH), row(d),
            pl.BlockSpec((None, A_WIDTH, d), lambda m: (layer, 0, 0)),
            pl.BlockSpec((None, G_VAL_WIDTH, d), lambda m: (layer, A_WIDTH // G_VAL_WIDTH, 0)),
            pl.BlockSpec((None, 6, d), lambda m: ((m * tm) // seq, 0, 0)),
            full(1, d), full(d, 2 * LANES), full(1, LANES),
        ],
        out_specs=[row(d), row(d // 2), pl.BlockSpec((None, ROUTE_ROWS, tm), lambda m: (m, 0, 0)),
                   row(LANES), full(8, LANES)],
        out_shape=[
            jax.ShapeDtypeStruct((t, d), F32),
            jax.ShapeDtypeStruct((t, d // 2), jnp.uint32),
            jax.ShapeDtypeStruct((t // tm, ROUTE_ROWS, tm), jnp.int32),
            jax.ShapeDtypeStruct((t, LANES), F32),
            jax.ShapeDtypeStruct((8, LANES), jnp.int32),
        ],
        scratch_shapes=[pltpu.VMEM((1, LANES), F32)],
        compiler_params=_cparams(("arbitrary",)),
        name="outproj_router",
    )(oa2, ob2, x2, w_o, w_o, mod, ln, w_r, b_r)


def _dispatch_kernel(dest_ref, h_ref, xg_in, xg_out, sem):
    del xg_in
    tm = h_ref.shape[0]

    for r in range(tm):
        for k in range(TOP_K):
            pltpu.make_async_copy(h_ref.at[pl.ds(r, 1), :],
                                  xg_out.at[pl.ds(dest_ref[0, k * tm + r], 1), :], sem).start(priority=r % 2)
    done = xg_out.at[pl.ds(0, TOP_K * tm), :]
    pltpu.make_async_copy(done, done, sem).wait()


def _dispatch(dest3, h2p, xg):
    t, dp = h2p.shape
    tm = dest3.shape[2] // TOP_K
    return pl.pallas_call(
        _dispatch_kernel,
        grid=(t // tm,),
        in_specs=[
            pl.BlockSpec((None, 1, TOP_K * tm), lambda m: (m, 0, 0), memory_space=pltpu.SMEM),
            pl.BlockSpec((tm, dp), lambda m: (m, 0)),
            pl.BlockSpec(memory_space=pl.ANY),
        ],
        out_specs=pl.BlockSpec(memory_space=pl.ANY),
        out_shape=jax.ShapeDtypeStruct(xg.shape, xg.dtype),
        input_output_aliases={2: 0},
        scratch_shapes=[pltpu.SemaphoreType.DMA(())],
        compiler_params=_cparams(("arbitrary",)),
        name="moe_dispatch",
    )(dest3, h2p, xg)


def _expert_kernel(cur_ref, nxt_ref, last_ref, nused_ref, x_ref, wg_ref, wu_ref, wd_ref,
                   y_ref, wg_res, wu_res, wd_res, xb_scr, acc_scr):
    del cur_ref, nxt_ref
    b, j = pl.program_id(0), pl.program_id(1)
    active = (b >= 1) & (b <= nused_ref[0])

    @pl.when(active & (j == 0))
    def _():
        lo, hi = _unpack_bf16_pairs(x_ref[...])
        half = lo.shape[1]
        xb_scr[:, :half] = lo.astype(BF16)
        xb_scr[:, half:] = hi.astype(BF16)

    for jj in range(EXPERT_HIDDEN_CHUNKS):
        @pl.when(active & (j == jj))
        def _(jj=jj):
            xb = xb_scr[...]
            g = jnp.dot(xb, wg_res[jj], preferred_element_type=F32)
            u = jnp.dot(xb, wu_res[jj], preferred_element_type=F32)
            a = ((g * jax.nn.sigmoid(g)) * u).astype(BF16)
            part = jnp.dot(a, wd_res[jj], preferred_element_type=F32)
            if jj == 0:
                acc_scr[...] = part
            elif jj < EXPERT_HIDDEN_CHUNKS - 1:
                acc_scr[...] += part
            else:
                y_ref[...] = _pack_bf16_pairs(acc_scr[...] + part)

    @pl.when((b > nused_ref[0]) & (j == 0))
    def _():
        y_ref[...] = jnp.zeros_like(y_ref)

    @pl.when(last_ref[b] == 1)
    def _():
        wg_res[j] = wg_ref[...].astype(BF16)
        wu_res[j] = wu_ref[...].astype(BF16)
        wd_res[j] = wd_ref[...].astype(BF16)


EXPERT_HIDDEN_CHUNKS = 2


def _expert_mlp(cur_e, nxt_e, last, n_used, xg, wg, wu, wd, layer):
    dp = xg.shape[1]
    nb = xg.shape[0] // ROW_BLOCK
    d, f = wg.shape[2], wg.shape[3]
    nj = EXPERT_HIDDEN_CHUNKS
    fc = f // nj

    def x_index(b, j, cur, nxt, lst, nu):
        return (jnp.clip(b - 1, 0, nu[0] - 1), 0)

    def y_index(b, j, cur, nxt, lst, nu):
        return (jnp.maximum(b - 1, 0), 0)

    def w_index(chunk_axis):
        def index(b, j, cur, nxt, lst, nu):
            e = jnp.where(lst[b] == 1, nxt[b], cur[b])
            jj = jnp.where(lst[b] == 1, j, nj - 1)
            return (layer, e, 0, jj) if chunk_axis == 2 else (layer, e, jj, 0)
        return index

    grid_spec = pltpu.PrefetchScalarGridSpec(
        num_scalar_prefetch=4,
        grid=(nb + 1, nj),
        in_specs=[
            pl.BlockSpec((ROW_BLOCK, dp), x_index),
            pl.BlockSpec((None, None, d, fc), w_index(2)),
            pl.BlockSpec((None, None, d, fc), w_index(2)),
            pl.BlockSpec((None, None, fc, d), w_index(1)),
        ],
        out_specs=pl.BlockSpec((ROW_BLOCK, dp), y_index),
        scratch_shapes=[
            pltpu.VMEM((nj, d, fc), BF16),
            pltpu.VMEM((nj, d, fc), BF16),
            pltpu.VMEM((nj, fc, d), BF16),
            pltpu.VMEM((ROW_BLOCK, d), BF16),
            pltpu.VMEM((ROW_BLOCK, d), F32),
        ],
    )
    return pl.pallas_call(
        _expert_kernel,
        grid_spec=grid_spec,
        out_shape=jax.ShapeDtypeStruct((nb * ROW_BLOCK, dp), jnp.uint32),
        compiler_params=_cparams(("arbitrary", "arbitrary"), EXPERT_VMEM_LIMIT),
        name="expert_mlp",
    )(cur_e, nxt_e, last, n_used, xg, wg, wu, wd)


def _wait_row_gather(src_hbm, dst_vmem, sem):
    pltpu.make_async_copy(src_hbm.at[pl.ds(0, dst_vmem.shape[0]), :], dst_vmem, sem).wait()


def _combine_kernel(dest_ref, y_hbm, rw_ref, x_ref, mod_ref, lnf_ref, o_ref, y_scr, sems, *, final):
    tm = x_ref.shape[0]
    i = pl.program_id(0)
    n_tiles = pl.num_programs(0) - COMBINE_LOOKAHEAD
    n_slots = COMBINE_LOOKAHEAD + 1

    def start_gather():
        slot = i % n_slots
        for r in range(2 * tm):
            pltpu.make_async_copy(y_hbm.at[pl.ds(dest_ref[0, r], 1), :],
                                  y_scr.at[slot, pl.ds(r, 1), :], sems.at[slot]).start(priority=r % 2)

    def combine(prefetch):
        slot = (i - COMBINE_LOOKAHEAD) % n_slots
        _wait_row_gather(y_hbm, y_scr.at[slot], sems.at[slot])
        rw = rw_ref[...]
        a_lo, a_hi = _unpack_bf16_pairs(y_scr[slot, 0:tm, :])
        b_lo, b_hi = _unpack_bf16_pairs(y_scr[slot, tm:2 * tm, :])
        y = jnp.concatenate([rw[:, 0:1] * a_lo + rw[:, 1:2] * b_lo,
                             rw[:, 0:1] * a_hi + rw[:, 1:2] * b_hi], axis=1)
        if prefetch:
            start_gather()
        x_new = x_ref[...] + mod_ref[5:6, :] * y
        if final:
            x_new = x_new * lax.rsqrt(jnp.mean(x_new * x_new, axis=-1, keepdims=True) + EPS) * lnf_ref[...]
        o_ref[...] = x_new

    pl.when(i < COMBINE_LOOKAHEAD)(start_gather)
    pl.when((i >= COMBINE_LOOKAHEAD) & (i < n_tiles))(functools.partial(combine, True))
    pl.when(i >= n_tiles)(functools.partial(combine, False))


COMBINE_LOOKAHEAD = 2


def _combine(dest3, yg, rw, x2, mod, ln_f, seq, final):
    t, d = x2.shape
    tm = dest3.shape[2] // 2
    nt = t // tm
    prev = lambda m: jnp.maximum(m - COMBINE_LOOKAHEAD, 0)
    return pl.pallas_call(
        functools.partial(_combine_kernel, final=final),
        grid=(nt + COMBINE_LOOKAHEAD,),
        in_specs=[
            pl.BlockSpec((None, 1, 2 * tm), lambda m: (jnp.minimum(m, nt - 1), 0, 0), memory_space=pltpu.SMEM),
            pl.BlockSpec(memory_space=pl.ANY),
            pl.BlockSpec((tm, LANES), lambda m: (prev(m), 0)),
            pl.BlockSpec((tm, d), lambda m: (prev(m), 0)),
            pl.BlockSpec((None, 6, d), lambda m: ((prev(m) * tm) // seq, 0, 0)),
            pl.BlockSpec((1, d), lambda m: (0, 0)),
        ],
        out_specs=pl.BlockSpec((tm, d), lambda m: (prev(m), 0)),
        out_shape=jax.ShapeDtypeStruct((t, d), F32),
        scratch_shapes=[pltpu.VMEM((COMBINE_LOOKAHEAD + 1, 2 * tm, yg.shape[1]), jnp.uint32),
                        pltpu.SemaphoreType.DMA((COMBINE_LOOKAHEAD + 1,))],
        compiler_params=_cparams(("arbitrary",)),
        name="moe_combine",
    )(dest3, yg, rw, x2, mod, ln_f)


def _routing_tables(ri, counts, t):
    n_blocks = (t * TOP_K) // ROW_BLOCK + N_EXPERTS
    cnt = counts[0, :N_EXPERTS]
    padded = (cnt + ROW_BLOCK - 1) // ROW_BLOCK * ROW_BLOCK
    ends = jnp.cumsum(padded)
    starts = ends - padded
    experts, ranks = ri[:, 0:TOP_K, :], ri[:, TOP_K:2 * TOP_K, :]
    offsets = jnp.sum(jnp.where(experts[..., None] == jnp.arange(N_EXPERTS), starts, 0), axis=-1)
    dest = (offsets + ranks).reshape(ri.shape[0], 1, TOP_K * ri.shape[2])
    n_used = (ends[-1] // ROW_BLOCK).astype(jnp.int32)
    blocks = jnp.minimum(jnp.arange(n_blocks + 1, dtype=jnp.int32), n_used - 1) * ROW_BLOCK
    block_exp = jnp.sum((blocks[:, None] >= ends[None, :]).astype(jnp.int32), axis=1)
    block_exp = jnp.minimum(block_exp, N_EXPERTS - 1)
    cur_e = jnp.concatenate([block_exp[:1], block_exp[:-1]])
    nxt_e = block_exp
    last = (cur_e != nxt_e).astype(jnp.int32).at[0].set(1)
    return dest.astype(jnp.int32), cur_e, nxt_e, last, n_used.reshape(1)


def kernel(x, c, ln1, ln2, w_ada, b_ada, w_in, w_gk, b_gk, g_onorm, w_out,
           w_r1, b_r1, w_r2, b_r2, w_e_gate, w_e_up, w_e_down, ln_f):
    b, seq, d = x.shape
    depth = w_ada.shape[0]
    t = b * seq
    mod_all = _adaln_mod(c, w_ada, b_ada).reshape(depth, b, 6, d)
    tables = _rope_tables(seq)
    x2 = x.reshape(t, d)
    xg = jnp.zeros(((t * TOP_K // ROW_BLOCK + N_EXPERTS) * ROW_BLOCK, d // 2), jnp.uint32)
    w_in_b = w_in.astype(BF16)
    w_o = w_out.astype(BF16)
    for l in range(depth):
        mod = mod_all[l]
        w_ga = jnp.pad(w_in_b[l][:, PROJ_MAIN:], ((0, 0), (0, LANES - GATE_RANK)))
        w_gk_p = jnp.pad(w_gk[l], ((0, LANES - GATE_RANK), (0, 0)))
        proj, z = _in_projection(x2, ln1[l][None, :], mod, w_in_b, w_ga, w_gk_p, b_gk[l][None, :], seq, l)
        proj3 = proj.reshape(b, seq, PROJ_MAIN)
        oa = _moba_attention(proj3, tables)
        ob = _gla_mixer(proj3, z.reshape(b, seq, G_KEY_WIDTH), g_onorm[l][None, :])
        w_r = jnp.pad(jnp.concatenate([w_r2[l], w_r1[l]], axis=1), ((0, 0), (0, LANES - N_EXPERTS - N_GROUPS)))
        w_r_hi = w_r.astype(BF16)
        w_r = jnp.concatenate([w_r_hi, (w_r - w_r_hi.astype(F32)).astype(BF16)], axis=1)
        b_r = jnp.pad(jnp.concatenate([b_r2[l], b_r1[l]]), (0, LANES - N_EXPERTS - N_GROUPS))[None, :]
        x2, h2, ri, rw, counts = _outproj_router(
            oa.reshape(t, A_WIDTH), ob.reshape(t, G_VAL_WIDTH), x2, w_o, mod, ln2[l][None, :], w_r, b_r, seq, l)
        dest, cur_e, nxt_e, last, n_used = _routing_tables(ri, counts, t)
        xg = _dispatch(dest, h2, xg)
        yg = _expert_mlp(cur_e, nxt_e, last, n_used, xg, w_e_gate, w_e_up, w_e_down, l)
        x2 = _combine(dest, yg, rw, x2, mod, ln_f[None, :], seq, final=(l == depth - 1))
    return x2.reshape(b, seq, d)
```

```python
import functools

import jax
import jax.numpy as jnp
import numpy as np
from jax import lax
from jax.experimental import pallas as pl
from jax.experimental.pallas import tpu as pltpu

F32 = jnp.float32
BF16 = jnp.bfloat16
HIGHEST = lax.Precision.HIGHEST

HEAD_DIM = 128
A_HEADS = 8
A_WIDTH = A_HEADS * HEAD_DIM
MOBA_BLOCK = 256
MOBA_TOPK = 3
ROPE_THETA = 500000.0
ROPE_DIM = HEAD_DIM // 4
ATTN_SCALE = HEAD_DIM ** -0.5
G_HEADS = 4
G_VAL_DIM = 256
G_KEY_DIM = 128
G_KEY_WIDTH = G_HEADS * G_KEY_DIM
G_VAL_WIDTH = G_HEADS * G_VAL_DIM
GATE_RANK = 16
GATE_TAU = 16.0
GLA_CHUNK = 64
N_GROUPS = 4
EXPERTS_PER_GROUP = 8
N_EXPERTS = N_GROUPS * EXPERTS_PER_GROUP
TOP_K = 2
EPS = 1e-6

LANES = 128
PROJ_MAIN = 3 * A_WIDTH + 2 * G_KEY_WIDTH + 2 * G_VAL_WIDTH
ROW_BLOCK = 512
VMEM_LIMIT = 52 * 1024 * 1024
EXPERT_VMEM_LIMIT = 58 * 1024 * 1024


def _cparams(sem, vmem_limit=VMEM_LIMIT, flags=None):
    return pltpu.CompilerParams(dimension_semantics=sem, vmem_limit_bytes=vmem_limit, flags=flags)


def _pack_bf16_pairs(x):
    c = x.shape[1] // 2
    lo = pltpu.bitcast(x[:, :c].astype(BF16).astype(F32), jnp.uint32)
    hi = pltpu.bitcast(x[:, c:].astype(BF16).astype(F32), jnp.uint32)
    return (lo >> 16) | hi


def _unpack_bf16_pairs(w):
    lo = pltpu.bitcast(w << 16, F32)
    hi = pltpu.bitcast(w & jnp.uint32(0xFFFF0000), F32)
    return lo, hi


def _mod_kernel(c_ref, w_ref, b_ref, o_ref):
    c = c_ref[...]
    nb = c.shape[0]
    cc = jnp.concatenate(_split_bf16(c * jax.nn.sigmoid(c)), axis=0)
    w_hi, w_lo = _split_bf16(w_ref[...])
    r = jnp.dot(cc, w_hi, preferred_element_type=F32) + jnp.dot(cc, w_lo, preferred_element_type=F32)
    o_ref[...] = (r[:nb] + r[nb:]) + b_ref[...]


def _adaln_mod(c, w_ada, b_ada):
    depth, d, n6 = w_ada.shape
    b = c.shape[0]
    tn = 1024
    return pl.pallas_call(
        _mod_kernel,
        grid=(depth, n6 // tn),
        in_specs=[
            pl.BlockSpec((b, d), lambda l, n: (0, 0)),
            pl.BlockSpec((None, d, tn), lambda l, n: (l, 0, n)),
            pl.BlockSpec((None, 1, tn), lambda l, n: (l, 0, n)),
        ],
        out_specs=pl.BlockSpec((None, b, tn), lambda l, n: (l, 0, n)),
        out_shape=jax.ShapeDtypeStruct((depth, b, n6), F32),
        compiler_params=_cparams(("arbitrary", "arbitrary")),
        name="adaln_mod",
    )(c, w_ada, b_ada.reshape(depth, 1, n6))


def _modulated_norm(x, g, shift, scale):
    y = x * lax.rsqrt(jnp.mean(x * x, axis=-1, keepdims=True) + EPS) * g
    return y * (1.0 + scale) + shift


def _inproj_kernel(x_ref, ln_ref, mod_ref, w_ref, wga_ref, wgk_ref, bgk_ref, proj_ref, z_ref, h_scr):
    @pl.when(pl.program_id(1) == 0)
    def _():
        h = _modulated_norm(x_ref[...], ln_ref[...], mod_ref[0:1, :], mod_ref[1:2, :])
        hb = h.astype(BF16)
        h_scr[...] = hb
        ga = jnp.dot(hb, wga_ref[...], preferred_element_type=F32)
        z_ref[...] = jnp.dot(ga, wgk_ref[...], precision=HIGHEST, preferred_element_type=F32) + bgk_ref[...]

    proj_ref[...] = jnp.dot(h_scr[...], w_ref[...], preferred_element_type=F32).astype(proj_ref.dtype)


def _in_projection(x2, ln, mod, w_main, w_ga, w_gk, b_gk, seq, layer):
    t, d = x2.shape
    tm, tn = 1024, 1024
    return pl.pallas_call(
        _inproj_kernel,
        grid=(t // tm, PROJ_MAIN // tn),
        in_specs=[
            pl.BlockSpec((tm, d), lambda m, n: (m, 0)),
            pl.BlockSpec((1, d), lambda m, n: (0, 0)),
            pl.BlockSpec((None, 6, d), lambda m, n: ((m * tm) // seq, 0, 0)),
            pl.BlockSpec((None, d, tn), lambda m, n: (layer, 0, n)),
            pl.BlockSpec((d, LANES), lambda m, n: (0, 0)),
            pl.BlockSpec((LANES, G_KEY_WIDTH), lambda m, n: (0, 0)),
            pl.BlockSpec((1, G_KEY_WIDTH), lambda m, n: (0, 0)),
        ],
        out_specs=[
            pl.BlockSpec((tm, tn), lambda m, n: (m, n)),
            pl.BlockSpec((tm, G_KEY_WIDTH), lambda m, n: (m, 0)),
        ],
        out_shape=[
            jax.ShapeDtypeStruct((t, PROJ_MAIN), BF16),
            jax.ShapeDtypeStruct((t, G_KEY_WIDTH), F32),
        ],
        scratch_shapes=[pltpu.VMEM((tm, d), BF16)],
        compiler_params=_cparams(("arbitrary", "arbitrary")),
        name="in_projection",
    )(x2, ln, mod, w_main, w_ga, w_gk, b_gk)


def _rope(xb, cos, sin, swap):
    partner = jnp.dot(xb, swap, preferred_element_type=F32)
    return xb.astype(F32) * cos + partner * sin


LOG2E = 1.4426950408889634
DEN_ROWS = 16


def _moba_kernel(q_ref, k_ref, v_ref, cos_ref, sin_ref, swap_ref, o_ref, s_scr):
    streams = [_moba_head(hd, q_ref, k_ref, v_ref, cos_ref, sin_ref, swap_ref, o_ref, s_scr)
               for hd in range(MOBA_HEADS_PER_STEP)]
    while streams:
        streams = [g for g in streams if next(g, "done") != "done"]


MOBA_HEADS_PER_STEP = 2


def _moba_head(hd, q_ref, k_ref, v_ref, cos_ref, sin_ref, swap_ref, o_ref, s_scr):
    seq = q_ref.shape[0]
    nblk = seq // MOBA_BLOCK
    lanes = slice(hd * HEAD_DIM, (hd + 1) * HEAD_DIM)
    cos, sin, swap = cos_ref[...], sin_ref[...], swap_ref[...]
    q = _rope(q_ref[:, lanes], cos, sin, swap) * (ATTN_SCALE * LOG2E)
    k = _rope(k_ref[:, lanes], cos, sin, swap)
    yield
    kmean = jnp.mean(k.reshape(nblk, MOBA_BLOCK, HEAD_DIM), axis=1)
    gate = lax.dot_general(kmean, q, (((1,), (1,)), ((), ())), precision=HIGHEST,
                           preferred_element_type=F32)
    blk_of_q = lax.broadcasted_iota(jnp.int32, (nblk, seq), 1) // MOBA_BLOCK
    row = lax.broadcasted_iota(jnp.int32, (nblk, seq), 0)
    past = row < blk_of_q
    better = jnp.zeros((nblk, seq), F32)
    for m in range(nblk):
        gm = gate[m:m + 1, :]
        past_m = blk_of_q[m:m + 1, :] > m
        beats = (gm > gate) | ((gm == gate) & (row > m))
        better = better + jnp.where(beats & past_m, 1.0, 0.0)
    sel = past & (better < float(MOBA_TOPK))
    yield

    qb = q.astype(BF16)
    kb = k.astype(BF16)
    vt = jnp.concatenate([v_ref[:, lanes].astype(F32).T, jnp.ones((DEN_ROWS, seq), F32)], axis=0).astype(BF16)
    yield
    key_i = lax.broadcasted_iota(jnp.int32, (MOBA_BLOCK, MOBA_BLOCK), 0)
    qry_i = lax.broadcasted_iota(jnp.int32, (MOBA_BLOCK, MOBA_BLOCK), 1)
    causal = key_i <= qry_i
    def blk(j):
        return slice(j * MOBA_BLOCK, (j + 1) * MOBA_BLOCK)

    def score_pass(i):
        mx = None
        for n in range(i + 1):
            s = lax.dot_general(kb[blk(n), :], qb[blk(i), :], (((1,), (1,)), ((), ())),
                                preferred_element_type=F32)
            s = jnp.where(causal if n == i else sel[n:n + 1, blk(i)], s, -jnp.inf)
            s_scr[hd, i % 2, n] = s
            bm = jnp.max(s, axis=0, keepdims=True)
            mx = bm if mx is None else jnp.maximum(mx, bm)
            yield mx

    def value_pass(i, mx):
        acc = jnp.zeros((HEAD_DIM + DEN_ROWS, MOBA_BLOCK), F32)
        for n in range(i + 1):
            p = jnp.exp2(s_scr[hd, i % 2, n] - mx)
            acc = acc + jnp.dot(vt[:, blk(n)], p.astype(BF16), preferred_element_type=F32)
            yield None
        den = acc[HEAD_DIM:HEAD_DIM + 1, :]
        o_ref[blk(i), lanes] = (acc[:HEAD_DIM, :] / den).T.astype(o_ref.dtype)
        yield None

    mx = list(score_pass(0))[-1]
    for i in range(nblk):
        values = value_pass(i, mx)
        scores = score_pass(i + 1) if i + 1 < nblk else iter(())
        for mx_next in scores:
            mx = mx_next
            next(values, None)
            yield
        for _ in values:
            yield


def _rope_tables(seq):
    half = ROPE_DIM // 2
    inv = jnp.power(ROPE_THETA, -jnp.arange(half, dtype=F32) * (2.0 / ROPE_DIM))
    ang = jnp.arange(seq, dtype=F32)[:, None] * inv[None, :]
    cos, sin = jnp.cos(ang), jnp.sin(ang)
    ones = jnp.ones((seq, HEAD_DIM - ROPE_DIM), F32)
    zeros = jnp.zeros((seq, HEAD_DIM - half), F32)
    cos_t = jnp.concatenate([cos, cos, ones], axis=1)
    sin_t = jnp.concatenate([-sin, sin, zeros[:, half:]], axis=1)
    src = jnp.arange(HEAD_DIM)[:, None]
    dst = jnp.arange(HEAD_DIM)[None, :]
    swap = ((dst < ROPE_DIM) & (src == jnp.where(dst < half, dst + half, dst - half))).astype(BF16)
    return cos_t, sin_t, swap


def _moba_attention(proj3, tables):
    b, seq, _ = proj3.shape
    hps = MOBA_HEADS_PER_STEP
    steps = A_HEADS // hps
    head = lambda off: pl.BlockSpec((None, seq, hps * HEAD_DIM), lambda bi, h: (bi, 0, off + h))
    tab = pl.BlockSpec((seq, HEAD_DIM), lambda bi, h: (0, 0))
    return pl.pallas_call(
        _moba_kernel,
        grid=(b, steps),
        in_specs=[head(0), head(steps), head(2 * steps), tab, tab,
                  pl.BlockSpec((HEAD_DIM, HEAD_DIM), lambda bi, h: (0, 0))],
        out_specs=pl.BlockSpec((None, seq, hps * HEAD_DIM), lambda bi, h: (bi, 0, h)),
        out_shape=jax.ShapeDtypeStruct((b, seq, A_WIDTH), BF16),
        scratch_shapes=[pltpu.VMEM((hps, 2, seq // MOBA_BLOCK, MOBA_BLOCK, MOBA_BLOCK), F32)],
        compiler_params=_cparams(("arbitrary", "arbitrary")),
        name="moba_attention",
    )(proj3, proj3, proj3, *tables)


def _log_sigmoid(z):
    return jnp.minimum(z, 0.0) - jnp.log1p(jnp.exp(-jnp.abs(z)))


def _chunk_cumsum(x):
    pos = lax.broadcasted_iota(jnp.int32, x.shape, 0) % GLA_CHUNK
    shift = 1
    while shift < GLA_CHUNK:
        x = x + jnp.where(pos >= shift, pltpu.roll(x, shift, 0), 0.0)
        shift *= 2
    return x


def _gla_kernel(q_ref, k_ref, v_ref, og_ref, z_ref, gn_ref, o_ref,
                qd_scr, ki_scr, ke_scr, dec_scr, kv_scr, st_scr, oi_scr):
    seq = q_ref.shape[0]
    nc = seq // GLA_CHUNK
    log_a = _log_sigmoid(z_ref[...]) / GATE_TAU
    b = _chunk_cumsum(log_a)
    b3 = b.reshape(nc, GLA_CHUNK, G_KEY_DIM)
    b_last = b3[:, GLA_CHUNK - 1:GLA_CHUNK, :]
    k = k_ref[...].astype(F32)
    qd_scr[...] = ((q_ref[...].astype(F32) * (G_KEY_DIM ** -0.5)) * jnp.exp(b)).astype(BF16)
    ki_scr[...] = (k * jnp.exp(-b)).astype(BF16)
    ke3 = k.reshape(nc, GLA_CHUNK, G_KEY_DIM) * jnp.exp(b_last - b3)
    ke_scr[...] = ke3.reshape(seq, G_KEY_DIM).astype(BF16)
    dec_scr[...] = jnp.exp(b_last)
    group = GLA_GROUP * GLA_CHUNK
    t_i = lax.broadcasted_iota(jnp.int32, (group, group), 0)
    s_i = lax.broadcasted_iota(jnp.int32, (group, group), 1)
    keep = (s_i <= t_i) & ((s_i // GLA_CHUNK) == (t_i // GLA_CHUNK))
    gn = gn_ref[...]

    def chunk_rows(n):
        return pl.ds(pl.multiple_of(n * GLA_CHUNK, GLA_CHUNK), GLA_CHUNK)

    def intra(g, carry):
        rows = pl.ds(pl.multiple_of(g * group, group), group)
        vb = v_ref[rows, :].astype(BF16)
        ke = ke_scr[rows, :]
        att = lax.dot_general(qd_scr[rows, :], ki_scr[rows, :], (((1,), (1,)), ((), ())),
                              preferred_element_type=F32)
        att = jnp.where(keep, att, 0.0).astype(BF16)
        oi_scr[rows, :] = jnp.dot(att, vb, preferred_element_type=F32)
        for c in range(GLA_GROUP):
            cs = slice(c * GLA_CHUNK, (c + 1) * GLA_CHUNK)
            kv_scr[g * GLA_GROUP + c] = lax.dot_general(vb[cs, :], ke[cs, :], (((0,), (0,)), ((), ())),
                                                        preferred_element_type=F32)
        return carry

    lax.fori_loop(0, nc // GLA_GROUP, intra, 0, unroll=GLA_UNROLL // GLA_GROUP)

    def scan(n, state):
        st_scr[n] = state.astype(BF16)
        return state * dec_scr[n] + kv_scr[n]

    lax.fori_loop(0, nc, scan, jnp.zeros((G_VAL_DIM, G_KEY_DIM), F32), unroll=GLA_UNROLL)

    def inter(n, carry):
        rows = chunk_rows(n)
        o = oi_scr[rows, :] + lax.dot_general(qd_scr[rows, :], st_scr[n], (((1,), (1,)), ((), ())),
                                              preferred_element_type=F32)
        o = o * lax.rsqrt(jnp.mean(o * o, axis=-1, keepdims=True) + EPS) * gn
        og = og_ref[rows, :].astype(F32)
        o_ref[rows, :] = (o * (og * jax.nn.sigmoid(og))).astype(o_ref.dtype)
        return carry

    lax.fori_loop(0, nc, inter, 0, unroll=GLA_UNROLL)


GLA_UNROLL = 16
GLA_GROUP = 4


def _gla_mixer(proj3, z3, g_onorm):
    b, seq, _ = proj3.shape
    kq = 3 * A_WIDTH // G_KEY_DIM
    kv = (3 * A_WIDTH + 2 * G_KEY_WIDTH) // G_VAL_DIM
    key = lambda off: pl.BlockSpec((None, seq, G_KEY_DIM), lambda bi, h: (bi, 0, off + h))
    val = lambda off: pl.BlockSpec((None, seq, G_VAL_DIM), lambda bi, h: (bi, 0, off + h))
    return pl.pallas_call(
        _gla_kernel,
        grid=(b, G_HEADS),
        in_specs=[
            key(kq), key(kq + G_HEADS), val(kv), val(kv + G_HEADS),
            pl.BlockSpec((None, seq, G_KEY_DIM), lambda bi, h: (bi, 0, h)),
            pl.BlockSpec((1, G_VAL_DIM), lambda bi, h: (0, 0)),
        ],
        out_specs=pl.BlockSpec((None, seq, G_VAL_DIM), lambda bi, h: (bi, 0, h)),
        out_shape=jax.ShapeDtypeStruct((b, seq, G_VAL_WIDTH), BF16),
        scratch_shapes=[
            pltpu.VMEM((seq, G_KEY_DIM), BF16),
            pltpu.VMEM((seq, G_KEY_DIM), BF16),
            pltpu.VMEM((seq, G_KEY_DIM), BF16),
            pltpu.VMEM((seq // GLA_CHUNK, 1, G_KEY_DIM), F32),
            pltpu.VMEM((seq // GLA_CHUNK, G_VAL_DIM, G_KEY_DIM), F32),
            pltpu.VMEM((seq // GLA_CHUNK, G_VAL_DIM, G_KEY_DIM), BF16),
            pltpu.VMEM((seq, G_VAL_DIM), F32),
        ],
        compiler_params=_cparams(("arbitrary", "arbitrary")),
        name="gla_mixer",
    )(proj3, proj3, proj3, proj3, z3, g_onorm)


def _first_lane(mask, lane):
    return jnp.min(jnp.where(mask, lane, LANES), axis=-1, keepdims=True)


def _split_bf16(x):
    hi = x.astype(BF16)
    return hi, (x - hi.astype(F32)).astype(BF16)


def _outproj_router_kernel(oa_ref, ob_ref, x_ref, wa_ref, wb_ref, mod_ref, ln_ref, wr_ref, br_ref,
                           xo_ref, h_ref, ri_ref, rw_ref, cnt_ref, run_scr):
    tm = x_ref.shape[0]
    sub = tm // ROUTER_SUBTILES

    @pl.when(pl.program_id(0) == 0)
    def _():
        run_scr[...] = jnp.zeros_like(run_scr)

    lane = lax.broadcasted_iota(jnp.int32, (sub, LANES), 1)
    r_i = lax.broadcasted_iota(jnp.int32, (sub, sub), 0)
    c_i = lax.broadcasted_iota(jnp.int32, (sub, sub), 1)
    strict_lower = jnp.where(c_i < r_i, 1.0, 0.0).astype(BF16)
    state = {"run": run_scr[...]}

    def sub_tile(s):
        rows = slice(s * sub, (s + 1) * sub)
        mix = jnp.dot(oa_ref[rows, :], wa_ref[...], preferred_element_type=F32)
        mix = mix + jnp.dot(ob_ref[rows, :], wb_ref[...], preferred_element_type=F32)
        x_new = x_ref[rows, :] + mod_ref[2:3, :] * mix
        xo_ref[rows, :] = x_new
        h = _modulated_norm(x_new, ln_ref[...], mod_ref[3:4, :], mod_ref[4:5, :])
        h_ref[rows, :] = _pack_bf16_pairs(h)
        yield

        hh = jnp.concatenate(_split_bf16(h), axis=0)
        r = jnp.dot(hh, wr_ref[...], preferred_element_type=F32)
        logits = (r[:sub, :LANES] + r[sub:, :LANES]) + (r[:sub, LANES:] + r[sub:, LANES:]) + br_ref[...]
        is_grp = (lane >= N_EXPERTS) & (lane < N_EXPERTS + N_GROUPS)
        l1 = jnp.where(is_grp, logits, -jnp.inf)
        m1 = jnp.max(l1, axis=-1, keepdims=True)
        grp = _first_lane(l1 == m1, lane) - N_EXPERTS
        p_grp = 1.0 / jnp.sum(jnp.exp(l1 - m1), axis=-1, keepdims=True)
        in_grp = (lane < N_EXPERTS) & ((lane // EXPERTS_PER_GROUP) == grp)
        l2 = jnp.where(in_grp, logits, -jnp.inf)
        va = jnp.max(l2, axis=-1, keepdims=True)
        ia = _first_lane(l2 == va, lane)
        l2b = jnp.where(lane == ia, -jnp.inf, l2)
        vb = jnp.max(l2b, axis=-1, keepdims=True)
        ib = _first_lane(l2b == vb, lane)
        eb = jnp.exp(vb - va)
        wa = p_grp * (1.0 / (1.0 + eb))
        wb = p_grp * (eb / (1.0 + eb))
        rw_ref[rows, :] = jnp.where(lane == 0, wa, 0.0) + jnp.where(lane == 1, wb, 0.0)
        oh_a = jnp.where(lane == ia, 1.0, 0.0)
        oh_b = jnp.where(lane == ib, 1.0, 0.0)
        cnt = oh_a + oh_b
        within = jnp.dot(strict_lower, cnt.astype(BF16), preferred_element_type=F32)
        yield

        before = within + state["run"]
        state["run"] = state["run"] + jnp.sum(cnt, axis=0, keepdims=True)
        rank_a = jnp.sum(oh_a * before, axis=-1, keepdims=True)
        rank_b = jnp.sum(oh_b * before, axis=-1, keepdims=True)
        ri = jnp.where(lane == 0, ia.astype(F32), 0.0) + jnp.where(lane == 1, ib.astype(F32), 0.0)
        ri = ri + jnp.where(lane == 2, rank_a, 0.0) + jnp.where(lane == 3, rank_b, 0.0)
        ri_ref[:, rows] = ri.T[0:ROUTE_ROWS, :].astype(jnp.int32)
        yield

    stages = [sub_tile(s) for s in range(ROUTER_SUBTILES)]
    for _ in range(3):
        for g in stages:
            next(g)

    run_scr[...] = state["run"]
    cnt_ref[...] = jnp.broadcast_to(state["run"], cnt_ref.shape).astype(jnp.int32)


ROUTER_SUBTILES = 4


ROUTE_ROWS = 8
TOKEN_TILE = 512


def _outproj_router(oa2, ob2, x2, w_o, mod, ln, w_r, b_r, seq, layer):
    t, d = x2.shape
    tm = TOKEN_TILE
    row = lambda w: pl.BlockSpec((tm, w), lambda m: (m, 0))
    full = lambda a, c: pl.BlockSpec((a, c), lambda m: (0, 0))
    return pl.pallas_call(
        _outproj_router_kernel,
        grid=(t // tm,),
        in_specs=[
            row(A_WIDTH), row(G_VAL_WIDTH), row(d),
            pl.BlockSpec((None, A_WIDTH, d), lambda m: (layer, 0, 0)),
            pl.BlockSpec((None, G_VAL_WIDTH, d), lambda m: (layer, A_WIDTH // G_VAL_WIDTH, 0)),
            pl.BlockSpec((None, 6, d), lambda m: ((m * tm) // seq, 0, 0)),
            full(1, d), full(d, 2 * LANES), full(1, LANES),
        ],
        out_specs=[row(d), row(d // 2), pl.BlockSpec((None, ROUTE_ROWS, tm), lambda m: (m, 0, 0)),
                   row(LANES), full(8, LANES)],
        out_shape=[
            jax.ShapeDtypeStruct((t, d), F32),
            jax.ShapeDtypeStruct((t, d // 2), jnp.uint32),
            jax.ShapeDtypeStruct((t // tm, ROUTE_ROWS, tm), jnp.int32),
            jax.ShapeDtypeStruct((t, LANES), F32),
            jax.ShapeDtypeStruct((8, LANES), jnp.int32),
        ],
        scratch_shapes=[pltpu.VMEM((1, LANES), F32)],
        compiler_params=_cparams(("arbitrary",)),
        name="outproj_router",
    )(oa2, ob2, x2, w_o, w_o, mod, ln, w_r, b_r)


def _dispatch_kernel(dest_ref, h_ref, xg_in, xg_out, sem):
    del xg_in
    tm = h_ref.shape[0]

    for r in range(tm):
        for k in range(TOP_K):
            pltpu.make_async_copy(h_ref.at[pl.ds(r, 1), :],
                                  xg_out.at[pl.ds(dest_ref[0, k * tm + r], 1), :], sem).start(priority=r % 2)
    done = xg_out.at[pl.ds(0, TOP_K * tm), :]
    pltpu.make_async_copy(done, done, sem).wait()


def _dispatch(dest3, h2p, xg):
    t, dp = h2p.shape
    tm = dest3.shape[2] // TOP_K
    return pl.pallas_call(
        _dispatch_kernel,
        grid=(t // tm,),
        in_specs=[
            pl.BlockSpec((None, 1, TOP_K * tm), lambda m: (m, 0, 0), memory_space=pltpu.SMEM),
            pl.BlockSpec((tm, dp), lambda m: (m, 0)),
            pl.BlockSpec(memory_space=pl.ANY),
        ],
        out_specs=pl.BlockSpec(memory_space=pl.ANY),
        out_shape=jax.ShapeDtypeStruct(xg.shape, xg.dtype),
        input_output_aliases={2: 0},
        scratch_shapes=[pltpu.SemaphoreType.DMA(())],
        compiler_params=_cparams(("arbitrary",)),
        name="moe_dispatch",
    )(dest3, h2p, xg)


def _expert_kernel(cur_ref, nxt_ref, last_ref, nused_ref, x_ref, wg_ref, wu_ref, wd_ref,
                   y_ref, wg_res, wu_res, wd_res, xb_scr, acc_scr):
    del cur_ref, nxt_ref
    b, j = pl.program_id(0), pl.program_id(1)
    active = (b >= 1) & (b <= nused_ref[0])

    @pl.when(active & (j == 0))
    def _():
        lo, hi = _unpack_bf16_pairs(x_ref[...])
        half = lo.shape[1]
        xb_scr[:, :half] = lo.astype(BF16)
        xb_scr[:, half:] = hi.astype(BF16)

    for jj in range(EXPERT_HIDDEN_CHUNKS):
        @pl.when(active & (j == jj))
        def _(jj=jj):
            xb = xb_scr[...]
            g = jnp.dot(xb, wg_res[jj], preferred_element_type=F32)
            u = jnp.dot(xb, wu_res[jj], preferred_element_type=F32)
            a = ((g * jax.nn.sigmoid(g)) * u).astype(BF16)
            part = jnp.dot(a, wd_res[jj], preferred_element_type=F32)
            if jj == 0:
                acc_scr[...] = part
            elif jj < EXPERT_HIDDEN_CHUNKS - 1:
                acc_scr[...] += part
            else:
                y_ref[...] = _pack_bf16_pairs(acc_scr[...] + part)

    @pl.when((b > nused_ref[0]) & (j == 0))
    def _():
        y_ref[...] = jnp.zeros_like(y_ref)

    @pl.when(last_ref[b] == 1)
    def _():
        wg_res[j] = wg_ref[...].astype(BF16)
        wu_res[j] = wu_ref[...].astype(BF16)
        wd_res[j] = wd_ref[...].astype(BF16)


EXPERT_HIDDEN_CHUNKS = 2


def _expert_mlp(cur_e, nxt_e, last, n_used, xg, wg, wu, wd, layer):
    dp = xg.shape[1]
    nb = xg.shape[0] // ROW_BLOCK
    d, f = wg.shape[2], wg.shape[3]
    nj = EXPERT_HIDDEN_CHUNKS
    fc = f // nj

    def x_index(b, j, cur, nxt, lst, nu):
        return (jnp.clip(b - 1, 0, nu[0] - 1), 0)

    def y_index(b, j, cur, nxt, lst, nu):
        return (jnp.maximum(b - 1, 0), 0)

    def w_index(chunk_axis):
        def index(b, j, cur, nxt, lst, nu):
            e = jnp.where(lst[b] == 1, nxt[b], cur[b])
            jj = jnp.where(lst[b] == 1, j, nj - 1)
            return (layer, e, 0, jj) if chunk_axis == 2 else (layer, e, jj, 0)
        return index

    grid_spec = pltpu.PrefetchScalarGridSpec(
        num_scalar_prefetch=4,
        grid=(nb + 1, nj),
        in_specs=[
            pl.BlockSpec((ROW_BLOCK, dp), x_index),
            pl.BlockSpec((None, None, d, fc), w_index(2)),
            pl.BlockSpec((None, None, d, fc), w_index(2)),
            pl.BlockSpec((None, None, fc, d), w_index(1)),
        ],
        out_specs=pl.BlockSpec((ROW_BLOCK, dp), y_index),
        scratch_shapes=[
            pltpu.VMEM((nj, d, fc), BF16),
            pltpu.VMEM((nj, d, fc), BF16),
            pltpu.VMEM((nj, fc, d), BF16),
            pltpu.VMEM((ROW_BLOCK, d), BF16),
            pltpu.VMEM((ROW_BLOCK, d), F32),
        ],
    )
    return pl.pallas_call(
        _expert_kernel,
        grid_spec=grid_spec,
        out_shape=jax.ShapeDtypeStruct((nb * ROW_BLOCK, dp), jnp.uint32),
        compiler_params=_cparams(("arbitrary", "arbitrary"), EXPERT_VMEM_LIMIT),
        name="expert_mlp",
    )(cur_e, nxt_e, last, n_used, xg, wg, wu, wd)


def _wait_row_gather(src_hbm, dst_vmem, sem):
    pltpu.make_async_copy(src_hbm.at[pl.ds(0, dst_vmem.shape[0]), :], dst_vmem, sem).wait()


def _combine_kernel(dest_ref, y_hbm, rw_ref, x_ref, mod_ref, lnf_ref, o_ref, y_scr, sems, *, final):
    tm = x_ref.shape[0]
    i = pl.program_id(0)
    n_tiles = pl.num_programs(0) - COMBINE_LOOKAHEAD
    n_slots = COMBINE_LOOKAHEAD + 1

    def start_gather():
        slot = i % n_slots
        for r in range(2 * tm):
            pltpu.make_async_copy(y_hbm.at[pl.ds(dest_ref[0, r], 1), :],
                                  y_scr.at[slot, pl.ds(r, 1), :], sems.at[slot]).start(priority=r % 2)

    def combine(prefetch):
        slot = (i - COMBINE_LOOKAHEAD) % n_slots
        _wait_row_gather(y_hbm, y_scr.at[slot], sems.at[slot])
        rw = rw_ref[...]
        a_lo, a_hi = _unpack_bf16_pairs(y_scr[slot, 0:tm, :])
        b_lo, b_hi = _unpack_bf16_pairs(y_scr[slot, tm:2 * tm, :])
        y = jnp.concatenate([rw[:, 0:1] * a_lo + rw[:, 1:2] * b_lo,
                             rw[:, 0:1] * a_hi + rw[:, 1:2] * b_hi], axis=1)
        if prefetch:
            start_gather()
        x_new = x_ref[...] + mod_ref[5:6, :] * y
        if final:
            x_new = x_new * lax.rsqrt(jnp.mean(x_new * x_new, axis=-1, keepdims=True) + EPS) * lnf_ref[...]
        o_ref[...] = x_new

    pl.when(i < COMBINE_LOOKAHEAD)(start_gather)
    pl.when((i >= COMBINE_LOOKAHEAD) & (i < n_tiles))(functools.partial(combine, True))
    pl.when(i >= n_tiles)(functools.partial(combine, False))


COMBINE_LOOKAHEAD = 2


def _combine(dest3, yg, rw, x2, mod, ln_f, seq, final):
    t, d = x2.shape
    tm = dest3.shape[2] // 2
    nt = t // tm
    prev = lambda m: jnp.maximum(m - COMBINE_LOOKAHEAD, 0)
    return pl.pallas_call(
        functools.partial(_combine_kernel, final=final),
        grid=(nt + COMBINE_LOOKAHEAD,),
        in_specs=[
            pl.BlockSpec((None, 1, 2 * tm), lambda m: (jnp.minimum(m, nt - 1), 0, 0), memory_space=pltpu.SMEM),
            pl.BlockSpec(memory_space=pl.ANY),
            pl.BlockSpec((tm, LANES), lambda m: (prev(m), 0)),
            pl.BlockSpec((tm, d), lambda m: (prev(m), 0)),
            pl.BlockSpec((None, 6, d), lambda m: ((prev(m) * tm) // seq, 0, 0)),
            pl.BlockSpec((1, d), lambda m: (0, 0)),
        ],
        out_specs=pl.BlockSpec((tm, d), lambda m: (prev(m), 0)),
        out_shape=jax.ShapeDtypeStruct((t, d), F32),
        scratch_shapes=[pltpu.VMEM((COMBINE_LOOKAHEAD + 1, 2 * tm, yg.shape[1]), jnp.uint32),
                        pltpu.SemaphoreType.DMA((COMBINE_LOOKAHEAD + 1,))],
        compiler_params=_cparams(("arbitrary",)),
        name="moe_combine",
    )(dest3, yg, rw, x2, mod, ln_f)


def _routing_tables(ri, counts, t):
    n_blocks = (t * TOP_K) // ROW_BLOCK + N_EXPERTS
    cnt = counts[0, :N_EXPERTS]
    padded = (cnt + ROW_BLOCK - 1) // ROW_BLOCK * ROW_BLOCK
    ends = jnp.cumsum(padded)
    starts = ends - padded
    experts, ranks = ri[:, 0:TOP_K, :], ri[:, TOP_K:2 * TOP_K, :]
    offsets = jnp.sum(jnp.where(experts[..., None] == jnp.arange(N_EXPERTS), starts, 0), axis=-1)
    dest = (offsets + ranks).reshape(ri.shape[0], 1, TOP_K * ri.shape[2])
    n_used = (ends[-1] // ROW_BLOCK).astype(jnp.int32)
    blocks = jnp.minimum(jnp.arange(n_blocks + 1, dtype=jnp.int32), n_used - 1) * ROW_BLOCK
    block_exp = jnp.sum((blocks[:, None] >= ends[None, :]).astype(jnp.int32), axis=1)
    block_exp = jnp.minimum(block_exp, N_EXPERTS - 1)
    cur_e = jnp.concatenate([block_exp[:1], block_exp[:-1]])
    nxt_e = block_exp
    last = (cur_e != nxt_e).astype(jnp.int32).at[0].set(1)
    return dest.astype(jnp.int32), cur_e, nxt_e, last, n_used.reshape(1)


def kernel(x, c, ln1, ln2, w_ada, b_ada, w_in, w_gk, b_gk, g_onorm, w_out,
           w_r1, b_r1, w_r2, b_r2, w_e_gate, w_e_up, w_e_down, ln_f):
    b, seq, d = x.shape
    depth = w_ada.shape[0]
    t = b * seq
    mod_all = _adaln_mod(c, w_ada, b_ada).reshape(depth, b, 6, d)
    tables = _rope_tables(seq)
    x2 = x.reshape(t, d)
    xg = jnp.zeros(((t * TOP_K // ROW_BLOCK + N_EXPERTS) * ROW_BLOCK, d // 2), jnp.uint32)
    w_in_b = w_in.astype(BF16)
    w_o = w_out.astype(BF16)
    for l in range(depth):
        mod = mod_all[l]
        w_ga = jnp.pad(w_in_b[l][:, PROJ_MAIN:], ((0, 0), (0, LANES - GATE_RANK)))
        w_gk_p = jnp.pad(w_gk[l], ((0, LANES - GATE_RANK), (0, 0)))
        proj, z = _in_projection(x2, ln1[l][None, :], mod, w_in_b, w_ga, w_gk_p, b_gk[l][None, :], seq, l)
        proj3 = proj.reshape(b, seq, PROJ_MAIN)
        oa = _moba_attention(proj3, tables)
        ob = _gla_mixer(proj3, z.reshape(b, seq, G_KEY_WIDTH), g_onorm[l][None, :])
        w_r = jnp.pad(jnp.concatenate([w_r2[l], w_r1[l]], axis=1), ((0, 0), (0, LANES - N_EXPERTS - N_GROUPS)))
        w_r_hi = w_r.astype(BF16)
        w_r = jnp.concatenate([w_r_hi, (w_r - w_r_hi.astype(F32)).astype(BF16)], axis=1)
        b_r = jnp.pad(jnp.concatenate([b_r2[l], b_r1[l]]), (0, LANES - N_EXPERTS - N_GROUPS))[None, :]
        x2, h2, ri, rw, counts = _outproj_router(
            oa.reshape(t, A_WIDTH), ob.reshape(t, G_VAL_WIDTH), x2, w_o, mod, ln2[l][None, :], w_r, b_r, seq, l)
        dest, cur_e, nxt_e, last, n_used = _routing_tables(ri, counts, t)
        xg = _dispatch(dest, h2, xg)
        yg = _expert_mlp(cur_e, nxt_e, last, n_used, xg, w_e_gate, w_e_up, w_e_down, l)
        x2 = _combine(dest, yg, rw, x2, mod, ln_f[None, :], seq, final=(l == depth - 1))
    return x2.reshape(b, seq, d)
```

```python
import functools

import jax
import jax.numpy as jnp
import numpy as np
from jax import lax
from jax.experimental import pallas as pl
from jax.experimental.pallas import tpu as pltpu

F32 = jnp.float32
BF16 = jnp.bfloat16
HIGHEST = lax.Precision.HIGHEST

HEAD_DIM = 128
A_HEADS = 8
A_WIDTH = A_HEADS * HEAD_DIM
MOBA_BLOCK = 256
MOBA_TOPK = 3
ROPE_THETA = 500000.0
ROPE_DIM = HEAD_DIM // 4
ATTN_SCALE = HEAD_DIM ** -0.5
G_HEADS = 4
G_VAL_DIM = 256
G_KEY_DIM = 128
G_KEY_WIDTH = G_HEADS * G_KEY_DIM
G_VAL_WIDTH = G_HEADS * G_VAL_DIM
GATE_RANK = 16
GATE_TAU = 16.0
GLA_CHUNK = 64
N_GROUPS = 4
EXPERTS_PER_GROUP = 8
N_EXPERTS = N_GROUPS * EXPERTS_PER_GROUP
TOP_K = 2
EPS = 1e-6

LANES = 128
PROJ_MAIN = 3 * A_WIDTH + 2 * G_KEY_WIDTH + 2 * G_VAL_WIDTH
ROW_BLOCK = 512
VMEM_LIMIT = 52 * 1024 * 1024
EXPERT_VMEM_LIMIT = 58 * 1024 * 1024


def _cparams(sem, vmem_limit=VMEM_LIMIT, flags=None):
    return pltpu.CompilerParams(dimension_semantics=sem, vmem_limit_bytes=vmem_limit, flags=flags)


def _pack_bf16_pairs(x):
    c = x.shape[1] // 2
    lo = pltpu.bitcast(x[:, :c].astype(BF16).astype(F32), jnp.uint32)
    hi = pltpu.bitcast(x[:, c:].astype(BF16).astype(F32), jnp.uint32)
    return (lo >> 16) | hi


def _unpack_bf16_pairs(w):
    lo = pltpu.bitcast(w << 16, F32)
    hi = pltpu.bitcast(w & jnp.uint32(0xFFFF0000), F32)
    return lo, hi


def _mod_kernel(c_ref, w_ref, b_ref, o_ref):
    c = c_ref[...]
    nb = c.shape[0]
    cc = jnp.concatenate(_split_bf16(c * jax.nn.sigmoid(c)), axis=0)
    w_hi, w_lo = _split_bf16(w_ref[...])
    r = jnp.dot(cc, w_hi, preferred_element_type=F32) + jnp.dot(cc, w_lo, preferred_element_type=F32)
    o_ref[...] = (r[:nb] + r[nb:]) + b_ref[...]


def _adaln_mod(c, w_ada, b_ada):
    depth, d, n6 = w_ada.shape
    b = c.shape[0]
    tn = 1024
    return pl.pallas_call(
        _mod_kernel,
        grid=(depth, n6 // tn),
        in_specs=[
            pl.BlockSpec((b, d), lambda l, n: (0, 0)),
            pl.BlockSpec((None, d, tn), lambda l, n: (l, 0, n)),
            pl.BlockSpec((None, 1, tn), lambda l, n: (l, 0, n)),
        ],
        out_specs=pl.BlockSpec((None, b, tn), lambda l, n: (l, 0, n)),
        out_shape=jax.ShapeDtypeStruct((depth, b, n6), F32),
        compiler_params=_cparams(("arbitrary", "arbitrary")),
        name="adaln_mod",
    )(c, w_ada, b_ada.reshape(depth, 1, n6))


def _modulated_norm(x, g, shift, scale):
    y = x * lax.rsqrt(jnp.mean(x * x, axis=-1, keepdims=True) + EPS) * g
    return y * (1.0 + scale) + shift


def _inproj_kernel(x_ref, ln_ref, mod_ref, w_ref, wga_ref, wgk_ref, bgk_ref, proj_ref, z_ref, h_scr):
    @pl.when(pl.program_id(1) == 0)
    def _():
        h = _modulated_norm(x_ref[...], ln_ref[...], mod_ref[0:1, :], mod_ref[1:2, :])
        hb = h.astype(BF16)
        h_scr[...] = hb
        ga = jnp.dot(hb, wga_ref[...], preferred_element_type=F32)
        z_ref[...] = jnp.dot(ga, wgk_ref[...], precision=HIGHEST, preferred_element_type=F32) + bgk_ref[...]

    proj_ref[...] = jnp.dot(h_scr[...], w_ref[...], preferred_element_type=F32).astype(proj_ref.dtype)


def _in_projection(x2, ln, mod, w_main, w_ga, w_gk, b_gk, seq, layer):
    t, d = x2.shape
    tm, tn = 1024, 1024
    return pl.pallas_call(
        _inproj_kernel,
        grid=(t // tm, PROJ_MAIN // tn),
        in_specs=[
            pl.BlockSpec((tm, d), lambda m, n: (m, 0)),
            pl.BlockSpec((1, d), lambda m, n: (0, 0)),
            pl.BlockSpec((None, 6, d), lambda m, n: ((m * tm) // seq, 0, 0)),
            pl.BlockSpec((None, d, tn), lambda m, n: (layer, 0, n)),
            pl.BlockSpec((d, LANES), lambda m, n: (0, 0)),
            pl.BlockSpec((LANES, G_KEY_WIDTH), lambda m, n: (0, 0)),
            pl.BlockSpec((1, G_KEY_WIDTH), lambda m, n: (0, 0)),
        ],
        out_specs=[
            pl.BlockSpec((tm, tn), lambda m, n: (m, n)),
            pl.BlockSpec((tm, G_KEY_WIDTH), lambda m, n: (m, 0)),
        ],
        out_shape=[
            jax.ShapeDtypeStruct((t, PROJ_MAIN), BF16),
            jax.ShapeDtypeStruct((t, G_KEY_WIDTH), F32),
        ],
        scratch_shapes=[pltpu.VMEM((tm, d), BF16)],
        compiler_params=_cparams(("arbitrary", "arbitrary")),
        name="in_projection",
    )(x2, ln, mod, w_main, w_ga, w_gk, b_gk)


def _rope(xb, cos, sin, swap):
    partner = jnp.dot(xb, swap, preferred_element_type=F32)
    return xb.astype(F32) * cos + partner * sin


LOG2E = 1.4426950408889634
DEN_ROWS = 16


def _moba_kernel(q_ref, k_ref, v_ref, cos_ref, sin_ref, swap_ref, o_ref, s_scr):
    streams = [_moba_head(hd, q_ref, k_ref, v_ref, cos_ref, sin_ref, swap_ref, o_ref, s_scr)
               for hd in range(MOBA_HEADS_PER_STEP)]
    while streams:
        streams = [g for g in streams if next(g, "done") != "done"]


MOBA_HEADS_PER_STEP = 2


def _moba_head(hd, q_ref, k_ref, v_ref, cos_ref, sin_ref, swap_ref, o_ref, s_scr):
    seq = q_ref.shape[0]
    nblk = seq // MOBA_BLOCK
    lanes = slice(hd * HEAD_DIM, (hd + 1) * HEAD_DIM)
    cos, sin, swap = cos_ref[...], sin_ref[...], swap_ref[...]
    q = _rope(q_ref[:, lanes], cos, sin, swap) * (ATTN_SCALE * LOG2E)
    k = _rope(k_ref[:, lanes], cos, sin, swap)
    yield
    kmean = jnp.mean(k.reshape(nblk, MOBA_BLOCK, HEAD_DIM), axis=1)
    gate = lax.dot_general(kmean, q, (((1,), (1,)), ((), ())), precision=HIGHEST,
                           preferred_element_type=F32)
    blk_of_q = lax.broadcasted_iota(jnp.int32, (nblk, seq), 1) // MOBA_BLOCK
    row = lax.broadcasted_iota(jnp.int32, (nblk, seq), 0)
    past = row < blk_of_q
    better = jnp.zeros((nblk, seq), F32)
    for m in range(nblk):
        gm = gate[m:m + 1, :]
        past_m = blk_of_q[m:m + 1, :] > m
        beats = (gm > gate) | ((gm == gate) & (row > m))
        better = better + jnp.where(beats & past_m, 1.0, 0.0)
    sel = past & (better < float(MOBA_TOPK))
    yield

    qb = q.astype(BF16)
    kb = k.astype(BF16)
    vt = jnp.concatenate([v_ref[:, lanes].astype(F32).T, jnp.ones((DEN_ROWS, seq), F32)], axis=0).astype(BF16)
    yield
    key_i = lax.broadcasted_iota(jnp.int32, (MOBA_BLOCK, MOBA_BLOCK), 0)
    qry_i = lax.broadcasted_iota(jnp.int32, (MOBA_BLOCK, MOBA_BLOCK), 1)
    causal = key_i <= qry_i
    def blk(j):
        return slice(j * MOBA_BLOCK, (j + 1) * MOBA_BLOCK)

    def score_pass(i):
        mx = None
        for n in range(i + 1):
            s = lax.dot_general(kb[blk(n), :], qb[blk(i), :], (((1,), (1,)), ((), ())),
                                preferred_element_type=F32)
            s = jnp.where(causal if n == i else sel[n:n + 1, blk(i)], s, -jnp.inf)
            s_scr[hd, i % 2, n] = s
            bm = jnp.max(s, axis=0, keepdims=True)
            mx = bm if mx is None else jnp.maximum(mx, bm)
            yield mx

    def value_pass(i, mx):
        acc = jnp.zeros((HEAD_DIM + DEN_ROWS, MOBA_BLOCK), F32)
        for n in range(i + 1):
            p = jnp.exp2(s_scr[hd, i % 2, n] - mx)
            acc = acc + jnp.dot(vt[:, blk(n)], p.astype(BF16), preferred_element_type=F32)
            yield None
        den = acc[HEAD_DIM:HEAD_DIM + 1, :]
        o_ref[blk(i), lanes] = (acc[:HEAD_DIM, :] / den).T.astype(o_ref.dtype)
        yield None

    mx = list(score_pass(0))[-1]
    for i in range(nblk):
        values = value_pass(i, mx)
        scores = score_pass(i + 1) if i + 1 < nblk else iter(())
        for mx_next in scores:
            mx = mx_next
            next(values, None)
            yield
        for _ in values:
            yield


def _rope_tables(seq):
    half = ROPE_DIM // 2
    inv = jnp.power(ROPE_THETA, -jnp.arange(half, dtype=F32) * (2.0 / ROPE_DIM))
    ang = jnp.arange(seq, dtype=F32)[:, None] * inv[None, :]
    cos, sin = jnp.cos(ang), jnp.sin(ang)
    ones = jnp.ones((seq, HEAD_DIM - ROPE_DIM), F32)
    zeros = jnp.zeros((seq, HEAD_DIM - half), F32)
    cos_t = jnp.concatenate([cos, cos, ones], axis=1)
    sin_t = jnp.concatenate([-sin, sin, zeros[:, half:]], axis=1)
    src = jnp.arange(HEAD_DIM)[:, None]
    dst = jnp.arange(HEAD_DIM)[None, :]
    swap = ((dst < ROPE_DIM) & (src == jnp.where(dst < half, dst + half, dst - half))).astype(BF16)
    return cos_t, sin_t, swap


def _moba_attention(proj3, tables):
    b, seq, _ = proj3.shape
    hps = MOBA_HEADS_PER_STEP
    steps = A_HEADS // hps
    head = lambda off: pl.BlockSpec((None, seq, hps * HEAD_DIM), lambda bi, h: (bi, 0, off + h))
    tab = pl.BlockSpec((seq, HEAD_DIM), lambda bi, h: (0, 0))
    return pl.pallas_call(
        _moba_kernel,
        grid=(b, steps),
        in_specs=[head(0), head(steps), head(2 * steps), tab, tab,
                  pl.BlockSpec((HEAD_DIM, HEAD_DIM), lambda bi, h: (0, 0))],
        out_specs=pl.BlockSpec((None, seq, hps * HEAD_DIM), lambda bi, h: (bi, 0, h)),
        out_shape=jax.ShapeDtypeStruct((b, seq, A_WIDTH), BF16),
        scratch_shapes=[pltpu.VMEM((hps, 2, seq // MOBA_BLOCK, MOBA_BLOCK, MOBA_BLOCK), F32)],
        compiler_params=_cparams(("arbitrary", "arbitrary")),
        name="moba_attention",
    )(proj3, proj3, proj3, *tables)


def _log_sigmoid(z):
    return jnp.minimum(z, 0.0) - jnp.log1p(jnp.exp(-jnp.abs(z)))


def _chunk_cumsum(x):
    pos = lax.broadcasted_iota(jnp.int32, x.shape, 0) % GLA_CHUNK
    shift = 1
    while shift < GLA_CHUNK:
        x = x + jnp.where(pos >= shift, pltpu.roll(x, shift, 0), 0.0)
        shift *= 2
    return x


def _gla_kernel(q_ref, k_ref, v_ref, og_ref, z_ref, gn_ref, o_ref,
                qd_scr, ki_scr, ke_scr, dec_scr, kv_scr, st_scr, oi_scr):
    seq = q_ref.shape[0]
    nc = seq // GLA_CHUNK
    heads = range(GLA_HEADS_PER_STEP)
    kl = lambda hd: slice(hd * G_KEY_DIM, (hd + 1) * G_KEY_DIM)
    vl = lambda hd: slice(hd * G_VAL_DIM, (hd + 1) * G_VAL_DIM)
    for hd in heads:
        log_a = _log_sigmoid(z_ref[:, kl(hd)]) / GATE_TAU
        b = _chunk_cumsum(log_a)
        b3 = b.reshape(nc, GLA_CHUNK, G_KEY_DIM)
        b_last = b3[:, GLA_CHUNK - 1:GLA_CHUNK, :]
        k = k_ref[:, kl(hd)].astype(F32)
        qd_scr[hd] = ((q_ref[:, kl(hd)].astype(F32) * (G_KEY_DIM ** -0.5)) * jnp.exp(b)).astype(BF16)
        ki_scr[hd] = (k * jnp.exp(-b)).astype(BF16)
        ke3 = k.reshape(nc, GLA_CHUNK, G_KEY_DIM) * jnp.exp(b_last - b3)
        ke_scr[hd] = ke3.reshape(seq, G_KEY_DIM).astype(BF16)
        dec_scr[hd] = jnp.exp(b_last)
    group = GLA_GROUP * GLA_CHUNK
    t_i = lax.broadcasted_iota(jnp.int32, (group, group), 0)
    s_i = lax.broadcasted_iota(jnp.int32, (group, group), 1)
    keep = (s_i <= t_i) & ((s_i // GLA_CHUNK) == (t_i // GLA_CHUNK))
    gn = gn_ref[...]

    def chunk_rows(n):
        return pl.ds(pl.multiple_of(n * GLA_CHUNK, GLA_CHUNK), GLA_CHUNK)

    def intra(g, carry):
        rows = pl.ds(pl.multiple_of(g * group, group), group)
        for hd in heads:
            vb = v_ref[rows, vl(hd)].astype(BF16)
            ke = ke_scr[hd, rows, :]
            att = lax.dot_general(qd_scr[hd, rows, :], ki_scr[hd, rows, :], (((1,), (1,)), ((), ())),
                                  preferred_element_type=F32)
            att = jnp.where(keep, att, 0.0).astype(BF16)
            oi_scr[hd, rows, :] = jnp.dot(att, vb, preferred_element_type=F32)
            for c in range(GLA_GROUP):
                cs = slice(c * GLA_CHUNK, (c + 1) * GLA_CHUNK)
                kv_scr[hd, g * GLA_GROUP + c] = lax.dot_general(
                    vb[cs, :], ke[cs, :], (((0,), (0,)), ((), ())), preferred_element_type=F32)
        return carry

    lax.fori_loop(0, nc // GLA_GROUP, intra, 0, unroll=GLA_UNROLL // GLA_GROUP)

    def scan(n, states):
        for hd in heads:
            st_scr[hd, n] = states[hd].astype(BF16)
        return tuple(states[hd] * dec_scr[hd, n] + kv_scr[hd, n] for hd in heads)

    zero_state = jnp.zeros((G_VAL_DIM, G_KEY_DIM), F32)
    lax.fori_loop(0, nc, scan, tuple(zero_state for _ in heads), unroll=GLA_UNROLL)

    def inter(n, carry):
        rows = chunk_rows(n)
        for hd in heads:
            o = oi_scr[hd, rows, :] + lax.dot_general(qd_scr[hd, rows, :], st_scr[hd, n],
                                                      (((1,), (1,)), ((), ())),
                                                      preferred_element_type=F32)
            o = o * lax.rsqrt(jnp.mean(o * o, axis=-1, keepdims=True) + EPS) * gn
            og = og_ref[rows, vl(hd)].astype(F32)
            o_ref[rows, vl(hd)] = (o * (og * jax.nn.sigmoid(og))).astype(o_ref.dtype)
        return carry

    lax.fori_loop(0, nc, inter, 0, unroll=GLA_UNROLL)


GLA_HEADS_PER_STEP = 2
GLA_UNROLL = 16
GLA_GROUP = 4


def _gla_mixer(proj3, z3, g_onorm):
    b, seq, _ = proj3.shape
    kq = 3 * A_WIDTH // G_KEY_DIM
    kv = (3 * A_WIDTH + 2 * G_KEY_WIDTH) // G_VAL_DIM
    hps = GLA_HEADS_PER_STEP
    steps = G_HEADS // hps
    key = lambda off: pl.BlockSpec((None, seq, hps * G_KEY_DIM), lambda bi, h: (bi, 0, off // hps + h))
    val = lambda off: pl.BlockSpec((None, seq, hps * G_VAL_DIM), lambda bi, h: (bi, 0, off // hps + h))
    return pl.pallas_call(
        _gla_kernel,
        grid=(b, steps),
        in_specs=[
            key(kq), key(kq + G_HEADS), val(kv), val(kv + G_HEADS),
            pl.BlockSpec((None, seq, hps * G_KEY_DIM), lambda bi, h: (bi, 0, h)),
            pl.BlockSpec((1, G_VAL_DIM), lambda bi, h: (0, 0)),
        ],
        out_specs=pl.BlockSpec((None, seq, hps * G_VAL_DIM), lambda bi, h: (bi, 0, h)),
        out_shape=jax.ShapeDtypeStruct((b, seq, G_VAL_WIDTH), BF16),
        scratch_shapes=[
            pltpu.VMEM((hps, seq, G_KEY_DIM), BF16),
            pltpu.VMEM((hps, seq, G_KEY_DIM), BF16),
            pltpu.VMEM((hps, seq, G_KEY_DIM), BF16),
            pltpu.VMEM((hps, seq // GLA_CHUNK, 1, G_KEY_DIM), F32),
            pltpu.VMEM((hps, seq // GLA_CHUNK, G_VAL_DIM, G_KEY_DIM), F32),
            pltpu.VMEM((hps, seq // GLA_CHUNK, G_VAL_DIM, G_KEY_DIM), BF16),
            pltpu.VMEM((hps, seq, G_VAL_DIM), F32),
        ],
        compiler_params=_cparams(("arbitrary", "arbitrary")),
        name="gla_mixer",
    )(proj3, proj3, proj3, proj3, z3, g_onorm)


def _first_lane(mask, lane):
    return jnp.min(jnp.where(mask, lane, LANES), axis=-1, keepdims=True)


def _split_bf16(x):
    hi = x.astype(BF16)
    return hi, (x - hi.astype(F32)).astype(BF16)


def _outproj_router_kernel(oa_ref, ob_ref, x_ref, wa_ref, wb_ref, mod_ref, ln_ref, wr_ref, br_ref,
                           xo_ref, h_ref, ri_ref, rw_ref, cnt_ref, run_scr):
    tm = x_ref.shape[0]
    sub = tm // ROUTER_SUBTILES

    @pl.when(pl.program_id(0) == 0)
    def _():
        run_scr[...] = jnp.zeros_like(run_scr)

    lane = lax.broadcasted_iota(jnp.int32, (sub, LANES), 1)
    r_i = lax.broadcasted_iota(jnp.int32, (sub, sub), 0)
    c_i = lax.broadcasted_iota(jnp.int32, (sub, sub), 1)
    strict_lower = jnp.where(c_i < r_i, 1.0, 0.0).astype(BF16)
    state = {"run": run_scr[...]}

    def sub_tile(s):
        rows = slice(s * sub, (s + 1) * sub)
        mix = jnp.dot(oa_ref[rows, :], wa_ref[...], preferred_element_type=F32)
        mix = mix + jnp.dot(ob_ref[rows, :], wb_ref[...], preferred_element_type=F32)
        x_new = x_ref[rows, :] + mod_ref[2:3, :] * mix
        xo_ref[rows, :] = x_new
        h = _modulated_norm(x_new, ln_ref[...], mod_ref[3:4, :], mod_ref[4:5, :])
        h_ref[rows, :] = _pack_bf16_pairs(h)
        yield

        hh = jnp.concatenate(_split_bf16(h), axis=0)
        r = jnp.dot(hh, wr_ref[...], preferred_element_type=F32)
        logits = (r[:sub, :LANES] + r[sub:, :LANES]) + (r[:sub, LANES:] + r[sub:, LANES:]) + br_ref[...]
        is_grp = (lane >= N_EXPERTS) & (lane < N_EXPERTS + N_GROUPS)
        l1 = jnp.where(is_grp, logits, -jnp.inf)
        m1 = jnp.max(l1, axis=-1, keepdims=True)
        grp = _first_lane(l1 == m1, lane) - N_EXPERTS
        p_grp = 1.0 / jnp.sum(jnp.exp(l1 - m1), axis=-1, keepdims=True)
        in_grp = (lane < N_EXPERTS) & ((lane // EXPERTS_PER_GROUP) == grp)
        l2 = jnp.where(in_grp, logits, -jnp.inf)
        va = jnp.max(l2, axis=-1, keepdims=True)
        ia = _first_lane(l2 == va, lane)
        l2b = jnp.where(lane == ia, -jnp.inf, l2)
        vb = jnp.max(l2b, axis=-1, keepdims=True)
        ib = _first_lane(l2b == vb, lane)
        eb = jnp.exp(vb - va)
        wa = p_grp * (1.0 / (1.0 + eb))
        wb = p_grp * (eb / (1.0 + eb))
        rw_ref[rows, :] = jnp.where(lane == 0, wa, 0.0) + jnp.where(lane == 1, wb, 0.0)
        oh_a = jnp.where(lane == ia, 1.0, 0.0)
        oh_b = jnp.where(lane == ib, 1.0, 0.0)
        cnt = oh_a + oh_b
        within = jnp.dot(strict_lower, cnt.astype(BF16), preferred_element_type=F32)
        yield

        before = within + state["run"]
        state["run"] = state["run"] + jnp.sum(cnt, axis=0, keepdims=True)
        rank_a = jnp.sum(oh_a * before, axis=-1, keepdims=True)
        rank_b = jnp.sum(oh_b * before, axis=-1, keepdims=True)
        ri = jnp.where(lane == 0, ia.astype(F32), 0.0) + jnp.where(lane == 1, ib.astype(F32), 0.0)
        ri = ri + jnp.where(lane == 2, rank_a, 0.0) + jnp.where(lane == 3, rank_b, 0.0)
        ri_ref[:, rows] = ri.T[0:ROUTE_ROWS, :].astype(jnp.int32)
        yield

    stages = [sub_tile(s) for s in range(ROUTER_SUBTILES)]
    for _ in range(3):
        for g in stages:
            next(g)

    run_scr[...] = state["run"]
    cnt_ref[...] = jnp.broadcast_to(state["run"], cnt_ref.shape).astype(jnp.int32)


ROUTER_SUBTILES = 4


ROUTE_ROWS = 8
TOKEN_TILE = 512


def _outproj_router(oa2, ob2, x2, w_o, mod, ln, w_r, b_r, seq, layer):
    t, d = x2.shape
    tm = TOKEN_TILE
    row = lambda w: pl.BlockSpec((tm, w), lambda m: (m, 0))
    full = lambda a, c: pl.BlockSpec((a, c), lambda m: (0, 0))
    return pl.pallas_call(
        _outproj_router_kernel,
        grid=(t // tm,),
        in_specs=[
            row(A_WIDTH), row(G_VAL_WIDTH), row(d),
            pl.BlockSpec((None, A_WIDTH, d), lambda m: (layer, 0, 0)),
            pl.BlockSpec((None, G_VAL_WIDTH, d), lambda m: (layer, A_WIDTH // G_VAL_WIDTH, 0)),
            pl.BlockSpec((None, 6, d), lambda m: ((m * tm) // seq, 0, 0)),
            full(1, d), full(d, 2 * LANES), full(1, LANES),
        ],
        out_specs=[row(d), row(d // 2), pl.BlockSpec((None, ROUTE_ROWS, tm), lambda m: (m, 0, 0)),
                   row(LANES), full(8, LANES)],
        out_shape=[
            jax.ShapeDtypeStruct((t, d), F32),
            jax.ShapeDtypeStruct((t, d // 2), jnp.uint32),
            jax.ShapeDtypeStruct((t // tm, ROUTE_ROWS, tm), jnp.int32),
            jax.ShapeDtypeStruct((t, LANES), F32),
            jax.ShapeDtypeStruct((8, LANES), jnp.int32),
        ],
        scratch_shapes=[pltpu.VMEM((1, LANES), F32)],
        compiler_params=_cparams(("arbitrary",)),
        name="outproj_router",
    )(oa2, ob2, x2, w_o, w_o, mod, ln, w_r, b_r)


def _dispatch_kernel(dest_ref, h_ref, xg_in, xg_out, sem):
    del xg_in
    tm = h_ref.shape[0]

    for r in range(tm):
        for k in range(TOP_K):
            pltpu.make_async_copy(h_ref.at[pl.ds(r, 1), :],
                                  xg_out.at[pl.ds(dest_ref[0, k * tm + r], 1), :], sem).start(priority=r % 2)
    done = xg_out.at[pl.ds(0, TOP_K * tm), :]
    pltpu.make_async_copy(done, done, sem).wait()


def _dispatch(dest3, h2p, xg):
    t, dp = h2p.shape
    tm = dest3.shape[2] // TOP_K
    return pl.pallas_call(
        _dispatch_kernel,
        grid=(t // tm,),
        in_specs=[
            pl.BlockSpec((None, 1, TOP_K * tm), lambda m: (m, 0, 0), memory_space=pltpu.SMEM),
            pl.BlockSpec((tm, dp), lambda m: (m, 0)),
            pl.BlockSpec(memory_space=pl.ANY),
        ],
        out_specs=pl.BlockSpec(memory_space=pl.ANY),
        out_shape=jax.ShapeDtypeStruct(xg.shape, xg.dtype),
        input_output_aliases={2: 0},
        scratch_shapes=[pltpu.SemaphoreType.DMA(())],
        compiler_params=_cparams(("arbitrary",)),
        name="moe_dispatch",
    )(dest3, h2p, xg)


def _expert_kernel(cur_ref, nxt_ref, last_ref, nused_ref, x_ref, wg_ref, wu_ref, wd_ref,
                   y_ref, wg_res, wu_res, wd_res, xb_scr, acc_scr):
    del cur_ref, nxt_ref
    b, j = pl.program_id(0), pl.program_id(1)
    active = (b >= 1) & (b <= nused_ref[0])

    @pl.when(active & (j == 0))
    def _():
        lo, hi = _unpack_bf16_pairs(x_ref[...])
        half = lo.shape[1]
        xb_scr[:, :half] = lo.astype(BF16)
        xb_scr[:, half:] = hi.astype(BF16)

    for jj in range(EXPERT_HIDDEN_CHUNKS):
        @pl.when(active & (j == jj))
        def _(jj=jj):
            xb = xb_scr[...]
            g = jnp.dot(xb, wg_res[jj], preferred_element_type=F32)
            u = jnp.dot(xb, wu_res[jj], preferred_element_type=F32)
            a = ((g * jax.nn.sigmoid(g)) * u).astype(BF16)
            part = jnp.dot(a, wd_res[jj], preferred_element_type=F32)
            if jj == 0:
                acc_scr[...] = part
            elif jj < EXPERT_HIDDEN_CHUNKS - 1:
                acc_scr[...] += part
            else:
                y_ref[...] = _pack_bf16_pairs(acc_scr[...] + part)

    @pl.when((b > nused_ref[0]) & (j == 0))
    def _():
        y_ref[...] = jnp.zeros_like(y_ref)

    @pl.when(last_ref[b] == 1)
    def _():
        wg_res[j] = wg_ref[...].astype(BF16)
        wu_res[j] = wu_ref[...].astype(BF16)
        wd_res[j] = wd_ref[...].astype(BF16)


EXPERT_HIDDEN_CHUNKS = 2


def _expert_mlp(cur_e, nxt_e, last, n_used, xg, wg, wu, wd, layer):
    dp = xg.shape[1]
    nb = xg.shape[0] // ROW_BLOCK
    d, f = wg.shape[2], wg.shape[3]
    nj = EXPERT_HIDDEN_CHUNKS
    fc = f // nj

    def x_index(b, j, cur, nxt, lst, nu):
        return (jnp.clip(b - 1, 0, nu[0] - 1), 0)

    def y_index(b, j, cur, nxt, lst, nu):
        return (jnp.maximum(b - 1, 0), 0)

    def w_index(chunk_axis):
        def index(b, j, cur, nxt, lst, nu):
            e = jnp.where(lst[b] == 1, nxt[b], cur[b])
            jj = jnp.where(lst[b] == 1, j, nj - 1)
            return (layer, e, 0, jj) if chunk_axis == 2 else (layer, e, jj, 0)
        return index

    grid_spec = pltpu.PrefetchScalarGridSpec(
        num_scalar_prefetch=4,
        grid=(nb + 1, nj),
        in_specs=[
            pl.BlockSpec((ROW_BLOCK, dp), x_index),
            pl.BlockSpec((None, None, d, fc), w_index(2)),
            pl.BlockSpec((None, None, d, fc), w_index(2)),
            pl.BlockSpec((None, None, fc, d), w_index(1)),
        ],
        out_specs=pl.BlockSpec((ROW_BLOCK, dp), y_index),
        scratch_shapes=[
            pltpu.VMEM((nj, d, fc), BF16),
            pltpu.VMEM((nj, d, fc), BF16),
            pltpu.VMEM((nj, fc, d), BF16),
            pltpu.VMEM((ROW_BLOCK, d), BF16),
            pltpu.VMEM((ROW_BLOCK, d), F32),
        ],
    )
    return pl.pallas_call(
        _expert_kernel,
        grid_spec=grid_spec,
        out_shape=jax.ShapeDtypeStruct((nb * ROW_BLOCK, dp), jnp.uint32),
        compiler_params=_cparams(("arbitrary", "arbitrary"), EXPERT_VMEM_LIMIT),
        name="expert_mlp",
    )(cur_e, nxt_e, last, n_used, xg, wg, wu, wd)


def _wait_row_gather(src_hbm, dst_vmem, sem):
    pltpu.make_async_copy(src_hbm.at[pl.ds(0, dst_vmem.shape[0]), :], dst_vmem, sem).wait()


def _combine_kernel(dest_ref, y_hbm, rw_ref, x_ref, mod_ref, lnf_ref, o_ref, y_scr, sems, *, final):
    tm = x_ref.shape[0]
    i = pl.program_id(0)
    n_tiles = pl.num_programs(0) - COMBINE_LOOKAHEAD
    n_slots = COMBINE_LOOKAHEAD + 1

    def start_gather():
        slot = i % n_slots
        for r in range(2 * tm):
            pltpu.make_async_copy(y_hbm.at[pl.ds(dest_ref[0, r], 1), :],
                                  y_scr.at[slot, pl.ds(r, 1), :], sems.at[slot]).start(priority=r % 2)

    def combine(prefetch):
        slot = (i - COMBINE_LOOKAHEAD) % n_slots
        _wait_row_gather(y_hbm, y_scr.at[slot], sems.at[slot])
        rw = rw_ref[...]
        a_lo, a_hi = _unpack_bf16_pairs(y_scr[slot, 0:tm, :])
        b_lo, b_hi = _unpack_bf16_pairs(y_scr[slot, tm:2 * tm, :])
        y = jnp.concatenate([rw[:, 0:1] * a_lo + rw[:, 1:2] * b_lo,
                             rw[:, 0:1] * a_hi + rw[:, 1:2] * b_hi], axis=1)
        if prefetch:
            start_gather()
        x_new = x_ref[...] + mod_ref[5:6, :] * y
        if final:
            x_new = x_new * lax.rsqrt(jnp.mean(x_new * x_new, axis=-1, keepdims=True) + EPS) * lnf_ref[...]
        o_ref[...] = x_new

    pl.when(i < COMBINE_LOOKAHEAD)(start_gather)
    pl.when((i >= COMBINE_LOOKAHEAD) & (i < n_tiles))(functools.partial(combine, True))
    pl.when(i >= n_tiles)(functools.partial(combine, False))


COMBINE_LOOKAHEAD = 2


def _combine(dest3, yg, rw, x2, mod, ln_f, seq, final):
    t, d = x2.shape
    tm = dest3.shape[2] // 2
    nt = t // tm
    prev = lambda m: jnp.maximum(m - COMBINE_LOOKAHEAD, 0)
    return pl.pallas_call(
        functools.partial(_combine_kernel, final=final),
        grid=(nt + COMBINE_LOOKAHEAD,),
        in_specs=[
            pl.BlockSpec((None, 1, 2 * tm), lambda m: (jnp.minimum(m, nt - 1), 0, 0), memory_space=pltpu.SMEM),
            pl.BlockSpec(memory_space=pl.ANY),
            pl.BlockSpec((tm, LANES), lambda m: (prev(m), 0)),
            pl.BlockSpec((tm, d), lambda m: (prev(m), 0)),
            pl.BlockSpec((None, 6, d), lambda m: ((prev(m) * tm) // seq, 0, 0)),
            pl.BlockSpec((1, d), lambda m: (0, 0)),
        ],
        out_specs=pl.BlockSpec((tm, d), lambda m: (prev(m), 0)),
        out_shape=jax.ShapeDtypeStruct((t, d), F32),
        scratch_shapes=[pltpu.VMEM((COMBINE_LOOKAHEAD + 1, 2 * tm, yg.shape[1]), jnp.uint32),
                        pltpu.SemaphoreType.DMA((COMBINE_LOOKAHEAD + 1,))],
        compiler_params=_cparams(("arbitrary",)),
        name="moe_combine",
    )(dest3, yg, rw, x2, mod, ln_f)


def _routing_tables(ri, counts, t):
    n_blocks = (t * TOP_K) // ROW_BLOCK + N_EXPERTS
    cnt = counts[0, :N_EXPERTS]
    padded = (cnt + ROW_BLOCK - 1) // ROW_BLOCK * ROW_BLOCK
    ends = jnp.cumsum(padded)
    starts = ends - padded
    experts, ranks = ri[:, 0:TOP_K, :], ri[:, TOP_K:2 * TOP_K, :]
    offsets = jnp.sum(jnp.where(experts[..., None] == jnp.arange(N_EXPERTS), starts, 0), axis=-1)
    dest = (offsets + ranks).reshape(ri.shape[0], 1, TOP_K * ri.shape[2])
    n_used = (ends[-1] // ROW_BLOCK).astype(jnp.int32)
    blocks = jnp.minimum(jnp.arange(n_blocks + 1, dtype=jnp.int32), n_used - 1) * ROW_BLOCK
    block_exp = jnp.sum((blocks[:, None] >= ends[None, :]).astype(jnp.int32), axis=1)
    block_exp = jnp.minimum(block_exp, N_EXPERTS - 1)
    cur_e = jnp.concatenate([block_exp[:1], block_exp[:-1]])
    nxt_e = block_exp
    last = (cur_e != nxt_e).astype(jnp.int32).at[0].set(1)
    return dest.astype(jnp.int32), cur_e, nxt_e, last, n_used.reshape(1)


def kernel(x, c, ln1, ln2, w_ada, b_ada, w_in, w_gk, b_gk, g_onorm, w_out,
           w_r1, b_r1, w_r2, b_r2, w_e_gate, w_e_up, w_e_down, ln_f):
    b, seq, d = x.shape
    depth = w_ada.shape[0]
    t = b * seq
    mod_all = _adaln_mod(c, w_ada, b_ada).reshape(depth, b, 6, d)
    tables = _rope_tables(seq)
    x2 = x.reshape(t, d)
    xg = jnp.zeros(((t * TOP_K // ROW_BLOCK + N_EXPERTS) * ROW_BLOCK, d // 2), jnp.uint32)
    w_in_b = w_in.astype(BF16)
    w_o = w_out.astype(BF16)
    for l in range(depth):
        mod = mod_all[l]
        w_ga = jnp.pad(w_in_b[l][:, PROJ_MAIN:], ((0, 0), (0, LANES - GATE_RANK)))
        w_gk_p = jnp.pad(w_gk[l], ((0, LANES - GATE_RANK), (0, 0)))
        proj, z = _in_projection(x2, ln1[l][None, :], mod, w_in_b, w_ga, w_gk_p, b_gk[l][None, :], seq, l)
        proj3 = proj.reshape(b, seq, PROJ_MAIN)
        oa = _moba_attention(proj3, tables)
        ob = _gla_mixer(proj3, z.reshape(b, seq, G_KEY_WIDTH), g_onorm[l][None, :])
        w_r = jnp.pad(jnp.concatenate([w_r2[l], w_r1[l]], axis=1), ((0, 0), (0, LANES - N_EXPERTS - N_GROUPS)))
        w_r_hi = w_r.astype(BF16)
        w_r = jnp.concatenate([w_r_hi, (w_r - w_r_hi.astype(F32)).astype(BF16)], axis=1)
        b_r = jnp.pad(jnp.concatenate([b_r2[l], b_r1[l]]), (0, LANES - N_EXPERTS - N_GROUPS))[None, :]
        x2, h2, ri, rw, counts = _outproj_router(
            oa.reshape(t, A_WIDTH), ob.reshape(t, G_VAL_WIDTH), x2, w_o, mod, ln2[l][None, :], w_r, b_r, seq, l)
        dest, cur_e, nxt_e, last, n_used = _routing_tables(ri, counts, t)
        xg = _dispatch(dest, h2, xg)
        yg = _expert_mlp(cur_e, nxt_e, last, n_used, xg, w_e_gate, w_e_up, w_e_down, l)
        x2 = _combine(dest, yg, rw, x2, mod, ln_f[None, :], seq, final=(l == depth - 1))
    return x2.reshape(b, seq, d)
```

```python
import functools

import jax
import jax.numpy as jnp
import numpy as np
from jax import lax
from jax.experimental import pallas as pl
from jax.experimental.pallas import tpu as pltpu

F32 = jnp.float32
BF16 = jnp.bfloat16
HIGHEST = lax.Precision.HIGHEST

HEAD_DIM = 128
A_HEADS = 8
A_WIDTH = A_HEADS * HEAD_DIM
MOBA_BLOCK = 256
MOBA_TOPK = 3
ROPE_THETA = 500000.0
ROPE_DIM = HEAD_DIM // 4
ATTN_SCALE = HEAD_DIM ** -0.5
G_HEADS = 4
G_VAL_DIM = 256
G_KEY_DIM = 128
G_KEY_WIDTH = G_HEADS * G_KEY_DIM
G_VAL_WIDTH = G_HEADS * G_VAL_DIM
GATE_RANK = 16
GATE_TAU = 16.0
GLA_CHUNK = 64
N_GROUPS = 4
EXPERTS_PER_GROUP = 8
N_EXPERTS = N_GROUPS * EXPERTS_PER_GROUP
TOP_K = 2
EPS = 1e-6

LANES = 128
PROJ_MAIN = 3 * A_WIDTH + 2 * G_KEY_WIDTH + 2 * G_VAL_WIDTH
ROW_BLOCK = 512
VMEM_LIMIT = 52 * 1024 * 1024
EXPERT_VMEM_LIMIT = 58 * 1024 * 1024


def _cparams(sem, vmem_limit=VMEM_LIMIT, flags=None):
    return pltpu.CompilerParams(dimension_semantics=sem, vmem_limit_bytes=vmem_limit, flags=flags)


def _pack_bf16_pairs(x):
    c = x.shape[1] // 2
    lo = pltpu.bitcast(x[:, :c].astype(BF16).astype(F32), jnp.uint32)
    hi = pltpu.bitcast(x[:, c:].astype(BF16).astype(F32), jnp.uint32)
    return (lo >> 16) | hi


def _unpack_bf16_pairs(w):
    lo = pltpu.bitcast(w << 16, F32)
    hi = pltpu.bitcast(w & jnp.uint32(0xFFFF0000), F32)
    return lo, hi


def _mod_kernel(c_ref, w_ref, b_ref, o_ref):
    c = c_ref[...]
    nb = c.shape[0]
    cc = jnp.concatenate(_split_bf16(c * jax.nn.sigmoid(c)), axis=0)
    w_hi, w_lo = _split_bf16(w_ref[...])
    r = jnp.dot(cc, w_hi, preferred_element_type=F32) + jnp.dot(cc, w_lo, preferred_element_type=F32)
    o_ref[...] = (r[:nb] + r[nb:]) + b_ref[...]


def _adaln_mod(c, w_ada, b_ada):
    depth, d, n6 = w_ada.shape
    b = c.shape[0]
    tn = 1024
    return pl.pallas_call(
        _mod_kernel,
        grid=(depth, n6 // tn),
        in_specs=[
            pl.BlockSpec((b, d), lambda l, n: (0, 0)),
            pl.BlockSpec((None, d, tn), lambda l, n: (l, 0, n)),
            pl.BlockSpec((None, 1, tn), lambda l, n: (l, 0, n)),
        ],
        out_specs=pl.BlockSpec((None, b, tn), lambda l, n: (l, 0, n)),
        out_shape=jax.ShapeDtypeStruct((depth, b, n6), F32),
        compiler_params=_cparams(("arbitrary", "arbitrary")),
        name="adaln_mod",
    )(c, w_ada, b_ada.reshape(depth, 1, n6))


def _modulated_norm(x, g, shift, scale):
    y = x * lax.rsqrt(jnp.mean(x * x, axis=-1, keepdims=True) + EPS) * g
    return y * (1.0 + scale) + shift


def _inproj_kernel(x_ref, ln_ref, mod_ref, w_ref, wga_ref, wgk_ref, bgk_ref, proj_ref, z_ref, h_scr):
    @pl.when(pl.program_id(1) == 0)
    def _():
        h = _modulated_norm(x_ref[...], ln_ref[...], mod_ref[0:1, :], mod_ref[1:2, :])
        hb = h.astype(BF16)
        h_scr[...] = hb
        ga = jnp.dot(hb, wga_ref[...], preferred_element_type=F32)
        z_ref[...] = jnp.dot(ga, wgk_ref[...], precision=HIGHEST, preferred_element_type=F32) + bgk_ref[...]

    proj_ref[...] = jnp.dot(h_scr[...], w_ref[...], preferred_element_type=F32).astype(proj_ref.dtype)


def _in_projection(x2, ln, mod, w_main, w_ga, w_gk, b_gk, seq, layer):
    t, d = x2.shape
    tm, tn = 1024, 1024
    return pl.pallas_call(
        _inproj_kernel,
        grid=(t // tm, PROJ_MAIN // tn),
        in_specs=[
            pl.BlockSpec((tm, d), lambda m, n: (m, 0)),
            pl.BlockSpec((1, d), lambda m, n: (0, 0)),
            pl.BlockSpec((None, 6, d), lambda m, n: ((m * tm) // seq, 0, 0)),
            pl.BlockSpec((None, d, tn), lambda m, n: (layer, 0, n)),
            pl.BlockSpec((d, LANES), lambda m, n: (0, 0)),
            pl.BlockSpec((LANES, G_KEY_WIDTH), lambda m, n: (0, 0)),
            pl.BlockSpec((1, G_KEY_WIDTH), lambda m, n: (0, 0)),
        ],
        out_specs=[
            pl.BlockSpec((tm, tn), lambda m, n: (m, n)),
            pl.BlockSpec((tm, G_KEY_WIDTH), lambda m, n: (m, 0)),
        ],
        out_shape=[
            jax.ShapeDtypeStruct((t, PROJ_MAIN), BF16),
            jax.ShapeDtypeStruct((t, G_KEY_WIDTH), F32),
        ],
        scratch_shapes=[pltpu.VMEM((tm, d), BF16)],
        compiler_params=_cparams(("arbitrary", "arbitrary")),
        name="in_projection",
    )(x2, ln, mod, w_main, w_ga, w_gk, b_gk)


def _rope(xb, cos, sin, swap):
    partner = jnp.dot(xb, swap, preferred_element_type=F32)
    return xb.astype(F32) * cos + partner * sin


LOG2E = 1.4426950408889634
DEN_ROWS = 16


def _moba_kernel(q_ref, k_ref, v_ref, cos_ref, sin_ref, swap_ref, o_ref, s_scr):
    streams = [_moba_head(hd, q_ref, k_ref, v_ref, cos_ref, sin_ref, swap_ref, o_ref, s_scr)
               for hd in range(MOBA_HEADS_PER_STEP)]
    while streams:
        streams = [g for g in streams if next(g, "done") != "done"]


MOBA_HEADS_PER_STEP = 2


def _moba_head(hd, q_ref, k_ref, v_ref, cos_ref, sin_ref, swap_ref, o_ref, s_scr):
    seq = q_ref.shape[0]
    nblk = seq // MOBA_BLOCK
    lanes = slice(hd * HEAD_DIM, (hd + 1) * HEAD_DIM)
    cos, sin, swap = cos_ref[...], sin_ref[...], swap_ref[...]
    q = _rope(q_ref[:, lanes], cos, sin, swap) * (ATTN_SCALE * LOG2E)
    k = _rope(k_ref[:, lanes], cos, sin, swap)
    yield
    kmean = jnp.mean(k.reshape(nblk, MOBA_BLOCK, HEAD_DIM), axis=1)
    kk = jnp.concatenate(_split_bf16(kmean), axis=0)
    q_hi, q_lo = _split_bf16(q)
    nt_dims = (((1,), (1,)), ((), ()))
    r = (lax.dot_general(kk, q_hi, nt_dims, preferred_element_type=F32)
         + lax.dot_general(kk, q_lo, nt_dims, preferred_element_type=F32))
    gate = r[:nblk] + r[nblk:]
    blk_of_q = lax.broadcasted_iota(jnp.int32, (nblk, seq), 1) // MOBA_BLOCK
    row = lax.broadcasted_iota(jnp.int32, (nblk, seq), 0)
    past = row < blk_of_q
    better = jnp.zeros((nblk, seq), F32)
    for m in range(nblk):
        gm = gate[m:m + 1, :]
        past_m = blk_of_q[m:m + 1, :] > m
        beats = (gm > gate) | ((gm == gate) & (row > m))
        better = better + jnp.where(beats & past_m, 1.0, 0.0)
    sel = past & (better < float(MOBA_TOPK))
    yield

    qb = q_hi
    kb = k.astype(BF16)
    vt = jnp.concatenate([v_ref[:, lanes].astype(F32).T, jnp.ones((DEN_ROWS, seq), F32)], axis=0).astype(BF16)
    yield
    key_i = lax.broadcasted_iota(jnp.int32, (MOBA_BLOCK, MOBA_BLOCK), 0)
    qry_i = lax.broadcasted_iota(jnp.int32, (MOBA_BLOCK, MOBA_BLOCK), 1)
    causal = key_i <= qry_i
    def blk(j):
        return slice(j * MOBA_BLOCK, (j + 1) * MOBA_BLOCK)

    def score_pass(i):
        mx = None
        for n in range(i + 1):
            s = lax.dot_general(kb[blk(n), :], qb[blk(i), :], (((1,), (1,)), ((), ())),
                                preferred_element_type=F32)
            s = jnp.where(causal if n == i else sel[n:n + 1, blk(i)], s, -jnp.inf)
            s_scr[hd, i % 2, n] = s
            bm = jnp.max(s, axis=0, keepdims=True)
            mx = bm if mx is None else jnp.maximum(mx, bm)
            yield mx

    def value_pass(i, mx):
        acc = jnp.zeros((HEAD_DIM + DEN_ROWS, MOBA_BLOCK), F32)
        for n in range(i + 1):
            p = jnp.exp2(s_scr[hd, i % 2, n] - mx)
            acc = acc + jnp.dot(vt[:, blk(n)], p.astype(BF16), preferred_element_type=F32)
            yield None
        den = acc[HEAD_DIM:HEAD_DIM + 1, :]
        o_ref[blk(i), lanes] = (acc[:HEAD_DIM, :] / den).T.astype(o_ref.dtype)
        yield None

    mx = list(score_pass(0))[-1]
    for i in range(nblk):
        values = value_pass(i, mx)
        scores = score_pass(i + 1) if i + 1 < nblk else iter(())
        for mx_next in scores:
            mx = mx_next
            next(values, None)
            yield
        for _ in values:
            yield


def _rope_tables(seq):
    half = ROPE_DIM // 2
    inv = jnp.power(ROPE_THETA, -jnp.arange(half, dtype=F32) * (2.0 / ROPE_DIM))
    ang = jnp.arange(seq, dtype=F32)[:, None] * inv[None, :]
    cos, sin = jnp.cos(ang), jnp.sin(ang)
    ones = jnp.ones((seq, HEAD_DIM - ROPE_DIM), F32)
    zeros = jnp.zeros((seq, HEAD_DIM - half), F32)
    cos_t = jnp.concatenate([cos, cos, ones], axis=1)
    sin_t = jnp.concatenate([-sin, sin, zeros[:, half:]], axis=1)
    src = jnp.arange(HEAD_DIM)[:, None]
    dst = jnp.arange(HEAD_DIM)[None, :]
    swap = ((dst < ROPE_DIM) & (src == jnp.where(dst < half, dst + half, dst - half))).astype(BF16)
    return cos_t, sin_t, swap


def _moba_attention(proj3, tables):
    b, seq, _ = proj3.shape
    hps = MOBA_HEADS_PER_STEP
    steps = A_HEADS // hps
    head = lambda off: pl.BlockSpec((None, seq, hps * HEAD_DIM), lambda bi, h: (bi, 0, off + h))
    tab = pl.BlockSpec((seq, HEAD_DIM), lambda bi, h: (0, 0))
    return pl.pallas_call(
        _moba_kernel,
        grid=(b, steps),
        in_specs=[head(0), head(steps), head(2 * steps), tab, tab,
                  pl.BlockSpec((HEAD_DIM, HEAD_DIM), lambda bi, h: (0, 0))],
        out_specs=pl.BlockSpec((None, seq, hps * HEAD_DIM), lambda bi, h: (bi, 0, h)),
        out_shape=jax.ShapeDtypeStruct((b, seq, A_WIDTH), BF16),
        scratch_shapes=[pltpu.VMEM((hps, 2, seq // MOBA_BLOCK, MOBA_BLOCK, MOBA_BLOCK), F32)],
        compiler_params=_cparams(("arbitrary", "arbitrary")),
        name="moba_attention",
    )(proj3, proj3, proj3, *tables)


def _log_sigmoid(z):
    return jnp.minimum(z, 0.0) - jnp.log1p(jnp.exp(-jnp.abs(z)))


def _chunk_cumsum(x):
    pos = lax.broadcasted_iota(jnp.int32, x.shape, 0) % GLA_CHUNK
    shift = 1
    while shift < GLA_CHUNK:
        x = x + jnp.where(pos >= shift, pltpu.roll(x, shift, 0), 0.0)
        shift *= 2
    return x


def _gla_kernel(q_ref, k_ref, v_ref, og_ref, z_ref, gn_ref, o_ref,
                qd_scr, ki_scr, ke_scr, dec_scr, kv_scr, st_scr, oi_scr):
    seq = q_ref.shape[0]
    nc = seq // GLA_CHUNK
    heads = range(GLA_HEADS_PER_STEP)
    kl = lambda hd: slice(hd * G_KEY_DIM, (hd + 1) * G_KEY_DIM)
    vl = lambda hd: slice(hd * G_VAL_DIM, (hd + 1) * G_VAL_DIM)
    for hd in heads:
        log_a = _log_sigmoid(z_ref[:, kl(hd)]) / GATE_TAU
        b = _chunk_cumsum(log_a)
        b3 = b.reshape(nc, GLA_CHUNK, G_KEY_DIM)
        b_last = b3[:, GLA_CHUNK - 1:GLA_CHUNK, :]
        k = k_ref[:, kl(hd)].astype(F32)
        qd_scr[hd] = ((q_ref[:, kl(hd)].astype(F32) * (G_KEY_DIM ** -0.5)) * jnp.exp(b)).astype(BF16)
        ki_scr[hd] = (k * jnp.exp(-b)).astype(BF16)
        ke3 = k.reshape(nc, GLA_CHUNK, G_KEY_DIM) * jnp.exp(b_last - b3)
        ke_scr[hd] = ke3.reshape(seq, G_KEY_DIM).astype(BF16)
        dec_scr[hd] = jnp.exp(b_last)
    group = GLA_GROUP * GLA_CHUNK
    t_i = lax.broadcasted_iota(jnp.int32, (group, group), 0)
    s_i = lax.broadcasted_iota(jnp.int32, (group, group), 1)
    keep = (s_i <= t_i) & ((s_i // GLA_CHUNK) == (t_i // GLA_CHUNK))
    gn = gn_ref[...]

    def chunk_rows(n):
        return pl.ds(pl.multiple_of(n * GLA_CHUNK, GLA_CHUNK), GLA_CHUNK)

    def intra(g, carry):
        rows = pl.ds(pl.multiple_of(g * group, group), group)
        for hd in heads:
            vb = v_ref[rows, vl(hd)].astype(BF16)
            ke = ke_scr[hd, rows, :]
            att = lax.dot_general(qd_scr[hd, rows, :], ki_scr[hd, rows, :], (((1,), (1,)), ((), ())),
                                  preferred_element_type=F32)
            att = jnp.where(keep, att, 0.0).astype(BF16)
            oi_scr[hd, rows, :] = jnp.dot(att, vb, preferred_element_type=F32)
            for c in range(GLA_GROUP):
                cs = slice(c * GLA_CHUNK, (c + 1) * GLA_CHUNK)
                kv_scr[hd, g * GLA_GROUP + c] = lax.dot_general(
                    vb[cs, :], ke[cs, :], (((0,), (0,)), ((), ())), preferred_element_type=F32)
        return carry

    lax.fori_loop(0, nc // GLA_GROUP, intra, 0, unroll=GLA_UNROLL // GLA_GROUP)

    def scan(n, states):
        for hd in heads:
            st_scr[hd, n] = states[hd].astype(BF16)
        return tuple(states[hd] * dec_scr[hd, n] + kv_scr[hd, n] for hd in heads)

    zero_state = jnp.zeros((G_VAL_DIM, G_KEY_DIM), F32)
    lax.fori_loop(0, nc, scan, tuple(zero_state for _ in heads), unroll=GLA_UNROLL)

    def inter(n, carry):
        rows = chunk_rows(n)
        for hd in heads:
            o = oi_scr[hd, rows, :] + lax.dot_general(qd_scr[hd, rows, :], st_scr[hd, n],
                                                      (((1,), (1,)), ((), ())),
                                                      preferred_element_type=F32)
            o = o * lax.rsqrt(jnp.mean(o * o, axis=-1, keepdims=True) + EPS) * gn
            og = og_ref[rows, vl(hd)].astype(F32)
            o_ref[rows, vl(hd)] = (o * (og * jax.nn.sigmoid(og))).astype(o_ref.dtype)
        return carry

    lax.fori_loop(0, nc, inter, 0, unroll=GLA_UNROLL)


GLA_HEADS_PER_STEP = 2
GLA_UNROLL = 16
GLA_GROUP = 4


def _gla_mixer(proj3, z3, g_onorm):
    b, seq, _ = proj3.shape
    kq = 3 * A_WIDTH // G_KEY_DIM
    kv = (3 * A_WIDTH + 2 * G_KEY_WIDTH) // G_VAL_DIM
    hps = GLA_HEADS_PER_STEP
    steps = G_HEADS // hps
    key = lambda off: pl.BlockSpec((None, seq, hps * G_KEY_DIM), lambda bi, h: (bi, 0, off // hps + h))
    val = lambda off: pl.BlockSpec((None, seq, hps * G_VAL_DIM), lambda bi, h: (bi, 0, off // hps + h))
    return pl.pallas_call(
        _gla_kernel,
        grid=(b, steps),
        in_specs=[
            key(kq), key(kq + G_HEADS), val(kv), val(kv + G_HEADS),
            pl.BlockSpec((None, seq, hps * G_KEY_DIM), lambda bi, h: (bi, 0, h)),
            pl.BlockSpec((1, G_VAL_DIM), lambda bi, h: (0, 0)),
        ],
        out_specs=pl.BlockSpec((None, seq, hps * G_VAL_DIM), lambda bi, h: (bi, 0, h)),
        out_shape=jax.ShapeDtypeStruct((b, seq, G_VAL_WIDTH), BF16),
        scratch_shapes=[
            pltpu.VMEM((hps, seq, G_KEY_DIM), BF16),
            pltpu.VMEM((hps, seq, G_KEY_DIM), BF16),
            pltpu.VMEM((hps, seq, G_KEY_DIM), BF16),
            pltpu.VMEM((hps, seq // GLA_CHUNK, 1, G_KEY_DIM), F32),
            pltpu.VMEM((hps, seq // GLA_CHUNK, G_VAL_DIM, G_KEY_DIM), F32),
            pltpu.VMEM((hps, seq // GLA_CHUNK, G_VAL_DIM, G_KEY_DIM), BF16),
            pltpu.VMEM((hps, seq, G_VAL_DIM), F32),
        ],
        compiler_params=_cparams(("arbitrary", "arbitrary")),
        name="gla_mixer",
    )(proj3, proj3, proj3, proj3, z3, g_onorm)


def _first_lane(mask, lane):
    return jnp.min(jnp.where(mask, lane, LANES), axis=-1, keepdims=True)


def _split_bf16(x):
    hi = x.astype(BF16)
    return hi, (x - hi.astype(F32)).astype(BF16)


def _outproj_router_kernel(oa_ref, ob_ref, x_ref, wa_ref, wb_ref, mod_ref, ln_ref, wr_ref, br_ref,
                           xo_ref, h_ref, ri_ref, rw_ref, cnt_ref, run_scr):
    tm = x_ref.shape[0]
    sub = tm // ROUTER_SUBTILES

    @pl.when(pl.program_id(0) == 0)
    def _():
        run_scr[...] = jnp.zeros_like(run_scr)

    lane = lax.broadcasted_iota(jnp.int32, (sub, LANES), 1)
    r_i = lax.broadcasted_iota(jnp.int32, (sub, sub), 0)
    c_i = lax.broadcasted_iota(jnp.int32, (sub, sub), 1)
    strict_lower = jnp.where(c_i < r_i, 1.0, 0.0).astype(BF16)
    state = {"run": run_scr[...]}

    def sub_tile(s):
        rows = slice(s * sub, (s + 1) * sub)
        mix = jnp.dot(oa_ref[rows, :], wa_ref[...], preferred_element_type=F32)
        mix = mix + jnp.dot(ob_ref[rows, :], wb_ref[...], preferred_element_type=F32)
        x_new = x_ref[rows, :] + mod_ref[2:3, :] * mix
        xo_ref[rows, :] = x_new
        h = _modulated_norm(x_new, ln_ref[...], mod_ref[3:4, :], mod_ref[4:5, :])
        h_ref[rows, :] = _pack_bf16_pairs(h)
        yield

        hh = jnp.concatenate(_split_bf16(h), axis=0)
        r = jnp.dot(hh, wr_ref[...], preferred_element_type=F32)
        logits = (r[:sub, :LANES] + r[sub:, :LANES]) + (r[:sub, LANES:] + r[sub:, LANES:]) + br_ref[...]
        is_grp = (lane >= N_EXPERTS) & (lane < N_EXPERTS + N_GROUPS)
        l1 = jnp.where(is_grp, logits, -jnp.inf)
        m1 = jnp.max(l1, axis=-1, keepdims=True)
        grp = _first_lane(l1 == m1, lane) - N_EXPERTS
        p_grp = 1.0 / jnp.sum(jnp.exp(l1 - m1), axis=-1, keepdims=True)
        in_grp = (lane < N_EXPERTS) & ((lane // EXPERTS_PER_GROUP) == grp)
        l2 = jnp.where(in_grp, logits, -jnp.inf)
        va = jnp.max(l2, axis=-1, keepdims=True)
        ia = _first_lane(l2 == va, lane)
        l2b = jnp.where(lane == ia, -jnp.inf, l2)
        vb = jnp.max(l2b, axis=-1, keepdims=True)
        ib = _first_lane(l2b == vb, lane)
        eb = jnp.exp(vb - va)
        wa = p_grp * (1.0 / (1.0 + eb))
        wb = p_grp * (eb / (1.0 + eb))
        rw_ref[rows, :] = jnp.where(lane == 0, wa, 0.0) + jnp.where(lane == 1, wb, 0.0)
        oh_a = jnp.where(lane == ia, 1.0, 0.0)
        oh_b = jnp.where(lane == ib, 1.0, 0.0)
        cnt = oh_a + oh_b
        within = jnp.dot(strict_lower, cnt.astype(BF16), preferred_element_type=F32)
        yield

        before = within + state["run"]
        state["run"] = state["run"] + jnp.sum(cnt, axis=0, keepdims=True)
        rank_a = jnp.sum(oh_a * before, axis=-1, keepdims=True)
        rank_b = jnp.sum(oh_b * before, axis=-1, keepdims=True)
        ri = jnp.where(lane == 0, ia.astype(F32), 0.0) + jnp.where(lane == 1, ib.astype(F32), 0.0)
        ri = ri + jnp.where(lane == 2, rank_a, 0.0) + jnp.where(lane == 3, rank_b, 0.0)
        ri_ref[:, rows] = ri.T[0:ROUTE_ROWS, :].astype(jnp.int32)
        yield

    stages = [sub_tile(s) for s in range(ROUTER_SUBTILES)]
    for _ in range(3):
        for g in stages:
            next(g)

    run_scr[...] = state["run"]
    cnt_ref[...] = jnp.broadcast_to(state["run"], cnt_ref.shape).astype(jnp.int32)


ROUTER_SUBTILES = 4


ROUTE_ROWS = 8
TOKEN_TILE = 512


def _outproj_router(oa2, ob2, x2, w_o, mod, ln, w_r, b_r, seq, layer):
    t, d = x2.shape
    tm = TOKEN_TILE
    row = lambda w: pl.BlockSpec((tm, w), lambda m: (m, 0))
    full = lambda a, c: pl.BlockSpec((a, c), lambda m: (0, 0))
    return pl.pallas_call(
        _outproj_router_kernel,
        grid=(t // tm,),
        in_specs=[
            row(A_WIDTH), row(G_VAL_WIDTH), row(d),
            pl.BlockSpec((None, A_WIDTH, d), lambda m: (layer, 0, 0)),
            pl.BlockSpec((None, G_VAL_WIDTH, d), lambda m: (layer, A_WIDTH // G_VAL_WIDTH, 0)),
            pl.BlockSpec((None, 6, d), lambda m: ((m * tm) // seq, 0, 0)),
            full(1, d), full(d, 2 * LANES), full(1, LANES),
        ],
        out_specs=[row(d), row(d // 2), pl.BlockSpec((None, ROUTE_ROWS, tm), lambda m: (m, 0, 0)),
                   row(LANES), full(8, LANES)],
        out_shape=[
            jax.ShapeDtypeStruct((t, d), F32),
            jax.ShapeDtypeStruct((t, d // 2), jnp.uint32),
            jax.ShapeDtypeStruct((t // tm, ROUTE_ROWS, tm), jnp.int32),
            jax.ShapeDtypeStruct((t, LANES), F32),
            jax.ShapeDtypeStruct((8, LANES), jnp.int32),
        ],
        scratch_shapes=[pltpu.VMEM((1, LANES), F32)],
        compiler_params=_cparams(("arbitrary",)),
        name="outproj_router",
    )(oa2, ob2, x2, w_o, w_o, mod, ln, w_r, b_r)


def _dispatch_kernel(dest_ref, h_ref, xg_in, xg_out, sem):
    del xg_in
    tm = h_ref.shape[0]

    for r in range(tm):
        for k in range(TOP_K):
            pltpu.make_async_copy(h_ref.at[pl.ds(r, 1), :],
                                  xg_out.at[pl.ds(dest_ref[0, k * tm + r], 1), :], sem).start(priority=r % 2)
    done = xg_out.at[pl.ds(0, TOP_K * tm), :]
    pltpu.make_async_copy(done, done, sem).wait()


def _dispatch(dest3, h2p, xg):
    t, dp = h2p.shape
    tm = dest3.shape[2] // TOP_K
    return pl.pallas_call(
        _dispatch_kernel,
        grid=(t // tm,),
        in_specs=[
            pl.BlockSpec((None, 1, TOP_K * tm), lambda m: (m, 0, 0), memory_space=pltpu.SMEM),
            pl.BlockSpec((tm, dp), lambda m: (m, 0)),
            pl.BlockSpec(memory_space=pl.ANY),
        ],
        out_specs=pl.BlockSpec(memory_space=pl.ANY),
        out_shape=jax.ShapeDtypeStruct(xg.shape, xg.dtype),
        input_output_aliases={2: 0},
        scratch_shapes=[pltpu.SemaphoreType.DMA(())],
        compiler_params=_cparams(("arbitrary",)),
        name="moe_dispatch",
    )(dest3, h2p, xg)


def _expert_kernel(cur_ref, nxt_ref, last_ref, nused_ref, x_ref, wg_ref, wu_ref, wd_ref,
                   y_ref, wg_res, wu_res, wd_res, xb_scr, acc_scr):
    del cur_ref, nxt_ref
    b, j = pl.program_id(0), pl.program_id(1)
    active = (b >= 1) & (b <= nused_ref[0])

    @pl.when(active & (j == 0))
    def _():
        lo, hi = _unpack_bf16_pairs(x_ref[...])
        half = lo.shape[1]
        xb_scr[:, :half] = lo.astype(BF16)
        xb_scr[:, half:] = hi.astype(BF16)

    for jj in range(EXPERT_HIDDEN_CHUNKS):
        @pl.when(active & (j == jj))
        def _(jj=jj):
            xb = xb_scr[...]
            g = jnp.dot(xb, wg_res[jj], preferred_element_type=F32)
            u = jnp.dot(xb, wu_res[jj], preferred_element_type=F32)
            a = ((g * jax.nn.sigmoid(g)) * u).astype(BF16)
            part = jnp.dot(a, wd_res[jj], preferred_element_type=F32)
            if jj == 0:
                acc_scr[...] = part
            elif jj < EXPERT_HIDDEN_CHUNKS - 1:
                acc_scr[...] += part
            else:
                y_ref[...] = _pack_bf16_pairs(acc_scr[...] + part)

    @pl.when((b > nused_ref[0]) & (j == 0))
    def _():
        y_ref[...] = jnp.zeros_like(y_ref)

    @pl.when(last_ref[b] == 1)
    def _():
        wg_res[j] = wg_ref[...].astype(BF16)
        wu_res[j] = wu_ref[...].astype(BF16)
        wd_res[j] = wd_ref[...].astype(BF16)


EXPERT_HIDDEN_CHUNKS = 2


def _expert_mlp(cur_e, nxt_e, last, n_used, xg, wg, wu, wd, layer):
    dp = xg.shape[1]
    nb = xg.shape[0] // ROW_BLOCK
    d, f = wg.shape[2], wg.shape[3]
    nj = EXPERT_HIDDEN_CHUNKS
    fc = f // nj

    def x_index(b, j, cur, nxt, lst, nu):
        return (jnp.clip(b - 1, 0, nu[0] - 1), 0)

    def y_index(b, j, cur, nxt, lst, nu):
        return (jnp.maximum(b - 1, 0), 0)

    def w_index(chunk_axis):
        def index(b, j, cur, nxt, lst, nu):
            e = jnp.where(lst[b] == 1, nxt[b], cur[b])
            jj = jnp.where(lst[b] == 1, j, nj - 1)
            return (layer, e, 0, jj) if chunk_axis == 2 else (layer, e, jj, 0)
        return index

    grid_spec = pltpu.PrefetchScalarGridSpec(
        num_scalar_prefetch=4,
        grid=(nb + 1, nj),
        in_specs=[
            pl.BlockSpec((ROW_BLOCK, dp), x_index),
            pl.BlockSpec((None, None, d, fc), w_index(2)),
            pl.BlockSpec((None, None, d, fc), w_index(2)),
            pl.BlockSpec((None, None, fc, d), w_index(1)),
        ],
        out_specs=pl.BlockSpec((ROW_BLOCK, dp), y_index),
        scratch_shapes=[
            pltpu.VMEM((nj, d, fc), BF16),
            pltpu.VMEM((nj, d, fc), BF16),
            pltpu.VMEM((nj, fc, d), BF16),
            pltpu.VMEM((ROW_BLOCK, d), BF16),
            pltpu.VMEM((ROW_BLOCK, d), F32),
        ],
    )
    return pl.pallas_call(
        _expert_kernel,
        grid_spec=grid_spec,
        out_shape=jax.ShapeDtypeStruct((nb * ROW_BLOCK, dp), jnp.uint32),
        compiler_params=_cparams(("arbitrary", "arbitrary"), EXPERT_VMEM_LIMIT),
        name="expert_mlp",
    )(cur_e, nxt_e, last, n_used, xg, wg, wu, wd)


def _wait_row_gather(src_hbm, dst_vmem, sem):
    pltpu.make_async_copy(src_hbm.at[pl.ds(0, dst_vmem.shape[0]), :], dst_vmem, sem).wait()


def _combine_kernel(dest_ref, y_hbm, rw_ref, x_ref, mod_ref, lnf_ref, o_ref, y_scr, sems, *, final):
    tm = x_ref.shape[0]
    i = pl.program_id(0)
    n_tiles = pl.num_programs(0) - COMBINE_LOOKAHEAD
    n_slots = COMBINE_LOOKAHEAD + 1

    def start_gather():
        slot = i % n_slots
        for r in range(2 * tm):
            pltpu.make_async_copy(y_hbm.at[pl.ds(dest_ref[0, r], 1), :],
                                  y_scr.at[slot, pl.ds(r, 1), :], sems.at[slot]).start(priority=r % 2)

    def combine(prefetch):
        slot = (i - COMBINE_LOOKAHEAD) % n_slots
        _wait_row_gather(y_hbm, y_scr.at[slot], sems.at[slot])
        rw = rw_ref[...]
        a_lo, a_hi = _unpack_bf16_pairs(y_scr[slot, 0:tm, :])
        b_lo, b_hi = _unpack_bf16_pairs(y_scr[slot, tm:2 * tm, :])
        y = jnp.concatenate([rw[:, 0:1] * a_lo + rw[:, 1:2] * b_lo,
                             rw[:, 0:1] * a_hi + rw[:, 1:2] * b_hi], axis=1)
        if prefetch:
            start_gather()
        x_new = x_ref[...] + mod_ref[5:6, :] * y
        if final:
            x_new = x_new * lax.rsqrt(jnp.mean(x_new * x_new, axis=-1, keepdims=True) + EPS) * lnf_ref[...]
        o_ref[...] = x_new

    pl.when(i < COMBINE_LOOKAHEAD)(start_gather)
    pl.when((i >= COMBINE_LOOKAHEAD) & (i < n_tiles))(functools.partial(combine, True))
    pl.when(i >= n_tiles)(functools.partial(combine, False))


COMBINE_LOOKAHEAD = 2


def _combine(dest3, yg, rw, x2, mod, ln_f, seq, final):
    t, d = x2.shape
    tm = dest3.shape[2] // 2
    nt = t // tm
    prev = lambda m: jnp.maximum(m - COMBINE_LOOKAHEAD, 0)
    return pl.pallas_call(
        functools.partial(_combine_kernel, final=final),
        grid=(nt + COMBINE_LOOKAHEAD,),
        in_specs=[
            pl.BlockSpec((None, 1, 2 * tm), lambda m: (jnp.minimum(m, nt - 1), 0, 0), memory_space=pltpu.SMEM),
            pl.BlockSpec(memory_space=pl.ANY),
            pl.BlockSpec((tm, LANES), lambda m: (prev(m), 0)),
            pl.BlockSpec((tm, d), lambda m: (prev(m), 0)),
            pl.BlockSpec((None, 6, d), lambda m: ((prev(m) * tm) // seq, 0, 0)),
            pl.BlockSpec((1, d), lambda m: (0, 0)),
        ],
        out_specs=pl.BlockSpec((tm, d), lambda m: (prev(m), 0)),
        out_shape=jax.ShapeDtypeStruct((t, d), F32),
        scratch_shapes=[pltpu.VMEM((COMBINE_LOOKAHEAD + 1, 2 * tm, yg.shape[1]), jnp.uint32),
                        pltpu.SemaphoreType.DMA((COMBINE_LOOKAHEAD + 1,))],
        compiler_params=_cparams(("arbitrary",)),
        name="moe_combine",
    )(dest3, yg, rw, x2, mod, ln_f)


def _routing_tables(ri, counts, t):
    n_blocks = (t * TOP_K) // ROW_BLOCK + N_EXPERTS
    cnt = counts[0, :N_EXPERTS]
    padded = (cnt + ROW_BLOCK - 1) // ROW_BLOCK * ROW_BLOCK
    ends = jnp.cumsum(padded)
    starts = ends - padded
    experts, ranks = ri[:, 0:TOP_K, :], ri[:, TOP_K:2 * TOP_K, :]
    offsets = jnp.sum(jnp.where(experts[..., None] == jnp.arange(N_EXPERTS), starts, 0), axis=-1)
    dest = (offsets + ranks).reshape(ri.shape[0], 1, TOP_K * ri.shape[2])
    n_used = (ends[-1] // ROW_BLOCK).astype(jnp.int32)
    blocks = jnp.minimum(jnp.arange(n_blocks + 1, dtype=jnp.int32), n_used - 1) * ROW_BLOCK
    block_exp = jnp.sum((blocks[:, None] >= ends[None, :]).astype(jnp.int32), axis=1)
    block_exp = jnp.minimum(block_exp, N_EXPERTS - 1)
    cur_e = jnp.concatenate([block_exp[:1], block_exp[:-1]])
    nxt_e = block_exp
    last = (cur_e != nxt_e).astype(jnp.int32).at[0].set(1)
    return dest.astype(jnp.int32), cur_e, nxt_e, last, n_used.reshape(1)


def kernel(x, c, ln1, ln2, w_ada, b_ada, w_in, w_gk, b_gk, g_onorm, w_out,
           w_r1, b_r1, w_r2, b_r2, w_e_gate, w_e_up, w_e_down, ln_f):
    b, seq, d = x.shape
    depth = w_ada.shape[0]
    t = b * seq
    mod_all = _adaln_mod(c, w_ada, b_ada).reshape(depth, b, 6, d)
    tables = _rope_tables(seq)
    x2 = x.reshape(t, d)
    xg = jnp.zeros(((t * TOP_K // ROW_BLOCK + N_EXPERTS) * ROW_BLOCK, d // 2), jnp.uint32)
    w_in_b = w_in.astype(BF16)
    w_o = w_out.astype(BF16)
    for l in range(depth):
        mod = mod_all[l]
        w_ga = jnp.pad(w_in_b[l][:, PROJ_MAIN:], ((0, 0), (0, LANES - GATE_RANK)))
        w_gk_p = jnp.pad(w_gk[l], ((0, LANES - GATE_RANK), (0, 0)))
        proj, z = _in_projection(x2, ln1[l][None, :], mod, w_in_b, w_ga, w_gk_p, b_gk[l][None, :], seq, l)
        proj3 = proj.reshape(b, seq, PROJ_MAIN)
        oa = _moba_attention(proj3, tables)
        ob = _gla_mixer(proj3, z.reshape(b, seq, G_KEY_WIDTH), g_onorm[l][None, :])
        w_r = jnp.pad(jnp.concatenate([w_r2[l], w_r1[l]], axis=1), ((0, 0), (0, LANES - N_EXPERTS - N_GROUPS)))
        w_r_hi = w_r.astype(BF16)
        w_r = jnp.concatenate([w_r_hi, (w_r - w_r_hi.astype(F32)).astype(BF16)], axis=1)
        b_r = jnp.pad(jnp.concatenate([b_r2[l], b_r1[l]]), (0, LANES - N_EXPERTS - N_GROUPS))[None, :]
        x2, h2, ri, rw, counts = _outproj_router(
            oa.reshape(t, A_WIDTH), ob.reshape(t, G_VAL_WIDTH), x2, w_o, mod, ln2[l][None, :], w_r, b_r, seq, l)
        dest, cur_e, nxt_e, last, n_used = _routing_tables(ri, counts, t)
        xg = _dispatch(dest, h2, xg)
        yg = _expert_mlp(cur_e, nxt_e, last, n_used, xg, w_e_gate, w_e_up, w_e_down, l)
        x2 = _combine(dest, yg, rw, x2, mod, ln_f[None, :], seq, final=(l == depth - 1))
    return x2.reshape(b, seq, d)
```
